```python
import math
import jax, jax.numpy as jnp
from jax import lax
import numpy as np

D_MODEL = 1024
BATCH = 8
SEQ = 4096
DEPTH = 2

N_MIXERS = 2
HEAD_DIM = 64
DIFF_HEADS = D_MODEL // (2 * HEAD_DIM)
DIFF_V_DIM = 2 * HEAD_DIM
DIFF_LAMBDA_STD = 0.1
MOBA_HEADS = D_MODEL // HEAD_DIM
MOBA_BLOCK = 256
MOBA_TOPK = 3
MOBA_Q_CHUNK = 32
Q_BLOCK = 128
REL_BUCKETS = 32
REL_MAX_DIST = 128
N_BIAS_COLS = MOBA_HEADS
D_FF = int(math.ceil(8 * D_MODEL / 3 / 256)) * 256
FFN_RESIDUAL = 0.5
RMS_EPS = 1e-6
SUBLN_EPS = 1e-5

kernel_name = "hybrid_diffattn_moba_macaron"


def rms_norm(x, g, eps=RMS_EPS):
    xf = x.astype(jnp.float32)
    y = xf * lax.rsqrt(jnp.mean(xf * xf, axis=-1, keepdims=True) + eps)
    return (y * g.astype(jnp.float32)).astype(x.dtype)


def swiglu(h, w_in, w_out):
    g, u = jnp.split(h @ w_in, 2, axis=-1)
    return (jax.nn.silu(g) * u) @ w_out


def rel_bucket(dist):
    n = jnp.maximum(dist, 0)
    max_exact = REL_BUCKETS // 2
    nf = jnp.maximum(n, 1).astype(jnp.float32)
    large = max_exact + (jnp.log(nf / max_exact) / math.log(REL_MAX_DIST / max_exact)
                         * (REL_BUCKETS - max_exact)).astype(jnp.int32)
    large = jnp.minimum(large, REL_BUCKETS - 1)
    return jnp.where(n < max_exact, n, large)


def diff_attention(h, w_qkv, lam_params, subln_g, w_o, rel_bias, layer_idx):
    B, S, _ = h.shape
    H, d = DIFF_HEADS, HEAD_DIM
    q, k, v = jnp.split(h @ w_qkv, 3, axis=-1)
    q = q.reshape(B, S, 2 * H, d).transpose(0, 2, 1, 3)
    k = k.reshape(B, S, 2 * H, d).transpose(0, 2, 1, 3)
    v = v.reshape(B, S, H, DIFF_V_DIM).transpose(0, 2, 1, 3)
    lp = lam_params.astype(jnp.float32)
    lam_init = 0.8 - 0.6 * math.exp(-0.3 * layer_idx)
    lam = jnp.exp(jnp.sum(lp[0] * lp[1])) - jnp.exp(jnp.sum(lp[2] * lp[3])) + lam_init
    n_qb = S // Q_BLOCK
    q_blocks = q.reshape(B, 2 * H, n_qb, Q_BLOCK, d).transpose(2, 0, 1, 3, 4)
    k_pos = jnp.arange(S)
    scale = d ** -0.5

    def block(args):
        qb_idx, qb = args
        q_pos = qb_idx * Q_BLOCK + jnp.arange(Q_BLOCK)
        dist = q_pos[:, None] - k_pos[None, :]
        bias = jnp.moveaxis(rel_bias[rel_bucket(dist)], -1, 0).astype(jnp.float32)
        s = jnp.einsum('bmqd,bmkd->bmqk', qb, k).astype(jnp.float32) * scale + bias
        s = jnp.where(dist >= 0, s, -jnp.inf)
        p = jax.nn.softmax(s, axis=-1).reshape(B, H, 2, Q_BLOCK, S)
        a = p[:, :, 0] - lam * p[:, :, 1]
        return jnp.einsum('bhqk,bhkv->bhqv', a.astype(v.dtype), v)

    o = lax.map(block, (jnp.arange(n_qb), q_blocks))
    o = o.transpose(1, 2, 0, 3, 4).reshape(B, H, S, DIFF_V_DIM)
    o = rms_norm(o, subln_g, SUBLN_EPS) * (1.0 - lam_init)
    o = o.transpose(0, 2, 1, 3).reshape(B, S, H * DIFF_V_DIM)
    return o @ w_o


def moba_attention(h, w_qkv, w_o, rel_bias):
    B, S, _ = h.shape
    H, d, L, C = MOBA_HEADS, HEAD_DIM, MOBA_BLOCK, MOBA_Q_CHUNK
    q, k, v = jnp.split(h @ w_qkv, 3, axis=-1)
    q = q.reshape(B, S, H, d).transpose(0, 2, 1, 3)
    k = k.reshape(B, S, H, d).transpose(0, 2, 1, 3)
    v = v.reshape(B, S, H, d).transpose(0, 2, 1, 3)
    n_blk = -(-S // L)
    S_pad = n_blk * L
    pad = ((0, 0), (0, 0), (0, S_pad - S), (0, 0))
    q, k, v = jnp.pad(q, pad), jnp.pad(k, pad), jnp.pad(v, pad)
    kb = k.reshape(B, H, n_blk, L, d)
    vb = v.reshape(B, H, n_blk, L, d)
    k_mean = jnp.mean(kb.astype(jnp.float32), axis=3)
    gate = jnp.einsum('bhsd,bhnd->bhsn', q.astype(jnp.float32), k_mean)
    q_blk = jnp.arange(S_pad) // L
    eligible = jnp.arange(n_blk)[None, :] < q_blk[:, None]
    gate = jnp.where(eligible, gate, -jnp.inf)
    n_sel = min(MOBA_TOPK, n_blk)
    _, sel = lax.top_k(gate, n_sel)
    valid = sel < q_blk[:, None]
    n_ch = S_pad // C

    def to_chunks(a):
        return a.reshape(B, H, n_ch, C, a.shape[-1]).transpose(2, 0, 1, 3, 4)

    b_idx = jnp.arange(B)[:, None, None, None]
    h_idx = jnp.arange(H)[None, :, None, None]
    bias_t = rel_bias.T.astype(jnp.float32)
    scale = d ** -0.5

    def chunk(args):
        c, qc, selc, validc = args
        q_pos = c * C + jnp.arange(C)
        own = (c * C) // L
        kg = kb[b_idx, h_idx, selc]
        vg = vb[b_idx, h_idx, selc]
        kpos_past = selc[..., None] * L + jnp.arange(L)
        bias_past = bias_t[h_idx[..., None], rel_bucket(q_pos[:, None, None] - kpos_past)]
        s_past = jnp.einsum('bhqd,bhqnld->bhqnl', qc, kg).astype(jnp.float32) * scale + bias_past
        s_past = jnp.where(validc[..., None], s_past, -jnp.inf)
        ko = lax.dynamic_index_in_dim(kb, own, axis=2, keepdims=False)
        vo = lax.dynamic_index_in_dim(vb, own, axis=2, keepdims=False)
        dist_own = q_pos[:, None] - (own * L + jnp.arange(L))[None, :]
        bias_own = jnp.moveaxis(bias_t.T[rel_bucket(dist_own)], -1, 0)
        s_own = jnp.einsum('bhqd,bhld->bhql', qc, ko).astype(jnp.float32) * scale + bias_own
        s_own = jnp.where(dist_own >= 0, s_own, -jnp.inf)
        s = jnp.concatenate([s_past.reshape(B, H, C, n_sel * L), s_own], axis=-1)
        p = jax.nn.softmax(s, axis=-1).astype(v.dtype)
        p_past = p[..., :n_sel * L].reshape(B, H, C, n_sel, L)
        p_own = p[..., n_sel * L:]
        return (jnp.einsum('bhqnl,bhqnld->bhqd', p_past, vg)
                + jnp.einsum('bhql,bhld->bhqd', p_own, vo))

    o = lax.map(chunk, (jnp.arange(n_ch), to_chunks(q), to_chunks(sel), to_chunks(valid)))
    o = o.transpose(1, 0, 3, 2, 4).reshape(B, S_pad, H * d)[:, :S]
    return o @ w_o


def setup_inputs(seed: int = 0) -> dict:
    key = jax.random.key(seed)
    ks = jax.random.split(key, 12)
    D, F = D_MODEL, D_FF
    n_a = len(range(0, DEPTH, N_MIXERS))
    n_b = len(range(1, DEPTH, N_MIXERS))
    nrm = jax.random.normal
    f32 = jnp.float32
    return {
        'x': nrm(ks[0], (BATCH, SEQ, D), f32),
        'rel_bias': 0.2 * nrm(ks[1], (REL_BUCKETS, N_BIAS_COLS), f32),
        'norm_g': 1.0 + 0.02 * nrm(ks[2], (DEPTH, 3, D), f32),
        'final_norm_g': 1.0 + 0.02 * nrm(ks[3], (D,), f32),
        'ffn_w_in': nrm(ks[4], (DEPTH, 2, D, 2 * F), f32) * D ** -0.5,
        'ffn_w_out': nrm(ks[5], (DEPTH, 2, F, D), f32) * F ** -0.5,
        'diff_w_qkv': nrm(ks[6], (n_a, D, 3 * DIFF_HEADS * DIFF_V_DIM), f32) * D ** -0.5,
        'diff_lambda': DIFF_LAMBDA_STD * nrm(ks[7], (n_a, 4, HEAD_DIM), f32),
        'diff_subln_g': 1.0 + 0.02 * nrm(ks[8], (n_a, DIFF_V_DIM), f32),
        'diff_w_o': nrm(ks[9], (n_a, DIFF_HEADS * DIFF_V_DIM, D), f32) * (DIFF_HEADS * DIFF_V_DIM) ** -0.5,
        'moba_w_qkv': nrm(ks[10], (n_b, D, 3 * MOBA_HEADS * HEAD_DIM), f32) * D ** -0.5,
        'moba_w_o': nrm(ks[11], (n_b, MOBA_HEADS * HEAD_DIM, D), f32) * (MOBA_HEADS * HEAD_DIM) ** -0.5,
    }


def reference(x, rel_bias, norm_g, final_norm_g, ffn_w_in, ffn_w_out, diff_w_qkv, diff_lambda,
              diff_subln_g, diff_w_o, moba_w_qkv, moba_w_o):
    h = x
    for i in range(DEPTH):
        g = norm_g[i]
        h = h + FFN_RESIDUAL * swiglu(rms_norm(h, g[0]), ffn_w_in[i, 0], ffn_w_out[i, 0])
        hn = rms_norm(h, g[1])
        j = i // N_MIXERS
        if i % N_MIXERS == 0:
            mix = diff_attention(hn, diff_w_qkv[j], diff_lambda[j], diff_subln_g[j], diff_w_o[j], rel_bias, i)
        else:
            mix = moba_attention(hn, moba_w_qkv[j], moba_w_o[j], rel_bias)
        h = h + mix
        h = h + FFN_RESIDUAL * swiglu(rms_norm(h, g[2]), ffn_w_in[i, 1], ffn_w_out[i, 1])
    return rms_norm(h, final_norm_g)
```

```python
import functools
import math

import numpy as np
import jax
import jax.numpy as jnp
from jax import lax
from jax.experimental import pallas as pl
from jax.experimental.pallas import tpu as pltpu

HEAD_DIM = 64
PAIR = 2 * HEAD_DIM
MOBA_BLOCK = 256
MOBA_TOPK = 3
REL_BUCKETS = 32
REL_MAX_DIST = 128
FFN_RESIDUAL = 0.5
RMS_EPS = 1e-6
SUBLN_EPS = 1e-5

ATT_TILE = 256
FFN_CHUNK = 256
TOKEN_TILE = 512
MASK_VALUE = -1e30
VMEM_LIMIT_BYTES = 56 * 1024 * 1024

F32 = jnp.float32
BF16 = jnp.bfloat16
_NT = (((1,), (1,)), ((), ()))


def _rms(x, g, eps):
    return x * lax.rsqrt(jnp.mean(x * x, axis=-1, keepdims=True) + eps) * g


def _const_spec(shape):
    return pl.BlockSpec(shape, lambda *_: (0,) * len(shape), pipeline_mode=pl.Buffered(1))


def _rel_bucket_np(dist):
    n = np.maximum(dist, 0)
    max_exact = REL_BUCKETS // 2
    nf = np.maximum(n, 1).astype(np.float32)
    large = max_exact + (np.log(nf / np.float32(max_exact)) / np.float32(math.log(REL_MAX_DIST / max_exact))
                         * np.float32(REL_BUCKETS - max_exact)).astype(np.int32)
    large = np.minimum(large, REL_BUCKETS - 1)
    return np.where(n < max_exact, n, large).astype(np.int32)


def _bucket_tiles(t):
    j = np.arange(t)[:, None]
    i = np.arange(t)[None, :]
    diag = np.where(i - j >= 0, _rel_bucket_np(i - j), -1)
    prev = _rel_bucket_np(i - j + t)
    assert _rel_bucket_np(np.arange(t + 1, 8 * t)).min() == REL_BUCKETS - 1
    return np.stack([diag, prev]).astype(np.int32)


def _bias_kernel(rb_ref, idx_ref, out_ref):
    m = pl.program_id(0)
    far = rb_ref[REL_BUCKETS - 1, m]
    for t in range(2):
        idx = idx_ref[t]
        acc = jnp.zeros(idx.shape, F32)
        for b in range(REL_BUCKETS - 1):
            acc = jnp.where(idx == b, rb_ref[b, m] - far, acc)
        out_ref[0, t] = jnp.where(idx < 0, MASK_VALUE, acc)


def _bias_tiles(rel_bias, t):
    n_maps = rel_bias.shape[1]
    idx = jnp.asarray(_bucket_tiles(t))
    return pl.pallas_call(
        _bias_kernel,
        grid=(n_maps,),
        in_specs=[pl.BlockSpec(memory_space=pltpu.SMEM),
                  pl.BlockSpec((2, t, t), lambda m: (0, 0, 0))],
        out_specs=pl.BlockSpec((1, 2, t, t), lambda m: (m, 0, 0, 0)),
        out_shape=jax.ShapeDtypeStruct((n_maps, 2, t, t), F32),
        name="rel_bias_tiles",
    )(rel_bias.astype(F32), idx)


def _ffn_kernel(*refs, n_chunks, has_proj, has_final):
    refs = list(refs)
    x_ref = refs.pop(0)
    if has_proj:
        o_ref, wo_ref = refs.pop(0), refs.pop(0)
    g_ref, win_ref, wout_ref = refs.pop(0), refs.pop(0), refs.pop(0)
    if has_final:
        gf_ref = refs.pop(0)
    (out_ref,) = refs

    x = x_ref[...]
    if has_proj:
        x = x + jnp.dot(o_ref[...], wo_ref[...], preferred_element_type=F32)
    hn = _rms(x, g_ref[...], RMS_EPS).astype(BF16)
    acc = None
    for c in range(n_chunks):
        gu = jnp.dot(hn, win_ref[c], preferred_element_type=F32)
        gate, up = gu[:, :FFN_CHUNK], gu[:, FFN_CHUNK:]
        a = (gate * (1.0 / (1.0 + jnp.exp(-gate))) * up).astype(BF16)
        part = jnp.dot(a, wout_ref[c], preferred_element_type=F32)
        acc = part if acc is None else acc + part
    y = x + FFN_RESIDUAL * acc
    if has_final:
        y = _rms(y, gf_ref[...], RMS_EPS)
    out_ref[...] = y


def _ffn(x2d, g, w_in, w_out, proj=None, final_g=None):
    T, D = x2d.shape
    F = w_out.shape[0]
    n_chunks = F // FFN_CHUNK
    assert n_chunks * FFN_CHUNK == F and T % TOKEN_TILE == 0
    tm = TOKEN_TILE
    w_in_c = jnp.concatenate([w_in[:, :F].reshape(D, n_chunks, FFN_CHUNK),
                              w_in[:, F:].reshape(D, n_chunks, FFN_CHUNK)], axis=-1)
    w_in_c = w_in_c.transpose(1, 0, 2).astype(BF16)
    w_out_c = w_out.reshape(n_chunks, FFN_CHUNK, D).astype(BF16)

    row = lambda i: (i, 0)
    args, specs = [x2d], [pl.BlockSpec((tm, D), row)]
    if proj is not None:
        o2d, w_o = proj
        args += [o2d, w_o.astype(BF16)]
        specs += [pl.BlockSpec((tm, o2d.shape[1]), row), _const_spec(w_o.shape)]
    args += [g.reshape(1, D).astype(F32), w_in_c, w_out_c]
    specs += [_const_spec((1, D)), _const_spec(w_in_c.shape), _const_spec(w_out_c.shape)]
    if final_g is not None:
        args.append(final_g.reshape(1, D).astype(F32))
        specs.append(_const_spec((1, D)))

    return pl.pallas_call(
        functools.partial(_ffn_kernel, n_chunks=n_chunks, has_proj=proj is not None,
                          has_final=final_g is not None),
        grid=(T // tm,),
        in_specs=specs,
        out_specs=pl.BlockSpec((tm, D), row),
        out_shape=jax.ShapeDtypeStruct((T, D), F32),
        compiler_params=pltpu.CompilerParams(dimension_semantics=("parallel",),
                                             vmem_limit_bytes=VMEM_LIMIT_BYTES),
        name="ffn",
    )(*args)


def _proj_kernel(x_ref, g_ref, wqT_ref, wk_ref, wvT_ref, qT_ref, k_ref, vT_ref, *, n_pairs, n_sub):
    hn = _rms(x_ref[0], g_ref[...], RMS_EPS).astype(BF16)
    k_ref[0] = jnp.dot(hn, wk_ref[...], preferred_element_type=F32).astype(BF16)
    qT_ref[0] = lax.dot_general(wqT_ref[...], hn, _NT, preferred_element_type=F32).astype(BF16)
    vT = lax.dot_general(wvT_ref[...], hn, _NT, preferred_element_type=F32).astype(BF16)
    for p in range(n_pairs):
        for c in range(n_sub):
            vT_ref[0, p, c] = vT[p * PAIR:(p + 1) * PAIR, c * ATT_TILE:(c + 1) * ATT_TILE]


def _qkv_proj(h, g, w_qkv):
    B, S, D = h.shape
    tm = TOKEN_TILE
    n_pairs, n_sub, nk = D // PAIR, tm // ATT_TILE, S // ATT_TILE
    wq, wk, wv = w_qkv[:, :D], w_qkv[:, D:2 * D], w_qkv[:, 2 * D:]
    wqT = (wq * HEAD_DIM ** -0.5).T.astype(BF16)
    wvT = wv.T.astype(BF16)
    return pl.pallas_call(
        functools.partial(_proj_kernel, n_pairs=n_pairs, n_sub=n_sub),
        grid=(B, S // tm),
        in_specs=[pl.BlockSpec((1, tm, D), lambda b, s: (b, s, 0)),
                  _const_spec((1, D)), _const_spec((D, D)), _const_spec((D, D)), _const_spec((D, D))],
        out_specs=[pl.BlockSpec((1, D, tm), lambda b, s: (b, 0, s)),
                   pl.BlockSpec((1, tm, D), lambda b, s: (b, s, 0)),
                   pl.BlockSpec((1, n_pairs, n_sub, PAIR, ATT_TILE), lambda b, s: (b, 0, s, 0, 0))],
        out_shape=[jax.ShapeDtypeStruct((B, D, S), BF16),
                   jax.ShapeDtypeStruct((B, S, D), BF16),
                   jax.ShapeDtypeStruct((B, n_pairs, nk, PAIR, ATT_TILE), BF16)],
        compiler_params=pltpu.CompilerParams(dimension_semantics=("parallel", "parallel"),
                                             vmem_limit_bytes=VMEM_LIMIT_BYTES),
        name="qkv_proj",
    )(h, g.reshape(1, D).astype(F32), wqT, wk.astype(BF16), wvT)


def _split_pair(qT_ref, qz_ref):
    q = qT_ref[0]
    zeros = jnp.zeros((HEAD_DIM, q.shape[1]), q.dtype)
    qz_ref[0, :HEAD_DIM, :] = q[:HEAD_DIM]
    qz_ref[0, HEAD_DIM:, :] = zeros
    qz_ref[1, :HEAD_DIM, :] = zeros
    qz_ref[1, HEAD_DIM:, :] = q[HEAD_DIM:]


def _tile_update(i, k_t, vT_t, bias, qz_ref, m_ref, l_ref, acc_ref, first):
    s = jnp.dot(k_t, qz_ref[i], preferred_element_type=F32)
    if bias is not None:
        s = s + bias
    m_cur = jnp.max(s, axis=0, keepdims=True)
    if first:
        m_new = m_cur
    else:
        m_old = m_ref[i]
        m_new = jnp.maximum(m_old, m_cur)
    p = jnp.exp(s - m_new)
    l_cur = jnp.sum(p, axis=0, keepdims=True)
    pv = jnp.dot(vT_t, p.astype(BF16), preferred_element_type=F32)
    if first:
        l_ref[i] = l_cur
        acc_ref[i] = pv
    else:
        alpha = jnp.exp(m_old - m_new)
        l_ref[i] = alpha * l_ref[i] + l_cur
        acc_ref[i] = alpha * acc_ref[i] + pv
    m_ref[i] = m_new


def _k_tile(k_ref, j):
    return k_ref[0, pl.ds(pl.multiple_of(j * ATT_TILE, ATT_TILE), ATT_TILE), :]


def _diff_kernel(qT_ref, k_ref, vT_ref, bias_ref, lam_ref, g_ref, o_ref,
                 qz_ref, m_ref, l_ref, acc_ref, *, lam_init):
    qi = pl.program_id(2)
    _split_pair(qT_ref, qz_ref)
    state = (qz_ref, m_ref, l_ref, acc_ref)

    kd, vd = _k_tile(k_ref, qi), vT_ref[0, 0, qi]
    for i in range(2):
        _tile_update(i, kd, vd, bias_ref[i, 0], *state, first=True)

    @pl.when(qi >= 1)
    def _():
        ka, va = _k_tile(k_ref, qi - 1), vT_ref[0, 0, qi - 1]
        for i in range(2):
            _tile_update(i, ka, va, bias_ref[i, 1], *state, first=False)

    def far(j, carry):
        kt, vt = _k_tile(k_ref, j), vT_ref[0, 0, j]
        for i in range(2):
            _tile_update(i, kt, vt, None, *state, first=False)
        return carry

    lax.fori_loop(0, jnp.maximum(qi - 1, 0), far, 0)

    lp = lam_ref[...]
    lam = (jnp.exp(jnp.sum(lp[0:1] * lp[1:2], axis=-1, keepdims=True))
           - jnp.exp(jnp.sum(lp[2:3] * lp[3:4], axis=-1, keepdims=True)) + lam_init)
    o = acc_ref[0] * (1.0 / l_ref[0]) - lam * (acc_ref[1] * (1.0 / l_ref[1]))
    o = o * lax.rsqrt(jnp.mean(o * o, axis=0, keepdims=True) + SUBLN_EPS) * g_ref[...] * (1.0 - lam_init)
    o_ref[0] = o.T.astype(BF16)


def _moba_kernel(qT_ref, k_ref, vT_ref, bias_ref, o_ref,
                 qz_ref, m_ref, l_ref, acc_ref, kmean_ref, selb_ref, *, nk):
    qi = pl.program_id(2)
    tq = qT_ref.shape[2]

    @pl.when(qi == 0)
    def _():
        for j in range(nk):
            kb = k_ref[0, j * MOBA_BLOCK:(j + 1) * MOBA_BLOCK, :].astype(F32)
            kmean_ref[j:j + 1, :] = jnp.mean(kb, axis=0, keepdims=True)

    _split_pair(qT_ref, qz_ref)
    blk = lax.broadcasted_iota(jnp.int32, (nk, tq), 0)
    eligible = blk < qi
    for i in range(2):
        gate = jnp.dot(kmean_ref[...], qz_ref[i].astype(F32), preferred_element_type=F32,
                       precision=lax.Precision.HIGHEST)
        gate_m = jnp.where(eligible, gate, -jnp.inf)
        n_better = jnp.zeros((nk, tq), F32)
        for c in range(nk):
            gc = gate_m[c:c + 1, :]
            better = jnp.where(gc > gate, 1.0, jnp.where((gc == gate) & (c < blk), 1.0, 0.0))
            n_better = n_better + better
        selected = eligible & (n_better < MOBA_TOPK)
        selb_ref[i] = jnp.where(selected, 0.0, MASK_VALUE)

    state = (qz_ref, m_ref, l_ref, acc_ref)
    half = lambda v, i: v[i * HEAD_DIM:(i + 1) * HEAD_DIM]

    kd, vd = _k_tile(k_ref, qi), vT_ref[0, 0, qi]
    for i in range(2):
        _tile_update(i, kd, half(vd, i), bias_ref[i, 0], *state, first=True)

    @pl.when(qi >= 1)
    def _():
        ka, va = _k_tile(k_ref, qi - 1), vT_ref[0, 0, qi - 1]
        for i in range(2):
            bias = bias_ref[i, 1] + selb_ref[i, pl.ds(qi - 1, 1), :]
            _tile_update(i, ka, half(va, i), bias, *state, first=False)

    def far(j, carry):
        kt, vt = _k_tile(k_ref, j), vT_ref[0, 0, j]
        for i in range(2):
            _tile_update(i, kt, half(vt, i), selb_ref[i, pl.ds(j, 1), :], *state, first=False)
        return carry

    lax.fori_loop(0, jnp.maximum(qi - 1, 0), far, 0)

    o = jnp.concatenate([acc_ref[i] * (1.0 / l_ref[i]) for i in range(2)], axis=0)
    o_ref[0] = o.T.astype(BF16)


def _attention(kind, qT, k, vT, bias, extra, *, lam_init=None):
    B, D, S = qT.shape
    n_pairs, nk, t = D // PAIR, S // ATT_TILE, ATT_TILE
    in_specs = [pl.BlockSpec((1, PAIR, t), lambda b, p, q: (b, p, q)),
                pl.BlockSpec((1, S, PAIR), lambda b, p, q: (b, 0, p)),
                pl.BlockSpec((1, 1, nk, PAIR, t), lambda b, p, q: (b, p, 0, 0, 0)),
                pl.BlockSpec((2, 2, t, t), lambda b, p, q: (p, 0, 0, 0))]
    scratch = [pltpu.VMEM((2, PAIR, t), BF16),
               pltpu.VMEM((2, 1, t), F32),
               pltpu.VMEM((2, 1, t), F32)]
    if kind == "diff":
        body = functools.partial(_diff_kernel, lam_init=lam_init)
        in_specs += [pl.BlockSpec(e.shape, lambda b, p, q: (0, 0)) for e in extra]
        scratch += [pltpu.VMEM((2, PAIR, t), F32)]
    else:
        body = functools.partial(_moba_kernel, nk=nk)
        scratch += [pltpu.VMEM((2, HEAD_DIM, t), F32),
                    pltpu.VMEM((nk, PAIR), F32),
                    pltpu.VMEM((2, nk, t), F32)]
    return pl.pallas_call(
        body,
        grid=(B, n_pairs, nk),
        in_specs=in_specs,
        out_specs=pl.BlockSpec((1, t, PAIR), lambda b, p, q: (b, q, p)),
        out_shape=jax.ShapeDtypeStruct((B, S, D), BF16),
        scratch_shapes=scratch,
        compiler_params=pltpu.CompilerParams(dimension_semantics=("parallel", "parallel", "arbitrary"),
                                             vmem_limit_bytes=VMEM_LIMIT_BYTES),
        name=kind + "_attention",
    )(qT, k, vT, bias, *extra)


def kernel(x, rel_bias, norm_g, final_norm_g, ffn_w_in, ffn_w_out, diff_w_qkv, diff_lambda,
           diff_subln_g, diff_w_o, moba_w_qkv, moba_w_o):
    B, S, D = x.shape
    depth = norm_g.shape[0]
    assert S % TOKEN_TILE == 0 and D % PAIR == 0 and MOBA_BLOCK == ATT_TILE
    assert rel_bias.shape == (REL_BUCKETS, D // HEAD_DIM)
    bias = _bias_tiles(rel_bias, ATT_TILE)

    h = x.reshape(B * S, D)
    for i in range(depth):
        g = norm_g[i]
        h = _ffn(h, g[0], ffn_w_in[i, 0], ffn_w_out[i, 0])
        j = i // 2
        if i % 2 == 0:
            qT, k, vT = _qkv_proj(h.reshape(B, S, D), g[1], diff_w_qkv[j])
            lam_init = 0.8 - 0.6 * math.exp(-0.3 * i)
            g_sub = jnp.broadcast_to(diff_subln_g[j].astype(F32)[:, None], (PAIR, ATT_TILE))
            o = _attention("diff", qT, k, vT, bias, (diff_lambda[j].astype(F32), g_sub), lam_init=lam_init)
            w_o = diff_w_o[j]
        else:
            qT, k, vT = _qkv_proj(h.reshape(B, S, D), g[1], moba_w_qkv[j])
            o = _attention("moba", qT, k, vT, bias, ())
            w_o = moba_w_o[j]
        last = i == depth - 1
        h = _ffn(h, g[2], ffn_w_in[i, 1], ffn_w_out[i, 1], proj=(o.reshape(B * S, D), w_o),
                 final_g=final_norm_g if last else None)
    return h.reshape(B, S, D)
```

```python
import functools
import math

import numpy as np
import jax
import jax.numpy as jnp
from jax import lax
from jax.experimental import pallas as pl
from jax.experimental.pallas import tpu as pltpu

HEAD_DIM = 64
PAIR = 2 * HEAD_DIM
MOBA_BLOCK = 256
MOBA_TOPK = 3
REL_BUCKETS = 32
REL_MAX_DIST = 128
FFN_RESIDUAL = 0.5
RMS_EPS = 1e-6
SUBLN_EPS = 1e-5

ATT_TILE = 256
PAIRS_PER_STEP = 8
QK_AHEAD = 4
FFN_CHUNK = 256
TOKEN_TILE = 512
MASK_VALUE = -1e30
LOG2E = math.log2(math.e)
VMEM_LIMIT_BYTES = 56 * 1024 * 1024

F32 = jnp.float32
BF16 = jnp.bfloat16
_NT = (((1,), (1,)), ((), ()))


def _rms(x, g, eps):
    return x * lax.rsqrt(jnp.mean(x * x, axis=-1, keepdims=True) + eps) * g


def _const_spec(shape):
    return pl.BlockSpec(shape, lambda *_: (0,) * len(shape), pipeline_mode=pl.Buffered(1))


def _rel_bucket_np(dist):
    n = np.maximum(dist, 0)
    max_exact = REL_BUCKETS // 2
    nf = np.maximum(n, 1).astype(np.float32)
    large = max_exact + (np.log(nf / np.float32(max_exact)) / np.float32(math.log(REL_MAX_DIST / max_exact))
                         * np.float32(REL_BUCKETS - max_exact)).astype(np.int32)
    large = np.minimum(large, REL_BUCKETS - 1)
    return np.where(n < max_exact, n, large).astype(np.int32)


def _bucket_tiles(t):
    j = np.arange(t)[:, None]
    i = np.arange(t)[None, :]
    diag = np.where(i - j >= 0, _rel_bucket_np(i - j), -1)
    prev = _rel_bucket_np(i - j + t)
    assert _rel_bucket_np(np.arange(t + 1, 8 * t)).min() == REL_BUCKETS - 1
    return np.stack([diag, prev]).astype(np.int32)


def _bias_kernel(rb_ref, idx_ref, out_ref):
    m = pl.program_id(0)
    far = rb_ref[REL_BUCKETS - 1, m]
    for t in range(2):
        idx = idx_ref[t]
        acc = jnp.zeros(idx.shape, F32)
        for b in range(REL_BUCKETS - 1):
            acc = jnp.where(idx == b, (rb_ref[b, m] - far) * LOG2E, acc)
        out_ref[0, t] = jnp.where(idx < 0, MASK_VALUE, acc)


def _bias_tiles(rel_bias, t):
    n_maps = rel_bias.shape[1]
    idx = jnp.asarray(_bucket_tiles(t))
    return pl.pallas_call(
        _bias_kernel,
        grid=(n_maps,),
        in_specs=[pl.BlockSpec(memory_space=pltpu.SMEM),
                  pl.BlockSpec((2, t, t), lambda m: (0, 0, 0))],
        out_specs=pl.BlockSpec((1, 2, t, t), lambda m: (m, 0, 0, 0)),
        out_shape=jax.ShapeDtypeStruct((n_maps, 2, t, t), F32),
        name="rel_bias_tiles",
    )(rel_bias.astype(F32), idx)


def _ffn_kernel(*refs, n_chunks, has_proj, has_final):
    refs = list(refs)
    x_ref = refs.pop(0)
    if has_proj:
        o_ref, wo_ref = refs.pop(0), refs.pop(0)
    g_ref, win_ref, wout_ref = refs.pop(0), refs.pop(0), refs.pop(0)
    if has_final:
        gf_ref = refs.pop(0)
    (out_ref,) = refs

    x = x_ref[...]
    if has_proj:
        x = x + jnp.dot(o_ref[...], wo_ref[...], preferred_element_type=F32)
    hn = _rms(x, g_ref[...], RMS_EPS).astype(BF16)
    acc = None
    for c in range(n_chunks):
        gu = jnp.dot(hn, win_ref[c], preferred_element_type=F32)
        gate, up = gu[:, :FFN_CHUNK], gu[:, FFN_CHUNK:]
        a = (gate * (1.0 / (1.0 + jnp.exp(-gate))) * up).astype(BF16)
        part = jnp.dot(a, wout_ref[c], preferred_element_type=F32)
        acc = part if acc is None else acc + part
    y = x + FFN_RESIDUAL * acc
    if has_final:
        y = _rms(y, gf_ref[...], RMS_EPS)
    out_ref[...] = y


def _ffn(x2d, g, w_in, w_out, proj=None, final_g=None):
    T, D = x2d.shape
    F = w_out.shape[0]
    n_chunks = F // FFN_CHUNK
    assert n_chunks * FFN_CHUNK == F and T % TOKEN_TILE == 0
    tm = TOKEN_TILE
    w_in_c = jnp.concatenate([w_in[:, :F].reshape(D, n_chunks, FFN_CHUNK),
                              w_in[:, F:].reshape(D, n_chunks, FFN_CHUNK)], axis=-1)
    w_in_c = w_in_c.transpose(1, 0, 2).astype(BF16)
    w_out_c = w_out.reshape(n_chunks, FFN_CHUNK, D).astype(BF16)

    row = lambda i: (i, 0)
    args, specs = [x2d], [pl.BlockSpec((tm, D), row)]
    if proj is not None:
        o2d, w_o = proj
        args += [o2d, w_o.astype(BF16)]
        specs += [pl.BlockSpec((tm, o2d.shape[1]), row), _const_spec(w_o.shape)]
    args += [g.reshape(1, D).astype(F32), w_in_c, w_out_c]
    specs += [_const_spec((1, D)), _const_spec(w_in_c.shape), _const_spec(w_out_c.shape)]
    if final_g is not None:
        args.append(final_g.reshape(1, D).astype(F32))
        specs.append(_const_spec((1, D)))

    return pl.pallas_call(
        functools.partial(_ffn_kernel, n_chunks=n_chunks, has_proj=proj is not None,
                          has_final=final_g is not None),
        grid=(T // tm,),
        in_specs=specs,
        out_specs=pl.BlockSpec((tm, D), row),
        out_shape=jax.ShapeDtypeStruct((T, D), F32),
        compiler_params=pltpu.CompilerParams(dimension_semantics=("parallel",),
                                             vmem_limit_bytes=VMEM_LIMIT_BYTES),
        name="ffn",
    )(*args)


def _proj_kernel(x_ref, g_ref, wqT_ref, wk_ref, wvT_ref, qT_ref, k_ref, vT_ref, *, n_pairs, n_sub):
    hn = _rms(x_ref[0], g_ref[...], RMS_EPS).astype(BF16)
    k_ref[0] = jnp.dot(hn, wk_ref[...], preferred_element_type=F32).astype(BF16)
    qT_ref[0] = lax.dot_general(wqT_ref[...], hn, _NT, preferred_element_type=F32).astype(BF16)
    vT = lax.dot_general(wvT_ref[...], hn, _NT, preferred_element_type=F32).astype(BF16)
    for p in range(n_pairs):
        for c in range(n_sub):
            vT_ref[0, p, c] = vT[p * PAIR:(p + 1) * PAIR, c * ATT_TILE:(c + 1) * ATT_TILE]


def _qkv_proj(h, g, w_qkv):
    B, S, D = h.shape
    tm = TOKEN_TILE
    n_pairs, n_sub, nk = D // PAIR, tm // ATT_TILE, S // ATT_TILE
    wq, wk, wv = w_qkv[:, :D], w_qkv[:, D:2 * D], w_qkv[:, 2 * D:]
    wqT = (wq * (HEAD_DIM ** -0.5 * LOG2E)).T.astype(BF16)
    wvT = wv.T.astype(BF16)
    return pl.pallas_call(
        functools.partial(_proj_kernel, n_pairs=n_pairs, n_sub=n_sub),
        grid=(B, S // tm),
        in_specs=[pl.BlockSpec((1, tm, D), lambda b, s: (b, s, 0)),
                  _const_spec((1, D)), _const_spec((D, D)), _const_spec((D, D)), _const_spec((D, D))],
        out_specs=[pl.BlockSpec((1, D, tm), lambda b, s: (b, 0, s)),
                   pl.BlockSpec((1, tm, D), lambda b, s: (b, s, 0)),
                   pl.BlockSpec((1, n_pairs, n_sub, PAIR, ATT_TILE), lambda b, s: (b, 0, s, 0, 0))],
        out_shape=[jax.ShapeDtypeStruct((B, D, S), BF16),
                   jax.ShapeDtypeStruct((B, S, D), BF16),
                   jax.ShapeDtypeStruct((B, n_pairs, nk, PAIR, ATT_TILE), BF16)],
        compiler_params=pltpu.CompilerParams(dimension_semantics=("parallel", "parallel"),
                                             vmem_limit_bytes=VMEM_LIMIT_BYTES),
        name="qkv_proj",
    )(h, g.reshape(1, D).astype(F32), wqT, wk.astype(BF16), wvT)


def _split_pairs(qT_ref, qz_ref):
    zeros = jnp.zeros((HEAD_DIM, qT_ref.shape[2]), qT_ref.dtype)
    for g in range(PAIRS_PER_STEP):
        q = qT_ref[0, g * PAIR:(g + 1) * PAIR, :]
        qz_ref[2 * g, :HEAD_DIM, :] = q[:HEAD_DIM]
        qz_ref[2 * g, HEAD_DIM:, :] = zeros
        qz_ref[2 * g + 1, :HEAD_DIM, :] = zeros
        qz_ref[2 * g + 1, HEAD_DIM:, :] = q[HEAD_DIM:]


def _tile_update(m, s, vT_t, bias, m_ref, l_ref, acc_ref, first):
    if bias is not None:
        s = bias + s
    m_cur = jnp.max(s, axis=0, keepdims=True)
    if first:
        m_new = m_cur
    else:
        m_old = m_ref[m]
        m_new = jnp.maximum(m_old, m_cur)
    p = jnp.exp2(s - m_new)
    l_cur = jnp.sum(p, axis=0, keepdims=True)
    pv = jnp.dot(vT_t, p.astype(BF16), preferred_element_type=F32)
    if first:
        l_ref[m] = l_cur
        acc_ref[m] = pv
    else:
        alpha = jnp.exp2(m_old - m_new)
        l_ref[m] = alpha * l_ref[m] + l_cur
        acc_ref[m] = alpha * acc_ref[m] + pv
    m_ref[m] = m_new


def _k_tile(k_ref, j, g):
    rows = pl.ds(pl.multiple_of(j * ATT_TILE, ATT_TILE), ATT_TILE)
    return k_ref[0, rows, g * PAIR:(g + 1) * PAIR]


def _sweep(qi, k_ref, vT_ref, v_rows, near_bias, far_bias, state):
    qz_ref, m_ref, l_ref, acc_ref = state
    n_maps = 2 * PAIRS_PER_STEP

    def visit(j, bias_of, first):
        scores = {}

        def qk(m):
            scores[m] = jnp.dot(_k_tile(k_ref, j, m // 2), qz_ref[m], preferred_element_type=F32)

        for m in range(min(QK_AHEAD, n_maps)):
            qk(m)
        for m in range(n_maps):
            if m + QK_AHEAD < n_maps:
                qk(m + QK_AHEAD)
            _tile_update(m, scores.pop(m), v_rows(vT_ref[0, m // 2, j], m), bias_of(m, j),
                         m_ref, l_ref, acc_ref, first=first)

    visit(qi, lambda m, j: near_bias(m, 0, j), True)

    @pl.when(qi >= 1)
    def _():
        visit(qi - 1, lambda m, j: near_bias(m, 1, j), False)

    def far(j, carry):
        visit(j, far_bias, False)
        return carry

    lax.fori_loop(0, jnp.maximum(qi - 1, 0), far, 0)


def _diff_kernel(qT_ref, k_ref, vT_ref, bias_ref, lam_ref, g_ref, o_ref,
                 qz_ref, m_ref, l_ref, acc_ref, *, lam_init):
    qi = pl.program_id(2)
    _split_pairs(qT_ref, qz_ref)
    _sweep(qi, k_ref, vT_ref, lambda v, m: v,
           lambda m, t, j: bias_ref[m, t], lambda m, j: None,
           (qz_ref, m_ref, l_ref, acc_ref))

    lp = lam_ref[...]
    lam = (jnp.exp(jnp.sum(lp[0:1] * lp[1:2], axis=-1, keepdims=True))
           - jnp.exp(jnp.sum(lp[2:3] * lp[3:4], axis=-1, keepdims=True)) + lam_init)
    for g in range(PAIRS_PER_STEP):
        o = (acc_ref[2 * g] * (1.0 / l_ref[2 * g])
             - lam * (acc_ref[2 * g + 1] * (1.0 / l_ref[2 * g + 1])))
        o = o * lax.rsqrt(jnp.mean(o * o, axis=0, keepdims=True) + SUBLN_EPS) * g_ref[...] * (1.0 - lam_init)
        o_ref[0, :, g * PAIR:(g + 1) * PAIR] = o.T.astype(BF16)


def _moba_kernel(qT_ref, k_ref, vT_ref, bias_ref, o_ref,
                 qz_ref, m_ref, l_ref, acc_ref, kmean_ref, selb_ref, *, nk):
    qi = pl.program_id(2)
    tq = qT_ref.shape[2]

    @pl.when(qi == 0)
    def _():
        for j in range(nk):
            kb = k_ref[0, j * MOBA_BLOCK:(j + 1) * MOBA_BLOCK, :].astype(F32)
            kmean_ref[j:j + 1, :] = jnp.mean(kb, axis=0, keepdims=True)

    _split_pairs(qT_ref, qz_ref)
    blk = lax.broadcasted_iota(jnp.int32, (nk, tq), 0)
    eligible = blk < qi
    for m in range(2 * PAIRS_PER_STEP):
        g = m // 2
        gate = jnp.dot(kmean_ref[:, g * PAIR:(g + 1) * PAIR], qz_ref[m].astype(F32),
                       preferred_element_type=F32, precision=lax.Precision.HIGHEST)
        gate_m = jnp.where(eligible, gate, -jnp.inf)
        n_better = jnp.zeros((nk, tq), F32)
        for c in range(nk):
            gc = gate_m[c:c + 1, :]
            better = jnp.where(gc > gate, 1.0, jnp.where((gc == gate) & (c < blk), 1.0, 0.0))
            n_better = n_better + better
        selected = eligible & (n_better < MOBA_TOPK)
        selb_ref[m] = jnp.where(selected, 0.0, MASK_VALUE)

    _sweep(qi, k_ref, vT_ref, lambda v, m: v[(m % 2) * HEAD_DIM:(m % 2 + 1) * HEAD_DIM],
           lambda m, t, j: bias_ref[m, t] if t == 0 else bias_ref[m, t] + selb_ref[m, pl.ds(j, 1), :],
           lambda m, j: selb_ref[m, pl.ds(j, 1), :],
           (qz_ref, m_ref, l_ref, acc_ref))

    for g in range(PAIRS_PER_STEP):
        o = jnp.concatenate([acc_ref[2 * g + i] * (1.0 / l_ref[2 * g + i]) for i in range(2)], axis=0)
        o_ref[0, :, g * PAIR:(g + 1) * PAIR] = o.T.astype(BF16)


def _attention(kind, qT, k, vT, bias, extra, *, lam_init=None):
    B, D, S = qT.shape
    G, nk, t = PAIRS_PER_STEP, S // ATT_TILE, ATT_TILE
    n_maps = 2 * G
    assert D % (G * PAIR) == 0
    in_specs = [pl.BlockSpec((1, G * PAIR, t), lambda b, p, q: (b, p, q)),
                pl.BlockSpec((1, S, G * PAIR), lambda b, p, q: (b, 0, p)),
                pl.BlockSpec((1, G, nk, PAIR, t), lambda b, p, q: (b, p, 0, 0, 0)),
                pl.BlockSpec((n_maps, 2, t, t), lambda b, p, q: (p, 0, 0, 0), pipeline_mode=pl.Buffered(1))]
    scratch = [pltpu.VMEM((n_maps, PAIR, t), BF16),
               pltpu.VMEM((n_maps, 1, t), F32),
               pltpu.VMEM((n_maps, 1, t), F32)]
    if kind == "diff":
        body = functools.partial(_diff_kernel, lam_init=lam_init)
        in_specs += [pl.BlockSpec(e.shape, lambda b, p, q: (0, 0)) for e in extra]
        scratch += [pltpu.VMEM((n_maps, PAIR, t), F32)]
    else:
        body = functools.partial(_moba_kernel, nk=nk)
        scratch += [pltpu.VMEM((n_maps, HEAD_DIM, t), F32),
                    pltpu.VMEM((nk, G * PAIR), F32),
                    pltpu.VMEM((n_maps, nk, t), F32)]
    return pl.pallas_call(
        body,
        grid=(B, D // (G * PAIR), nk),
        in_specs=in_specs,
        out_specs=pl.BlockSpec((1, t, G * PAIR), lambda b, p, q: (b, q, p)),
        out_shape=jax.ShapeDtypeStruct((B, S, D), BF16),
        scratch_shapes=scratch,
        compiler_params=pltpu.CompilerParams(dimension_semantics=("parallel", "parallel", "arbitrary"),
                                             vmem_limit_bytes=VMEM_LIMIT_BYTES),
        name=kind + "_attention",
    )(qT, k, vT, bias, *extra)


def kernel(x, rel_bias, norm_g, final_norm_g, ffn_w_in, ffn_w_out, diff_w_qkv, diff_lambda,
           diff_subln_g, diff_w_o, moba_w_qkv, moba_w_o):
    B, S, D = x.shape
    depth = norm_g.shape[0]
    assert S % TOKEN_TILE == 0 and D % PAIR == 0 and MOBA_BLOCK == ATT_TILE
    assert rel_bias.shape == (REL_BUCKETS, D // HEAD_DIM)
    bias = _bias_tiles(rel_bias, ATT_TILE)

    h = x.reshape(B * S, D)
    for i in range(depth):
        g = norm_g[i]
        h = _ffn(h, g[0], ffn_w_in[i, 0], ffn_w_out[i, 0])
        j = i // 2
        if i % 2 == 0:
            qT, k, vT = _qkv_proj(h.reshape(B, S, D), g[1], diff_w_qkv[j])
            lam_init = 0.8 - 0.6 * math.exp(-0.3 * i)
            g_sub = jnp.broadcast_to(diff_subln_g[j].astype(F32)[:, None], (PAIR, ATT_TILE))
            o = _attention("diff", qT, k, vT, bias, (diff_lambda[j].astype(F32), g_sub), lam_init=lam_init)
            w_o = diff_w_o[j]
        else:
            qT, k, vT = _qkv_proj(h.reshape(B, S, D), g[1], moba_w_qkv[j])
            o = _attention("moba", qT, k, vT, bias, ())
            w_o = moba_w_o[j]
        last = i == depth - 1
        h = _ffn(h, g[2], ffn_w_in[i, 1], ffn_w_out[i, 1], proj=(o.reshape(B * S, D), w_o),
                 final_g=final_norm_g if last else None)
    return h.reshape(B, S, D)
```

```python
import functools
import math

import numpy as np
import jax
import jax.numpy as jnp
from jax import lax
from jax.experimental import pallas as pl
from jax.experimental.pallas import tpu as pltpu

HEAD_DIM = 64
PAIR = 2 * HEAD_DIM
MOBA_BLOCK = 256
MOBA_TOPK = 3
REL_BUCKETS = 32
REL_MAX_DIST = 128
FFN_RESIDUAL = 0.5
RMS_EPS = 1e-6
SUBLN_EPS = 1e-5

ATT_TILE = 256
PAIRS_PER_STEP = 8
QK_AHEAD = 4
QK_AHEAD_PAIR = 2
FFN_CHUNK = 256
TOKEN_TILE = 512
MASK_VALUE = -1e30
LOG2E = math.log2(math.e)
VMEM_LIMIT_BYTES = 56 * 1024 * 1024

F32 = jnp.float32
BF16 = jnp.bfloat16
_NT = (((1,), (1,)), ((), ()))


def _rms(x, g, eps):
    return x * lax.rsqrt(jnp.mean(x * x, axis=-1, keepdims=True) + eps) * g


def _const_spec(shape):
    return pl.BlockSpec(shape, lambda *_: (0,) * len(shape), pipeline_mode=pl.Buffered(1))


def _rel_bucket_np(dist):
    n = np.maximum(dist, 0)
    max_exact = REL_BUCKETS // 2
    nf = np.maximum(n, 1).astype(np.float32)
    large = max_exact + (np.log(nf / np.float32(max_exact)) / np.float32(math.log(REL_MAX_DIST / max_exact))
                         * np.float32(REL_BUCKETS - max_exact)).astype(np.int32)
    large = np.minimum(large, REL_BUCKETS - 1)
    return np.where(n < max_exact, n, large).astype(np.int32)


def _bucket_tiles(t):
    j = np.arange(t)[:, None]
    i = np.arange(t)[None, :]
    diag = np.where(i - j >= 0, _rel_bucket_np(i - j), -1)
    prev = _rel_bucket_np(i - j + t)
    assert _rel_bucket_np(np.arange(t + 1, 8 * t)).min() == REL_BUCKETS - 1
    return np.stack([diag, prev]).astype(np.int32)


def _bias_kernel(rb_ref, idx_ref, out_ref):
    m = pl.program_id(0)
    far = rb_ref[REL_BUCKETS - 1, m]
    for t in range(2):
        idx = idx_ref[t]
        acc = jnp.zeros(idx.shape, F32)
        for b in range(REL_BUCKETS - 1):
            acc = jnp.where(idx == b, (rb_ref[b, m] - far) * LOG2E, acc)
        out_ref[0, t] = jnp.where(idx < 0, MASK_VALUE, acc)


def _bias_tiles(rel_bias, t):
    n_maps = rel_bias.shape[1]
    idx = jnp.asarray(_bucket_tiles(t))
    return pl.pallas_call(
        _bias_kernel,
        grid=(n_maps,),
        in_specs=[pl.BlockSpec(memory_space=pltpu.SMEM),
                  pl.BlockSpec((2, t, t), lambda m: (0, 0, 0))],
        out_specs=pl.BlockSpec((1, 2, t, t), lambda m: (m, 0, 0, 0)),
        out_shape=jax.ShapeDtypeStruct((n_maps, 2, t, t), F32),
        name="rel_bias_tiles",
    )(rel_bias.astype(F32), idx)


def _ffn_kernel(*refs, n_chunks, has_proj, has_final):
    refs = list(refs)
    x_ref = refs.pop(0)
    if has_proj:
        o_ref, wo_ref = refs.pop(0), refs.pop(0)
    g_ref, win_ref, wout_ref = refs.pop(0), refs.pop(0), refs.pop(0)
    if has_final:
        gf_ref = refs.pop(0)
    (out_ref,) = refs

    x = x_ref[...]
    if has_proj:
        x = x + jnp.dot(o_ref[...], wo_ref[...], preferred_element_type=F32)
    hn = _rms(x, g_ref[...], RMS_EPS).astype(BF16)
    acc = None
    for c in range(n_chunks):
        gu = jnp.dot(hn, win_ref[c], preferred_element_type=F32)
        gate, up = gu[:, :FFN_CHUNK], gu[:, FFN_CHUNK:]
        a = (gate * (1.0 / (1.0 + jnp.exp(-gate))) * up).astype(BF16)
        part = jnp.dot(a, wout_ref[c], preferred_element_type=F32)
        acc = part if acc is None else acc + part
    y = x + FFN_RESIDUAL * acc
    if has_final:
        y = _rms(y, gf_ref[...], RMS_EPS)
    out_ref[...] = y


def _ffn(x2d, g, w_in, w_out, proj=None, final_g=None):
    T, D = x2d.shape
    F = w_out.shape[0]
    n_chunks = F // FFN_CHUNK
    assert n_chunks * FFN_CHUNK == F and T % TOKEN_TILE == 0
    tm = TOKEN_TILE
    w_in_c = jnp.concatenate([w_in[:, :F].reshape(D, n_chunks, FFN_CHUNK),
                              w_in[:, F:].reshape(D, n_chunks, FFN_CHUNK)], axis=-1)
    w_in_c = w_in_c.transpose(1, 0, 2).astype(BF16)
    w_out_c = w_out.reshape(n_chunks, FFN_CHUNK, D).astype(BF16)

    row = lambda i: (i, 0)
    args, specs = [x2d], [pl.BlockSpec((tm, D), row)]
    if proj is not None:
        o2d, w_o = proj
        args += [o2d, w_o.astype(BF16)]
        specs += [pl.BlockSpec((tm, o2d.shape[1]), row), _const_spec(w_o.shape)]
    args += [g.reshape(1, D).astype(F32), w_in_c, w_out_c]
    specs += [_const_spec((1, D)), _const_spec(w_in_c.shape), _const_spec(w_out_c.shape)]
    if final_g is not None:
        args.append(final_g.reshape(1, D).astype(F32))
        specs.append(_const_spec((1, D)))

    return pl.pallas_call(
        functools.partial(_ffn_kernel, n_chunks=n_chunks, has_proj=proj is not None,
                          has_final=final_g is not None),
        grid=(T // tm,),
        in_specs=specs,
        out_specs=pl.BlockSpec((tm, D), row),
        out_shape=jax.ShapeDtypeStruct((T, D), F32),
        compiler_params=pltpu.CompilerParams(dimension_semantics=("parallel",),
                                             vmem_limit_bytes=VMEM_LIMIT_BYTES),
        name="ffn",
    )(*args)


def _proj_kernel(x_ref, g_ref, wqT_ref, wk_ref, wvT_ref, qT_ref, k_ref, vT_ref, *, n_pairs, n_sub):
    hn = _rms(x_ref[0], g_ref[...], RMS_EPS).astype(BF16)
    k_ref[0] = jnp.dot(hn, wk_ref[...], preferred_element_type=F32).astype(BF16)
    qT_ref[0] = lax.dot_general(wqT_ref[...], hn, _NT, preferred_element_type=F32).astype(BF16)
    vT = lax.dot_general(wvT_ref[...], hn, _NT, preferred_element_type=F32).astype(BF16)
    for p in range(n_pairs):
        for c in range(n_sub):
            vT_ref[0, p, c] = vT[p * PAIR:(p + 1) * PAIR, c * ATT_TILE:(c + 1) * ATT_TILE]


def _qkv_proj(h, g, w_qkv):
    B, S, D = h.shape
    tm = TOKEN_TILE
    n_pairs, n_sub, nk = D // PAIR, tm // ATT_TILE, S // ATT_TILE
    wq, wk, wv = w_qkv[:, :D], w_qkv[:, D:2 * D], w_qkv[:, 2 * D:]
    wqT = (wq * (HEAD_DIM ** -0.5 * LOG2E)).T.astype(BF16)
    wvT = wv.T.astype(BF16)
    return pl.pallas_call(
        functools.partial(_proj_kernel, n_pairs=n_pairs, n_sub=n_sub),
        grid=(B, S // tm),
        in_specs=[pl.BlockSpec((1, tm, D), lambda b, s: (b, s, 0)),
                  _const_spec((1, D)), _const_spec((D, D)), _const_spec((D, D)), _const_spec((D, D))],
        out_specs=[pl.BlockSpec((1, D, tm), lambda b, s: (b, 0, s)),
                   pl.BlockSpec((1, tm, D), lambda b, s: (b, s, 0)),
                   pl.BlockSpec((1, n_pairs, n_sub, PAIR, ATT_TILE), lambda b, s: (b, 0, s, 0, 0))],
        out_shape=[jax.ShapeDtypeStruct((B, D, S), BF16),
                   jax.ShapeDtypeStruct((B, S, D), BF16),
                   jax.ShapeDtypeStruct((B, n_pairs, nk, PAIR, ATT_TILE), BF16)],
        compiler_params=pltpu.CompilerParams(dimension_semantics=("parallel", "parallel"),
                                             vmem_limit_bytes=VMEM_LIMIT_BYTES),
        name="qkv_proj",
    )(h, g.reshape(1, D).astype(F32), wqT, wk.astype(BF16), wvT)


def _split_pairs(qT_ref, qz_ref):
    zeros = jnp.zeros((HEAD_DIM, qT_ref.shape[2]), qT_ref.dtype)
    for g in range(PAIRS_PER_STEP):
        q = qT_ref[0, g * PAIR:(g + 1) * PAIR, :]
        qz_ref[2 * g, :HEAD_DIM, :] = q[:HEAD_DIM]
        qz_ref[2 * g, HEAD_DIM:, :] = zeros
        qz_ref[2 * g + 1, :HEAD_DIM, :] = zeros
        qz_ref[2 * g + 1, HEAD_DIM:, :] = q[HEAD_DIM:]


def _update(m, scores, vT_tiles, biases, m_ref, l_ref, acc_ref, first):
    scores = [s if b is None else b + s for s, b in zip(scores, biases)]
    m_cur = jnp.max(functools.reduce(jnp.maximum, scores), axis=0, keepdims=True)
    if first:
        m_new = m_cur
    else:
        m_old = m_ref[m]
        m_new = jnp.maximum(m_old, m_cur)
    probs = [jnp.exp2(s - m_new) for s in scores]
    l_cur = jnp.sum(functools.reduce(jnp.add, probs), axis=0, keepdims=True)
    pv = functools.reduce(jnp.add, [jnp.dot(v, p.astype(BF16), preferred_element_type=F32)
                                    for v, p in zip(vT_tiles, probs)])
    if first:
        l_ref[m] = l_cur
        acc_ref[m] = pv
    else:
        alpha = jnp.exp2(m_old - m_new)
        l_ref[m] = alpha * l_ref[m] + l_cur
        acc_ref[m] = alpha * acc_ref[m] + pv
    m_ref[m] = m_new


def _sweep(qi, k_ref, vT_ref, v_rows, near_bias, far_bias, state, pre_ref):
    qz_ref, m_ref, l_ref, acc_ref = state
    n_maps = 2 * PAIRS_PER_STEP

    def raw_scores(j, m):
        rows = pl.ds(pl.multiple_of(j * ATT_TILE, ATT_TILE), ATT_TILE)
        k_t = k_ref[0, rows, (m // 2) * PAIR:(m // 2 + 1) * PAIR]
        return jnp.dot(k_t, qz_ref[m], preferred_element_type=F32)

    def block(tiles, biases, first, ahead, preloaded, next_tiles, next_ahead):
        scores = {}
        for m in range(ahead):
            scores[m] = ([pre_ref[m, t] for t in range(len(tiles))] if preloaded
                         else [raw_scores(j, m) for j in tiles])
        for m in range(n_maps):
            nxt = m + ahead
            if nxt < n_maps:
                scores[nxt] = [raw_scores(j, nxt) for j in tiles]
            elif nxt - n_maps < next_ahead:
                for t, j in enumerate(next_tiles):
                    pre_ref[nxt - n_maps, t] = raw_scores(j, nxt - n_maps)
            _update(m, scores.pop(m), [v_rows(vT_ref[0, m // 2, j], m) for j in tiles], biases(m),
                    m_ref, l_ref, acc_ref, first)

    n_far = jnp.maximum(qi - 1, 0)
    n_pairs = n_far // 2

    block([qi], lambda m: [near_bias(m, 0, qi)], True, QK_AHEAD, False, [n_far], QK_AHEAD)

    @pl.when(qi >= 1)
    def _():
        block([qi - 1], lambda m: [near_bias(m, 1, qi - 1)], False, QK_AHEAD, True, [0, 1], QK_AHEAD_PAIR)

    @pl.when(n_far % 2 == 1)
    def _():
        block([n_far - 1], lambda m: [far_bias(m, n_far - 1)], False, QK_AHEAD, False, [], 0)

    def far_pair(p, carry):
        j, jn = 2 * p, 2 * jnp.minimum(p + 1, n_pairs - 1)
        block([j, j + 1], lambda m: [far_bias(m, j), far_bias(m, j + 1)], False, QK_AHEAD_PAIR, True,
              [jn, jn + 1], QK_AHEAD_PAIR)
        return carry

    lax.fori_loop(0, n_pairs, far_pair, 0)


def _diff_kernel(qT_ref, k_ref, vT_ref, bias_ref, lam_ref, g_ref, o_ref,
                 qz_ref, m_ref, l_ref, pre_ref, acc_ref, *, lam_init):
    qi = pl.program_id(2)
    _split_pairs(qT_ref, qz_ref)
    _sweep(qi, k_ref, vT_ref, lambda v, m: v,
           lambda m, t, j: bias_ref[m, t], lambda m, j: None,
           (qz_ref, m_ref, l_ref, acc_ref), pre_ref)

    lp = lam_ref[...]
    lam = (jnp.exp(jnp.sum(lp[0:1] * lp[1:2], axis=-1, keepdims=True))
           - jnp.exp(jnp.sum(lp[2:3] * lp[3:4], axis=-1, keepdims=True)) + lam_init)
    for g in range(PAIRS_PER_STEP):
        o = (acc_ref[2 * g] * (1.0 / l_ref[2 * g])
             - lam * (acc_ref[2 * g + 1] * (1.0 / l_ref[2 * g + 1])))
        o = o * lax.rsqrt(jnp.mean(o * o, axis=0, keepdims=True) + SUBLN_EPS) * g_ref[...] * (1.0 - lam_init)
        o_ref[0, :, g * PAIR:(g + 1) * PAIR] = o.T.astype(BF16)


def _moba_kernel(qT_ref, k_ref, vT_ref, bias_ref, o_ref,
                 qz_ref, m_ref, l_ref, pre_ref, acc_ref, kmean_ref, selb_ref, *, nk):
    qi = pl.program_id(2)
    tq = qT_ref.shape[2]

    @pl.when(qi == 0)
    def _():
        for j in range(nk):
            kb = k_ref[0, j * MOBA_BLOCK:(j + 1) * MOBA_BLOCK, :].astype(F32)
            kmean_ref[j:j + 1, :] = jnp.mean(kb, axis=0, keepdims=True)

    _split_pairs(qT_ref, qz_ref)
    blk = lax.broadcasted_iota(jnp.int32, (nk, tq), 0)
    eligible = blk < qi
    for m in range(2 * PAIRS_PER_STEP):
        g = m // 2
        gate = jnp.dot(kmean_ref[:, g * PAIR:(g + 1) * PAIR], qz_ref[m].astype(F32),
                       preferred_element_type=F32, precision=lax.Precision.HIGHEST)
        gate_m = jnp.where(eligible, gate, -jnp.inf)
        n_better = jnp.zeros((nk, tq), F32)
        for c in range(nk):
            gc = gate_m[c:c + 1, :]
            better = jnp.where(gc > gate, 1.0, jnp.where((gc == gate) & (c < blk), 1.0, 0.0))
            n_better = n_better + better
        selected = eligible & (n_better < MOBA_TOPK)
        selb_ref[m] = jnp.where(selected, 0.0, MASK_VALUE)

    _sweep(qi, k_ref, vT_ref, lambda v, m: v[(m % 2) * HEAD_DIM:(m % 2 + 1) * HEAD_DIM],
           lambda m, t, j: bias_ref[m, t] if t == 0 else bias_ref[m, t] + selb_ref[m, pl.ds(j, 1), :],
           lambda m, j: selb_ref[m, pl.ds(j, 1), :],
           (qz_ref, m_ref, l_ref, acc_ref), pre_ref)

    for g in range(PAIRS_PER_STEP):
        o = jnp.concatenate([acc_ref[2 * g + i] * (1.0 / l_ref[2 * g + i]) for i in range(2)], axis=0)
        o_ref[0, :, g * PAIR:(g + 1) * PAIR] = o.T.astype(BF16)


def _attention(kind, qT, k, vT, bias, extra, *, lam_init=None):
    B, D, S = qT.shape
    G, nk, t = PAIRS_PER_STEP, S // ATT_TILE, ATT_TILE
    n_maps = 2 * G
    assert D % (G * PAIR) == 0 and nk >= 2 and QK_AHEAD_PAIR <= QK_AHEAD <= n_maps
    in_specs = [pl.BlockSpec((1, G * PAIR, t), lambda b, p, q: (b, p, q)),
                pl.BlockSpec((1, S, G * PAIR), lambda b, p, q: (b, 0, p)),
                pl.BlockSpec((1, G, nk, PAIR, t), lambda b, p, q: (b, p, 0, 0, 0)),
                pl.BlockSpec((n_maps, 2, t, t), lambda b, p, q: (p, 0, 0, 0), pipeline_mode=pl.Buffered(1))]
    scratch = [pltpu.VMEM((n_maps, PAIR, t), BF16),
               pltpu.VMEM((n_maps, 1, t), F32),
               pltpu.VMEM((n_maps, 1, t), F32),
               pltpu.VMEM((max(QK_AHEAD, QK_AHEAD_PAIR), 2, t, t), F32)]
    if kind == "diff":
        body = functools.partial(_diff_kernel, lam_init=lam_init)
        in_specs += [pl.BlockSpec(e.shape, lambda b, p, q: (0, 0)) for e in extra]
        scratch += [pltpu.VMEM((n_maps, PAIR, t), F32)]
    else:
        body = functools.partial(_moba_kernel, nk=nk)
        scratch += [pltpu.VMEM((n_maps, HEAD_DIM, t), F32),
                    pltpu.VMEM((nk, G * PAIR), F32),
                    pltpu.VMEM((n_maps, nk, t), F32)]
    return pl.pallas_call(
        body,
        grid=(B, D // (G * PAIR), nk),
        in_specs=in_specs,
        out_specs=pl.BlockSpec((1, t, G * PAIR), lambda b, p, q: (b, q, p)),
        out_shape=jax.ShapeDtypeStruct((B, S, D), BF16),
        scratch_shapes=scratch,
        compiler_params=pltpu.CompilerParams(dimension_semantics=("parallel", "parallel", "arbitrary"),
                                             vmem_limit_bytes=VMEM_LIMIT_BYTES),
        name=kind + "_attention",
    )(qT, k, vT, bias, *extra)


def kernel(x, rel_bias, norm_g, final_norm_g, ffn_w_in, ffn_w_out, diff_w_qkv, diff_lambda,
           diff_subln_g, diff_w_o, moba_w_qkv, moba_w_o):
    B, S, D = x.shape
    depth = norm_g.shape[0]
    assert S % TOKEN_TILE == 0 and D % PAIR == 0 and MOBA_BLOCK == ATT_TILE
    assert rel_bias.shape == (REL_BUCKETS, D // HEAD_DIM)
    bias = _bias_tiles(rel_bias, ATT_TILE)

    h = x.reshape(B * S, D)
    for i in range(depth):
        g = norm_g[i]
        h = _ffn(h, g[0], ffn_w_in[i, 0], ffn_w_out[i, 0])
        j = i // 2
        if i % 2 == 0:
            qT, k, vT = _qkv_proj(h.reshape(B, S, D), g[1], diff_w_qkv[j])
            lam_init = 0.8 - 0.6 * math.exp(-0.3 * i)
            g_sub = jnp.broadcast_to(diff_subln_g[j].astype(F32)[:, None], (PAIR, ATT_TILE))
            o = _attention("diff", qT, k, vT, bias, (diff_lambda[j].astype(F32), g_sub), lam_init=lam_init)
            w_o = diff_w_o[j]
        else:
            qT, k, vT = _qkv_proj(h.reshape(B, S, D), g[1], moba_w_qkv[j])
            o = _attention("moba", qT, k, vT, bias, ())
            w_o = moba_w_o[j]
        last = i == depth - 1
        h = _ffn(h, g[2], ffn_w_in[i, 1], ffn_w_out[i, 1], proj=(o.reshape(B * S, D), w_o),
                 final_g=final_norm_g if last else None)
    return h.reshape(B, S, D)
```

```python
import functools
import math

import numpy as np
import jax
import jax.numpy as jnp
from jax import lax
from jax.experimental import pallas as pl
from jax.experimental.pallas import tpu as pltpu

HEAD_DIM = 64
PAIR = 2 * HEAD_DIM
MOBA_BLOCK = 256
MOBA_TOPK = 3
KMEAN_TERMS = 3
REL_BUCKETS = 32
REL_MAX_DIST = 128
FFN_RESIDUAL = 0.5
RMS_EPS = 1e-6
SUBLN_EPS = 1e-5

ATT_TILE = 256
PAIRS_PER_STEP = 8
QK_AHEAD = 4
QK_AHEAD_PAIR = 2
FFN_CHUNK = 256
TOKEN_TILE = 512
MASK_VALUE = -1e30
LOG2E = math.log2(math.e)
VMEM_LIMIT_BYTES = 56 * 1024 * 1024

F32 = jnp.float32
BF16 = jnp.bfloat16
_NT = (((1,), (1,)), ((), ()))


def _rms(x, g, eps):
    return x * lax.rsqrt(jnp.mean(x * x, axis=-1, keepdims=True) + eps) * g


def _const_spec(shape):
    return pl.BlockSpec(shape, lambda *_: (0,) * len(shape), pipeline_mode=pl.Buffered(1))


def _rel_bucket_np(dist):
    n = np.maximum(dist, 0)
    max_exact = REL_BUCKETS // 2
    nf = np.maximum(n, 1).astype(np.float32)
    large = max_exact + (np.log(nf / np.float32(max_exact)) / np.float32(math.log(REL_MAX_DIST / max_exact))
                         * np.float32(REL_BUCKETS - max_exact)).astype(np.int32)
    large = np.minimum(large, REL_BUCKETS - 1)
    return np.where(n < max_exact, n, large).astype(np.int32)


def _bucket_tiles(t):
    j = np.arange(t)[:, None]
    i = np.arange(t)[None, :]
    diag = np.where(i - j >= 0, _rel_bucket_np(i - j), -1)
    prev = _rel_bucket_np(i - j + t)
    assert _rel_bucket_np(np.arange(t + 1, 8 * t)).min() == REL_BUCKETS - 1
    return np.stack([diag, prev]).astype(np.int32)


def _bias_kernel(rb_ref, idx_ref, out_ref):
    m = pl.program_id(0)
    far = rb_ref[REL_BUCKETS - 1, m]
    for t in range(2):
        idx = idx_ref[t]
        acc = jnp.zeros(idx.shape, F32)
        for b in range(REL_BUCKETS - 1):
            acc = jnp.where(idx == b, (rb_ref[b, m] - far) * LOG2E, acc)
        out_ref[0, t] = jnp.where(idx < 0, MASK_VALUE, acc)


def _bias_tiles(rel_bias, t):
    n_maps = rel_bias.shape[1]
    idx = jnp.asarray(_bucket_tiles(t))
    return pl.pallas_call(
        _bias_kernel,
        grid=(n_maps,),
        in_specs=[pl.BlockSpec(memory_space=pltpu.SMEM),
                  pl.BlockSpec((2, t, t), lambda m: (0, 0, 0))],
        out_specs=pl.BlockSpec((1, 2, t, t), lambda m: (m, 0, 0, 0)),
        out_shape=jax.ShapeDtypeStruct((n_maps, 2, t, t), F32),
        name="rel_bias_tiles",
    )(rel_bias.astype(F32), idx)


def _ffn_kernel(*refs, n_chunks, has_proj, has_final):
    refs = list(refs)
    x_ref = refs.pop(0)
    if has_proj:
        o_ref, wo_ref = refs.pop(0), refs.pop(0)
    g_ref, win_ref, wout_ref = refs.pop(0), refs.pop(0), refs.pop(0)
    if has_final:
        gf_ref = refs.pop(0)
    (out_ref,) = refs

    x = x_ref[...]
    if has_proj:
        x = x + jnp.dot(o_ref[...], wo_ref[...], preferred_element_type=F32)
    hn = _rms(x, g_ref[...], RMS_EPS).astype(BF16)
    acc = None
    d_ff = n_chunks * FFN_CHUNK
    for c in range(n_chunks):
        cols = slice(c * FFN_CHUNK, (c + 1) * FFN_CHUNK)
        gate = jnp.dot(hn, win_ref[:, cols], preferred_element_type=F32)
        up = jnp.dot(hn, win_ref[:, d_ff + c * FFN_CHUNK:d_ff + (c + 1) * FFN_CHUNK], preferred_element_type=F32)
        a = (gate * (1.0 / (1.0 + jnp.exp(-gate))) * up).astype(BF16)
        part = jnp.dot(a, wout_ref[cols, :], preferred_element_type=F32)
        acc = part if acc is None else acc + part
    y = x + FFN_RESIDUAL * acc
    if has_final:
        y = _rms(y, gf_ref[...], RMS_EPS)
    out_ref[...] = y


def _ffn(x2d, g, w_in, w_out, proj=None, final_g=None):
    T, D = x2d.shape
    F = w_out.shape[0]
    n_chunks = F // FFN_CHUNK
    assert n_chunks * FFN_CHUNK == F and T % TOKEN_TILE == 0
    tm = TOKEN_TILE
    w_in_c = w_in.astype(BF16)
    w_out_c = w_out.astype(BF16)

    row = lambda i: (i, 0)
    args, specs = [x2d], [pl.BlockSpec((tm, D), row)]
    if proj is not None:
        o2d, w_o = proj
        args += [o2d, w_o.astype(BF16)]
        specs += [pl.BlockSpec((tm, o2d.shape[1]), row), _const_spec(w_o.shape)]
    args += [g.reshape(1, D).astype(F32), w_in_c, w_out_c]
    specs += [_const_spec((1, D)), _const_spec(w_in_c.shape), _const_spec(w_out_c.shape)]
    if final_g is not None:
        args.append(final_g.reshape(1, D).astype(F32))
        specs.append(_const_spec((1, D)))

    return pl.pallas_call(
        functools.partial(_ffn_kernel, n_chunks=n_chunks, has_proj=proj is not None,
                          has_final=final_g is not None),
        grid=(T // tm,),
        in_specs=specs,
        out_specs=pl.BlockSpec((tm, D), row),
        out_shape=jax.ShapeDtypeStruct((T, D), F32),
        compiler_params=pltpu.CompilerParams(dimension_semantics=("parallel",),
                                             vmem_limit_bytes=VMEM_LIMIT_BYTES),
        name="ffn",
    )(*args)


def _proj_kernel(x_ref, g_ref, wqT_ref, wk_ref, wvT_ref, qT_ref, k_ref, vT_ref, *, n_pairs, n_sub):
    hn = _rms(x_ref[0], g_ref[...], RMS_EPS).astype(BF16)
    k_ref[0] = jnp.dot(hn, wk_ref[...], preferred_element_type=F32).astype(BF16)
    qT_ref[0] = lax.dot_general(wqT_ref[...], hn, _NT, preferred_element_type=F32).astype(BF16)
    vT = lax.dot_general(wvT_ref[...], hn, _NT, preferred_element_type=F32).astype(BF16)
    for p in range(n_pairs):
        for c in range(n_sub):
            vT_ref[0, p, c] = vT[p * PAIR:(p + 1) * PAIR, c * ATT_TILE:(c + 1) * ATT_TILE]


def _qkv_proj(h, g, w_qkv):
    B, S, D = h.shape
    tm = TOKEN_TILE
    n_pairs, n_sub, nk = D // PAIR, tm // ATT_TILE, S // ATT_TILE
    wq, wk, wv = w_qkv[:, :D], w_qkv[:, D:2 * D], w_qkv[:, 2 * D:]
    wqT = (wq * (HEAD_DIM ** -0.5 * LOG2E)).T.astype(BF16)
    wvT = wv.T.astype(BF16)
    return pl.pallas_call(
        functools.partial(_proj_kernel, n_pairs=n_pairs, n_sub=n_sub),
        grid=(B, S // tm),
        in_specs=[pl.BlockSpec((1, tm, D), lambda b, s: (b, s, 0)),
                  _const_spec((1, D)), _const_spec((D, D)), _const_spec((D, D)), _const_spec((D, D))],
        out_specs=[pl.BlockSpec((1, D, tm), lambda b, s: (b, 0, s)),
                   pl.BlockSpec((1, tm, D), lambda b, s: (b, s, 0)),
                   pl.BlockSpec((1, n_pairs, n_sub, PAIR, ATT_TILE), lambda b, s: (b, 0, s, 0, 0))],
        out_shape=[jax.ShapeDtypeStruct((B, D, S), BF16),
                   jax.ShapeDtypeStruct((B, S, D), BF16),
                   jax.ShapeDtypeStruct((B, n_pairs, nk, PAIR, ATT_TILE), BF16)],
        compiler_params=pltpu.CompilerParams(dimension_semantics=("parallel", "parallel"),
                                             vmem_limit_bytes=VMEM_LIMIT_BYTES),
        name="qkv_proj",
    )(h, g.reshape(1, D).astype(F32), wqT, wk.astype(BF16), wvT)


def _split_pairs(qT_ref, qz_ref):
    zeros = jnp.zeros((HEAD_DIM, qT_ref.shape[2]), qT_ref.dtype)
    for g in range(PAIRS_PER_STEP):
        q = qT_ref[0, g * PAIR:(g + 1) * PAIR, :]
        qz_ref[2 * g, :HEAD_DIM, :] = q[:HEAD_DIM]
        qz_ref[2 * g, HEAD_DIM:, :] = zeros
        qz_ref[2 * g + 1, :HEAD_DIM, :] = zeros
        qz_ref[2 * g + 1, HEAD_DIM:, :] = q[HEAD_DIM:]


def _update(m, scores, vT_tiles, biases, m_ref, l_ref, acc_ref, first):
    scores = [s if b is None else b + s for s, b in zip(scores, biases)]
    m_cur = jnp.max(functools.reduce(jnp.maximum, scores), axis=0, keepdims=True)
    if first:
        m_new = m_cur
    else:
        m_old = m_ref[m]
        m_new = jnp.maximum(m_old, m_cur)
    probs = [jnp.exp2(s - m_new) for s in scores]
    l_cur = jnp.sum(functools.reduce(jnp.add, probs), axis=0, keepdims=True)
    pv = functools.reduce(jnp.add, [jnp.dot(v, p.astype(BF16), preferred_element_type=F32)
                                    for v, p in zip(vT_tiles, probs)])
    if first:
        l_ref[m] = l_cur
        acc_ref[m] = pv
    else:
        alpha = jnp.exp2(m_old - m_new)
        l_ref[m] = alpha * l_ref[m] + l_cur
        acc_ref[m] = alpha * acc_ref[m] + pv
    m_ref[m] = m_new


def _sweep(qi, k_ref, vT_ref, v_rows, near_bias, far_bias, state, pre_ref):
    qz_ref, m_ref, l_ref, acc_ref = state
    n_maps = 2 * PAIRS_PER_STEP

    def raw_scores(j, m):
        rows = pl.ds(pl.multiple_of(j * ATT_TILE, ATT_TILE), ATT_TILE)
        k_t = k_ref[0, rows, (m // 2) * PAIR:(m // 2 + 1) * PAIR]
        return jnp.dot(k_t, qz_ref[m], preferred_element_type=F32)

    def block(tiles, biases, first, ahead, preloaded, next_tiles, next_ahead):
        scores = {}
        for m in range(ahead):
            scores[m] = ([pre_ref[m, t] for t in range(len(tiles))] if preloaded
                         else [raw_scores(j, m) for j in tiles])
        for m in range(n_maps):
            nxt = m + ahead
            if nxt < n_maps:
                scores[nxt] = [raw_scores(j, nxt) for j in tiles]
            elif nxt - n_maps < next_ahead:
                for t, j in enumerate(next_tiles):
                    pre_ref[nxt - n_maps, t] = raw_scores(j, nxt - n_maps)
            _update(m, scores.pop(m), [v_rows(vT_ref[0, m // 2, j], m) for j in tiles], biases(m),
                    m_ref, l_ref, acc_ref, first)

    n_far = jnp.maximum(qi - 1, 0)
    n_pairs = n_far // 2

    block([qi], lambda m: [near_bias(m, 0, qi)], True, QK_AHEAD, False, [n_far], QK_AHEAD)

    @pl.when(qi >= 1)
    def _():
        block([qi - 1], lambda m: [near_bias(m, 1, qi - 1)], False, QK_AHEAD, True, [0, 1], QK_AHEAD_PAIR)

    @pl.when(n_far % 2 == 1)
    def _():
        block([n_far - 1], lambda m: [far_bias(m, n_far - 1)], False, QK_AHEAD, False, [], 0)

    def far_pair(p, carry):
        j, jn = 2 * p, 2 * jnp.minimum(p + 1, n_pairs - 1)
        block([j, j + 1], lambda m: [far_bias(m, j), far_bias(m, j + 1)], False, QK_AHEAD_PAIR, True,
              [jn, jn + 1], QK_AHEAD_PAIR)
        return carry

    lax.fori_loop(0, n_pairs, far_pair, 0)


def _diff_kernel(qT_ref, k_ref, vT_ref, bias_ref, lam_ref, g_ref, o_ref,
                 qz_ref, m_ref, l_ref, pre_ref, acc_ref, *, lam_init):
    qi = pl.program_id(2)
    _split_pairs(qT_ref, qz_ref)
    _sweep(qi, k_ref, vT_ref, lambda v, m: v,
           lambda m, t, j: bias_ref[m, t], lambda m, j: None,
           (qz_ref, m_ref, l_ref, acc_ref), pre_ref)

    lp = lam_ref[...]
    lam = (jnp.exp(jnp.sum(lp[0:1] * lp[1:2], axis=-1, keepdims=True))
           - jnp.exp(jnp.sum(lp[2:3] * lp[3:4], axis=-1, keepdims=True)) + lam_init)
    for g in range(PAIRS_PER_STEP):
        o = (acc_ref[2 * g] * (1.0 / l_ref[2 * g])
             - lam * (acc_ref[2 * g + 1] * (1.0 / l_ref[2 * g + 1])))
        o = o * lax.rsqrt(jnp.mean(o * o, axis=0, keepdims=True) + SUBLN_EPS) * g_ref[...] * (1.0 - lam_init)
        o_ref[0, :, g * PAIR:(g + 1) * PAIR] = o.T.astype(BF16)


def _moba_kernel(qT_ref, k_ref, vT_ref, bias_ref, o_ref,
                 qz_ref, m_ref, l_ref, pre_ref, acc_ref, kmean_ref, selb_ref, *, nk):
    qi = pl.program_id(2)
    tq = qT_ref.shape[2]

    @pl.when(qi == 0)
    def _():
        for j in range(nk):
            kb = k_ref[0, j * MOBA_BLOCK:(j + 1) * MOBA_BLOCK, :].astype(F32)
            kmean_ref[j:j + 1, :] = jnp.mean(kb, axis=0, keepdims=True)

    _split_pairs(qT_ref, qz_ref)
    blk = lax.broadcasted_iota(jnp.int32, (nk, tq), 0).astype(F32)
    eligible = blk < qi.astype(F32)
    for m in range(2 * PAIRS_PER_STEP):
        rest = kmean_ref[:, (m // 2) * PAIR:(m // 2 + 1) * PAIR]
        gate = jnp.zeros((nk, tq), F32)
        for _ in range(KMEAN_TERMS):
            term = rest.astype(BF16)
            gate = gate + jnp.dot(term, qz_ref[m], preferred_element_type=F32)
            rest = rest - term.astype(F32)
        gate = jnp.where(eligible, gate, -jnp.inf)
        picked = jnp.zeros((nk, tq), F32)
        for _ in range(MOBA_TOPK):
            best = jnp.max(gate, axis=0, keepdims=True)
            pick = blk == jnp.min(jnp.where(gate == best, blk, float(nk)), axis=0, keepdims=True)
            picked = jnp.where(pick, 1.0, picked)
            gate = jnp.where(pick, -jnp.inf, gate)
        selb_ref[m] = jnp.where(eligible, jnp.where(picked > 0.0, 0.0, MASK_VALUE), MASK_VALUE)

    _sweep(qi, k_ref, vT_ref, lambda v, m: v[(m % 2) * HEAD_DIM:(m % 2 + 1) * HEAD_DIM],
           lambda m, t, j: bias_ref[m, t] if t == 0 else bias_ref[m, t] + selb_ref[m, pl.ds(j, 1), :],
           lambda m, j: selb_ref[m, pl.ds(j, 1), :],
           (qz_ref, m_ref, l_ref, acc_ref), pre_ref)

    for g in range(PAIRS_PER_STEP):
        o = jnp.concatenate([acc_ref[2 * g + i] * (1.0 / l_ref[2 * g + i]) for i in range(2)], axis=0)
        o_ref[0, :, g * PAIR:(g + 1) * PAIR] = o.T.astype(BF16)


def _attention(kind, qT, k, vT, bias, extra, *, lam_init=None):
    B, D, S = qT.shape
    G, nk, t = PAIRS_PER_STEP, S // ATT_TILE, ATT_TILE
    n_maps = 2 * G
    assert D % (G * PAIR) == 0 and nk >= 2 and QK_AHEAD_PAIR <= QK_AHEAD <= n_maps
    in_specs = [pl.BlockSpec((1, G * PAIR, t), lambda b, p, q: (b, p, q)),
                pl.BlockSpec((1, S, G * PAIR), lambda b, p, q: (b, 0, p)),
                pl.BlockSpec((1, G, nk, PAIR, t), lambda b, p, q: (b, p, 0, 0, 0)),
                pl.BlockSpec((n_maps, 2, t, t), lambda b, p, q: (p, 0, 0, 0), pipeline_mode=pl.Buffered(1))]
    scratch = [pltpu.VMEM((n_maps, PAIR, t), BF16),
               pltpu.VMEM((n_maps, 1, t), F32),
               pltpu.VMEM((n_maps, 1, t), F32),
               pltpu.VMEM((max(QK_AHEAD, QK_AHEAD_PAIR), 2, t, t), F32)]
    if kind == "diff":
        body = functools.partial(_diff_kernel, lam_init=lam_init)
        in_specs += [pl.BlockSpec(e.shape, lambda b, p, q: (0, 0)) for e in extra]
        scratch += [pltpu.VMEM((n_maps, PAIR, t), F32)]
    else:
        body = functools.partial(_moba_kernel, nk=nk)
        scratch += [pltpu.VMEM((n_maps, HEAD_DIM, t), F32),
                    pltpu.VMEM((nk, G * PAIR), F32),
                    pltpu.VMEM((n_maps, nk, t), F32)]
    return pl.pallas_call(
        body,
        grid=(B, D // (G * PAIR), nk),
        in_specs=in_specs,
        out_specs=pl.BlockSpec((1, t, G * PAIR), lambda b, p, q: (b, q, p)),
        out_shape=jax.ShapeDtypeStruct((B, S, D), BF16),
        scratch_shapes=scratch,
        compiler_params=pltpu.CompilerParams(dimension_semantics=("parallel", "parallel", "arbitrary"),
                                             vmem_limit_bytes=VMEM_LIMIT_BYTES),
        name=kind + "_attention",
    )(qT, k, vT, bias, *extra)


def kernel(x, rel_bias, norm_g, final_norm_g, ffn_w_in, ffn_w_out, diff_w_qkv, diff_lambda,
           diff_subln_g, diff_w_o, moba_w_qkv, moba_w_o):
    B, S, D = x.shape
    depth = norm_g.shape[0]
    assert S % TOKEN_TILE == 0 and D % PAIR == 0 and MOBA_BLOCK == ATT_TILE
    assert rel_bias.shape == (REL_BUCKETS, D // HEAD_DIM)
    bias = _bias_tiles(rel_bias, ATT_TILE)

    h = x.reshape(B * S, D)
    for i in range(depth):
        g = norm_g[i]
        h = _ffn(h, g[0], ffn_w_in[i, 0], ffn_w_out[i, 0])
        j = i // 2
        if i % 2 == 0:
            qT, k, vT = _qkv_proj(h.reshape(B, S, D), g[1], diff_w_qkv[j])
            lam_init = 0.8 - 0.6 * math.exp(-0.3 * i)
            g_sub = jnp.broadcast_to(diff_subln_g[j].astype(F32)[:, None], (PAIR, ATT_TILE))
            o = _attention("diff", qT, k, vT, bias, (diff_lambda[j].astype(F32), g_sub), lam_init=lam_init)
            w_o = diff_w_o[j]
        else:
            qT, k, vT = _qkv_proj(h.reshape(B, S, D), g[1], moba_w_qkv[j])
            o = _attention("moba", qT, k, vT, bias, ())
            w_o = moba_w_o[j]
        last = i == depth - 1
        h = _ffn(h, g[2], ffn_w_in[i, 1], ffn_w_out[i, 1], proj=(o.reshape(B * S, D), w_o),
                 final_g=final_norm_g if last else None)
    return h.reshape(B, S, D)
```

```python
import functools
import math

import numpy as np
import jax
import jax.numpy as jnp
from jax import lax
from jax.experimental import pallas as pl
from jax.experimental.pallas import tpu as pltpu

HEAD_DIM = 64
PAIR = 2 * HEAD_DIM
MOBA_BLOCK = 256
MOBA_TOPK = 3
KMEAN_TERMS = 3
ONES_ROWS = 16
REL_BUCKETS = 32
REL_MAX_DIST = 128
FFN_RESIDUAL = 0.5
RMS_EPS = 1e-6
SUBLN_EPS = 1e-5

ATT_TILE = 256
PAIRS_PER_STEP = 8
QK_AHEAD = 4
QK_AHEAD_PAIR = 2
FFN_CHUNK = 256
TOKEN_TILE = 512
MASK_VALUE = -1e30
LOG2E = math.log2(math.e)
VMEM_LIMIT_BYTES = 56 * 1024 * 1024

F32 = jnp.float32
BF16 = jnp.bfloat16
_NT = (((1,), (1,)), ((), ()))


def _rms(x, g, eps):
    return x * lax.rsqrt(jnp.mean(x * x, axis=-1, keepdims=True) + eps) * g


def _const_spec(shape):
    return pl.BlockSpec(shape, lambda *_: (0,) * len(shape), pipeline_mode=pl.Buffered(1))


def _rel_bucket_np(dist):
    n = np.maximum(dist, 0)
    max_exact = REL_BUCKETS // 2
    nf = np.maximum(n, 1).astype(np.float32)
    large = max_exact + (np.log(nf / np.float32(max_exact)) / np.float32(math.log(REL_MAX_DIST / max_exact))
                         * np.float32(REL_BUCKETS - max_exact)).astype(np.int32)
    large = np.minimum(large, REL_BUCKETS - 1)
    return np.where(n < max_exact, n, large).astype(np.int32)


def _bucket_tiles(t):
    j = np.arange(t)[:, None]
    i = np.arange(t)[None, :]
    diag = np.where(i - j >= 0, _rel_bucket_np(i - j), -1)
    prev = _rel_bucket_np(i - j + t)
    assert _rel_bucket_np(np.arange(t + 1, 8 * t)).min() == REL_BUCKETS - 1
    return np.stack([diag, prev]).astype(np.int32)


def _bias_kernel(rb_ref, idx_ref, out_ref):
    m = pl.program_id(0)
    far = rb_ref[REL_BUCKETS - 1, m]
    for t in range(2):
        idx = idx_ref[t]
        acc = jnp.zeros(idx.shape, F32)
        for b in range(REL_BUCKETS - 1):
            acc = jnp.where(idx == b, (rb_ref[b, m] - far) * LOG2E, acc)
        out_ref[0, t] = jnp.where(idx < 0, MASK_VALUE, acc)


def _bias_tiles(rel_bias, t):
    n_maps = rel_bias.shape[1]
    idx = jnp.asarray(_bucket_tiles(t))
    return pl.pallas_call(
        _bias_kernel,
        grid=(n_maps,),
        in_specs=[pl.BlockSpec(memory_space=pltpu.SMEM),
                  pl.BlockSpec((2, t, t), lambda m: (0, 0, 0))],
        out_specs=pl.BlockSpec((1, 2, t, t), lambda m: (m, 0, 0, 0)),
        out_shape=jax.ShapeDtypeStruct((n_maps, 2, t, t), F32),
        name="rel_bias_tiles",
    )(rel_bias.astype(F32), idx)


def _ffn_kernel(*refs, n_chunks, has_proj, has_final):
    refs = list(refs)
    x_ref = refs.pop(0)
    if has_proj:
        o_ref, wo_ref = refs.pop(0), refs.pop(0)
    g_ref, win_ref, wout_ref = refs.pop(0), refs.pop(0), refs.pop(0)
    if has_final:
        gf_ref = refs.pop(0)
    (out_ref,) = refs

    x = x_ref[...]
    if has_proj:
        x = x + jnp.dot(o_ref[...], wo_ref[...], preferred_element_type=F32)
    hn = _rms(x, g_ref[...], RMS_EPS).astype(BF16)
    acc = None
    d_ff = n_chunks * FFN_CHUNK
    for c in range(n_chunks):
        cols = slice(c * FFN_CHUNK, (c + 1) * FFN_CHUNK)
        gate = jnp.dot(hn, win_ref[:, cols], preferred_element_type=F32)
        up = jnp.dot(hn, win_ref[:, d_ff + c * FFN_CHUNK:d_ff + (c + 1) * FFN_CHUNK], preferred_element_type=F32)
        a = (gate * (1.0 / (1.0 + jnp.exp(-gate))) * up).astype(BF16)
        part = jnp.dot(a, wout_ref[cols, :], preferred_element_type=F32)
        acc = part if acc is None else acc + part
    y = x + FFN_RESIDUAL * acc
    if has_final:
        y = _rms(y, gf_ref[...], RMS_EPS)
    out_ref[...] = y


def _ffn(x2d, g, w_in, w_out, proj=None, final_g=None):
    T, D = x2d.shape
    F = w_out.shape[0]
    n_chunks = F // FFN_CHUNK
    assert n_chunks * FFN_CHUNK == F and T % TOKEN_TILE == 0
    tm = TOKEN_TILE
    w_in_c = w_in.astype(BF16)
    w_out_c = w_out.astype(BF16)

    row = lambda i: (i, 0)
    args, specs = [x2d], [pl.BlockSpec((tm, D), row)]
    if proj is not None:
        o2d, w_o = proj
        args += [o2d, w_o.astype(BF16)]
        specs += [pl.BlockSpec((tm, o2d.shape[1]), row), _const_spec(w_o.shape)]
    args += [g.reshape(1, D).astype(F32), w_in_c, w_out_c]
    specs += [_const_spec((1, D)), _const_spec(w_in_c.shape), _const_spec(w_out_c.shape)]
    if final_g is not None:
        args.append(final_g.reshape(1, D).astype(F32))
        specs.append(_const_spec((1, D)))

    return pl.pallas_call(
        functools.partial(_ffn_kernel, n_chunks=n_chunks, has_proj=proj is not None,
                          has_final=final_g is not None),
        grid=(T // tm,),
        in_specs=specs,
        out_specs=pl.BlockSpec((tm, D), row),
        out_shape=jax.ShapeDtypeStruct((T, D), F32),
        compiler_params=pltpu.CompilerParams(dimension_semantics=("parallel",),
                                             vmem_limit_bytes=VMEM_LIMIT_BYTES),
        name="ffn",
    )(*args)


def _proj_kernel(x_ref, g_ref, wqT_ref, wk_ref, wvT_ref, qT_ref, k_ref, vT_ref, *, n_pairs, n_sub, v_dim):
    hn = _rms(x_ref[0], g_ref[...], RMS_EPS).astype(BF16)
    k_ref[0] = jnp.dot(hn, wk_ref[...], preferred_element_type=F32).astype(BF16)
    qT_ref[0] = lax.dot_general(wqT_ref[...], hn, _NT, preferred_element_type=F32).astype(BF16)
    vT = lax.dot_general(wvT_ref[...], hn, _NT, preferred_element_type=F32).astype(BF16)
    ones_rows = jnp.where(lax.broadcasted_iota(jnp.int32, (ONES_ROWS, ATT_TILE), 0) == 0, 1.0, 0.0).astype(BF16)
    group = v_dim + ONES_ROWS
    for p in range(n_pairs):
        for c in range(n_sub):
            cols = slice(c * ATT_TILE, (c + 1) * ATT_TILE)
            for i in range(PAIR // v_dim):
                vT_ref[0, p, c, i * group:i * group + v_dim, :] = vT[p * PAIR + i * v_dim:p * PAIR + (i + 1) * v_dim, cols]
                vT_ref[0, p, c, i * group + v_dim:(i + 1) * group, :] = ones_rows


def _v_rows(v_dim):
    return (PAIR // v_dim) * (v_dim + ONES_ROWS)


def _qkv_proj(h, g, w_qkv, v_dim):
    B, S, D = h.shape
    tm = TOKEN_TILE
    n_pairs, n_sub, nk, rows = D // PAIR, tm // ATT_TILE, S // ATT_TILE, _v_rows(v_dim)
    wq, wk, wv = w_qkv[:, :D], w_qkv[:, D:2 * D], w_qkv[:, 2 * D:]
    wqT = (wq * (HEAD_DIM ** -0.5 * LOG2E)).T.astype(BF16)
    wvT = wv.T.astype(BF16)
    return pl.pallas_call(
        functools.partial(_proj_kernel, n_pairs=n_pairs, n_sub=n_sub, v_dim=v_dim),
        grid=(B, S // tm),
        in_specs=[pl.BlockSpec((1, tm, D), lambda b, s: (b, s, 0)),
                  _const_spec((1, D)), _const_spec((D, D)), _const_spec((D, D)), _const_spec((D, D))],
        out_specs=[pl.BlockSpec((1, D, tm), lambda b, s: (b, 0, s)),
                   pl.BlockSpec((1, tm, D), lambda b, s: (b, s, 0)),
                   pl.BlockSpec((1, n_pairs, n_sub, rows, ATT_TILE), lambda b, s: (b, 0, s, 0, 0))],
        out_shape=[jax.ShapeDtypeStruct((B, D, S), BF16),
                   jax.ShapeDtypeStruct((B, S, D), BF16),
                   jax.ShapeDtypeStruct((B, n_pairs, nk, rows, ATT_TILE), BF16)],
        compiler_params=pltpu.CompilerParams(dimension_semantics=("parallel", "parallel"),
                                             vmem_limit_bytes=VMEM_LIMIT_BYTES),
        name="qkv_proj",
    )(h, g.reshape(1, D).astype(F32), wqT, wk.astype(BF16), wvT)


def _split_pairs(qT_ref, qz_ref):
    zeros = jnp.zeros((HEAD_DIM, qT_ref.shape[2]), qT_ref.dtype)
    for g in range(PAIRS_PER_STEP):
        q = qT_ref[0, g * PAIR:(g + 1) * PAIR, :]
        qz_ref[2 * g, :HEAD_DIM, :] = q[:HEAD_DIM]
        qz_ref[2 * g, HEAD_DIM:, :] = zeros
        qz_ref[2 * g + 1, :HEAD_DIM, :] = zeros
        qz_ref[2 * g + 1, HEAD_DIM:, :] = q[HEAD_DIM:]


def _update(m, scores, vT_tiles, biases, m_ref, acc_ref, first):
    scores = [s if b is None else b + s for s, b in zip(scores, biases)]
    m_cur = jnp.max(functools.reduce(jnp.maximum, scores), axis=0, keepdims=True)
    if first:
        m_new = m_cur
    else:
        m_old = m_ref[m]
        m_new = jnp.maximum(m_old, m_cur)
    pv = functools.reduce(jnp.add, [jnp.dot(v, jnp.exp2(s - m_new).astype(BF16), preferred_element_type=F32)
                                    for v, s in zip(vT_tiles, scores)])
    if first:
        acc_ref[m] = pv
    else:
        acc_ref[m] = jnp.exp2(m_old - m_new) * acc_ref[m] + pv
    m_ref[m] = m_new


def _sweep(qi, k_ref, vT_ref, v_rows, near_bias, far_bias, state, pre_ref):
    qz_ref, m_ref, acc_ref = state
    n_maps = 2 * PAIRS_PER_STEP

    def raw_scores(j, m):
        rows = pl.ds(pl.multiple_of(j * ATT_TILE, ATT_TILE), ATT_TILE)
        k_t = k_ref[0, rows, (m // 2) * PAIR:(m // 2 + 1) * PAIR]
        return jnp.dot(k_t, qz_ref[m], preferred_element_type=F32)

    def block(tiles, biases, first, ahead, preloaded, next_tiles, next_ahead):
        scores = {}
        for m in range(ahead):
            scores[m] = ([pre_ref[m, t] for t in range(len(tiles))] if preloaded
                         else [raw_scores(j, m) for j in tiles])
        for m in range(n_maps):
            nxt = m + ahead
            if nxt < n_maps:
                scores[nxt] = [raw_scores(j, nxt) for j in tiles]
            elif nxt - n_maps < next_ahead:
                for t, j in enumerate(next_tiles):
                    pre_ref[nxt - n_maps, t] = raw_scores(j, nxt - n_maps)
            _update(m, scores.pop(m), [v_rows(vT_ref[0, m // 2, j], m) for j in tiles], biases(m),
                    m_ref, acc_ref, first)

    n_far = jnp.maximum(qi - 1, 0)
    n_pairs = n_far // 2

    block([qi], lambda m: [near_bias(m, 0, qi)], True, QK_AHEAD, False, [n_far], QK_AHEAD)

    @pl.when(qi >= 1)
    def _():
        block([qi - 1], lambda m: [near_bias(m, 1, qi - 1)], False, QK_AHEAD, True, [0, 1], QK_AHEAD_PAIR)

    @pl.when(n_far % 2 == 1)
    def _():
        block([n_far - 1], lambda m: [far_bias(m, n_far - 1)], False, QK_AHEAD, False, [], 0)

    def far_pair(p, carry):
        j, jn = 2 * p, 2 * jnp.minimum(p + 1, n_pairs - 1)
        block([j, j + 1], lambda m: [far_bias(m, j), far_bias(m, j + 1)], False, QK_AHEAD_PAIR, True,
              [jn, jn + 1], QK_AHEAD_PAIR)
        return carry

    lax.fori_loop(0, n_pairs, far_pair, 0)


def _diff_kernel(qT_ref, k_ref, vT_ref, bias_ref, lam_ref, g_ref, o_ref,
                 qz_ref, m_ref, pre_ref, acc_ref, *, lam_init):
    qi = pl.program_id(2)
    _split_pairs(qT_ref, qz_ref)
    _sweep(qi, k_ref, vT_ref, lambda v, m: v,
           lambda m, t, j: bias_ref[m, t], lambda m, j: None,
           (qz_ref, m_ref, acc_ref), pre_ref)

    lp = lam_ref[...]
    lam = (jnp.exp(jnp.sum(lp[0:1] * lp[1:2], axis=-1, keepdims=True))
           - jnp.exp(jnp.sum(lp[2:3] * lp[3:4], axis=-1, keepdims=True)) + lam_init)
    def normalized(m):
        return acc_ref[m, :PAIR, :] * (1.0 / acc_ref[m, PAIR:PAIR + 1, :])

    for g in range(PAIRS_PER_STEP):
        o = normalized(2 * g) - lam * normalized(2 * g + 1)
        o = o * lax.rsqrt(jnp.mean(o * o, axis=0, keepdims=True) + SUBLN_EPS) * g_ref[...] * (1.0 - lam_init)
        o_ref[0, :, g * PAIR:(g + 1) * PAIR] = o.T.astype(BF16)


def _moba_kernel(qT_ref, k_ref, vT_ref, bias_ref, o_ref,
                 qz_ref, m_ref, pre_ref, acc_ref, kmean_ref, selb_ref, *, nk):
    qi = pl.program_id(2)
    tq = qT_ref.shape[2]

    @pl.when(qi == 0)
    def _():
        for j in range(nk):
            kb = k_ref[0, j * MOBA_BLOCK:(j + 1) * MOBA_BLOCK, :].astype(F32)
            kmean_ref[j:j + 1, :] = jnp.mean(kb, axis=0, keepdims=True)

    _split_pairs(qT_ref, qz_ref)
    blk = lax.broadcasted_iota(jnp.int32, (nk, tq), 0).astype(F32)
    eligible = blk < qi.astype(F32)
    for m in range(2 * PAIRS_PER_STEP):
        rest = kmean_ref[:, (m // 2) * PAIR:(m // 2 + 1) * PAIR]
        gate = jnp.zeros((nk, tq), F32)
        for _ in range(KMEAN_TERMS):
            term = rest.astype(BF16)
            gate = gate + jnp.dot(term, qz_ref[m], preferred_element_type=F32)
            rest = rest - term.astype(F32)
        gate = jnp.where(eligible, gate, -jnp.inf)
        picked = jnp.zeros((nk, tq), F32)
        for _ in range(MOBA_TOPK):
            best = jnp.max(gate, axis=0, keepdims=True)
            pick = blk == jnp.min(jnp.where(gate == best, blk, float(nk)), axis=0, keepdims=True)
            picked = jnp.where(pick, 1.0, picked)
            gate = jnp.where(pick, -jnp.inf, gate)
        selb_ref[m] = jnp.where(eligible, jnp.where(picked > 0.0, 0.0, MASK_VALUE), MASK_VALUE)

    head_rows = HEAD_DIM + ONES_ROWS
    _sweep(qi, k_ref, vT_ref, lambda v, m: v[(m % 2) * head_rows:(m % 2 + 1) * head_rows],
           lambda m, t, j: bias_ref[m, t] if t == 0 else bias_ref[m, t] + selb_ref[m, pl.ds(j, 1), :],
           lambda m, j: selb_ref[m, pl.ds(j, 1), :],
           (qz_ref, m_ref, acc_ref), pre_ref)

    for g in range(PAIRS_PER_STEP):
        o = jnp.concatenate([acc_ref[m, :HEAD_DIM, :] * (1.0 / acc_ref[m, HEAD_DIM:HEAD_DIM + 1, :])
                             for m in (2 * g, 2 * g + 1)], axis=0)
        o_ref[0, :, g * PAIR:(g + 1) * PAIR] = o.T.astype(BF16)


def _attention(kind, qT, k, vT, bias, extra, *, lam_init=None):
    B, D, S = qT.shape
    G, nk, t = PAIRS_PER_STEP, S // ATT_TILE, ATT_TILE
    n_maps = 2 * G
    assert D % (G * PAIR) == 0 and nk >= 2 and QK_AHEAD_PAIR <= QK_AHEAD <= n_maps
    in_specs = [pl.BlockSpec((1, G * PAIR, t), lambda b, p, q: (b, p, q)),
                pl.BlockSpec((1, S, G * PAIR), lambda b, p, q: (b, 0, p)),
                pl.BlockSpec((1, G, nk, vT.shape[3], t), lambda b, p, q: (b, p, 0, 0, 0)),
                pl.BlockSpec((n_maps, 2, t, t), lambda b, p, q: (p, 0, 0, 0), pipeline_mode=pl.Buffered(1))]
    scratch = [pltpu.VMEM((n_maps, PAIR, t), BF16),
               pltpu.VMEM((n_maps, 1, t), F32),
               pltpu.VMEM((max(QK_AHEAD, QK_AHEAD_PAIR), 2, t, t), F32)]
    if kind == "diff":
        body = functools.partial(_diff_kernel, lam_init=lam_init)
        in_specs += [pl.BlockSpec(e.shape, lambda b, p, q: (0, 0)) for e in extra]
        scratch += [pltpu.VMEM((n_maps, PAIR + ONES_ROWS, t), F32)]
    else:
        body = functools.partial(_moba_kernel, nk=nk)
        scratch += [pltpu.VMEM((n_maps, HEAD_DIM + ONES_ROWS, t), F32),
                    pltpu.VMEM((nk, G * PAIR), F32),
                    pltpu.VMEM((n_maps, nk, t), F32)]
    return pl.pallas_call(
        body,
        grid=(B, D // (G * PAIR), nk),
        in_specs=in_specs,
        out_specs=pl.BlockSpec((1, t, G * PAIR), lambda b, p, q: (b, q, p)),
        out_shape=jax.ShapeDtypeStruct((B, S, D), BF16),
        scratch_shapes=scratch,
        compiler_params=pltpu.CompilerParams(dimension_semantics=("parallel", "parallel", "arbitrary"),
                                             vmem_limit_bytes=VMEM_LIMIT_BYTES),
        name=kind + "_attention",
    )(qT, k, vT, bias, *extra)


def kernel(x, rel_bias, norm_g, final_norm_g, ffn_w_in, ffn_w_out, diff_w_qkv, diff_lambda,
           diff_subln_g, diff_w_o, moba_w_qkv, moba_w_o):
    B, S, D = x.shape
    depth = norm_g.shape[0]
    assert S % TOKEN_TILE == 0 and D % PAIR == 0 and MOBA_BLOCK == ATT_TILE
    assert rel_bias.shape == (REL_BUCKETS, D // HEAD_DIM)
    bias = _bias_tiles(rel_bias, ATT_TILE)

    h = x.reshape(B * S, D)
    for i in range(depth):
        g = norm_g[i]
        h = _ffn(h, g[0], ffn_w_in[i, 0], ffn_w_out[i, 0])
        j = i // 2
        if i % 2 == 0:
            qT, k, vT = _qkv_proj(h.reshape(B, S, D), g[1], diff_w_qkv[j], PAIR)
            lam_init = 0.8 - 0.6 * math.exp(-0.3 * i)
            g_sub = jnp.broadcast_to(diff_subln_g[j].astype(F32)[:, None], (PAIR, ATT_TILE))
            o = _attention("diff", qT, k, vT, bias, (diff_lambda[j].astype(F32), g_sub), lam_init=lam_init)
            w_o = diff_w_o[j]
        else:
            qT, k, vT = _qkv_proj(h.reshape(B, S, D), g[1], moba_w_qkv[j], HEAD_DIM)
            o = _attention("moba", qT, k, vT, bias, ())
            w_o = moba_w_o[j]
        last = i == depth - 1
        h = _ffn(h, g[2], ffn_w_in[i, 1], ffn_w_out[i, 1], proj=(o.reshape(B * S, D), w_o),
                 final_g=final_norm_g if last else None)
    return h.reshape(B, S, D)
```

```python
import functools
import math

import numpy as np
import jax
import jax.numpy as jnp
from jax import lax
from jax.experimental import pallas as pl
from jax.experimental.pallas import tpu as pltpu

HEAD_DIM = 64
PAIR = 2 * HEAD_DIM
MOBA_BLOCK = 256
MOBA_TOPK = 3
KMEAN_TERMS = 3
ONES_ROWS = 16
REL_BUCKETS = 32
REL_MAX_DIST = 128
FFN_RESIDUAL = 0.5
RMS_EPS = 1e-6
SUBLN_EPS = 1e-5

ATT_TILE = 256
PAIRS_PER_STEP = 8
QK_AHEAD = 4
QK_AHEAD_PAIR = 2
FFN_CHUNK = 256
TOKEN_TILE = 512
FFN_TOKEN_TILE = 1024
MASK_VALUE = -1e30
LOG2E = math.log2(math.e)
VMEM_LIMIT_BYTES = 56 * 1024 * 1024

F32 = jnp.float32
BF16 = jnp.bfloat16
_NT = (((1,), (1,)), ((), ()))


def _rms(x, g, eps):
    return x * lax.rsqrt(jnp.mean(x * x, axis=-1, keepdims=True) + eps) * g


def _const_spec(shape):
    return pl.BlockSpec(shape, lambda *_: (0,) * len(shape), pipeline_mode=pl.Buffered(1))


def _rel_bucket_np(dist):
    n = np.maximum(dist, 0)
    max_exact = REL_BUCKETS // 2
    nf = np.maximum(n, 1).astype(np.float32)
    large = max_exact + (np.log(nf / np.float32(max_exact)) / np.float32(math.log(REL_MAX_DIST / max_exact))
                         * np.float32(REL_BUCKETS - max_exact)).astype(np.int32)
    large = np.minimum(large, REL_BUCKETS - 1)
    return np.where(n < max_exact, n, large).astype(np.int32)


def _bucket_tiles(t):
    j = np.arange(t)[:, None]
    i = np.arange(t)[None, :]
    diag = np.where(i - j >= 0, _rel_bucket_np(i - j), -1)
    prev = _rel_bucket_np(i - j + t)
    assert _rel_bucket_np(np.arange(t + 1, 8 * t)).min() == REL_BUCKETS - 1
    return np.stack([diag, prev]).astype(np.int32)


def _bias_kernel(rb_ref, idx_ref, out_ref):
    m = pl.program_id(0)
    far = rb_ref[REL_BUCKETS - 1, m]
    for t in range(2):
        idx = idx_ref[t]
        acc = jnp.zeros(idx.shape, F32)
        for b in range(REL_BUCKETS - 1):
            acc = jnp.where(idx == b, (rb_ref[b, m] - far) * LOG2E, acc)
        out_ref[0, t] = jnp.where(idx < 0, MASK_VALUE, acc)


def _bias_tiles(rel_bias, t):
    n_maps = rel_bias.shape[1]
    idx = jnp.asarray(_bucket_tiles(t))
    return pl.pallas_call(
        _bias_kernel,
        grid=(n_maps,),
        in_specs=[pl.BlockSpec(memory_space=pltpu.SMEM),
                  pl.BlockSpec((2, t, t), lambda m: (0, 0, 0))],
        out_specs=pl.BlockSpec((1, 2, t, t), lambda m: (m, 0, 0, 0)),
        out_shape=jax.ShapeDtypeStruct((n_maps, 2, t, t), F32),
        name="rel_bias_tiles",
    )(rel_bias.astype(F32), idx)


def _ffn_kernel(*refs, n_chunks, has_proj, has_final):
    refs = list(refs)
    x_ref = refs.pop(0)
    if has_proj:
        o_ref, wo_ref = refs.pop(0), refs.pop(0)
    g_ref, win_ref, wout_ref = refs.pop(0), refs.pop(0), refs.pop(0)
    if has_final:
        gf_ref = refs.pop(0)
    (out_ref,) = refs

    x = x_ref[...]
    if has_proj:
        x = x + jnp.dot(o_ref[...], wo_ref[...], preferred_element_type=F32)
    hn = _rms(x, g_ref[...], RMS_EPS).astype(BF16)
    acc = None
    d_ff = n_chunks * FFN_CHUNK
    for c in range(n_chunks):
        cols = slice(c * FFN_CHUNK, (c + 1) * FFN_CHUNK)
        gate = jnp.dot(hn, win_ref[:, cols], preferred_element_type=F32)
        up = jnp.dot(hn, win_ref[:, d_ff + c * FFN_CHUNK:d_ff + (c + 1) * FFN_CHUNK], preferred_element_type=F32)
        a = (gate * (1.0 / (1.0 + jnp.exp(-gate))) * up).astype(BF16)
        part = jnp.dot(a, wout_ref[cols, :], preferred_element_type=F32)
        acc = part if acc is None else acc + part
    y = x + FFN_RESIDUAL * acc
    if has_final:
        y = _rms(y, gf_ref[...], RMS_EPS)
    out_ref[...] = y


def _ffn(x2d, g, w_in, w_out, proj=None, final_g=None):
    T, D = x2d.shape
    F = w_out.shape[0]
    n_chunks = F // FFN_CHUNK
    assert n_chunks * FFN_CHUNK == F
    tm = math.gcd(T, FFN_TOKEN_TILE)
    w_in_c = w_in.astype(BF16)
    w_out_c = w_out.astype(BF16)

    row = lambda i: (i, 0)
    args, specs = [x2d], [pl.BlockSpec((tm, D), row)]
    if proj is not None:
        o2d, w_o = proj
        args += [o2d, w_o.astype(BF16)]
        specs += [pl.BlockSpec((tm, o2d.shape[1]), row), _const_spec(w_o.shape)]
    args += [g.reshape(1, D).astype(F32), w_in_c, w_out_c]
    specs += [_const_spec((1, D)), _const_spec(w_in_c.shape), _const_spec(w_out_c.shape)]
    if final_g is not None:
        args.append(final_g.reshape(1, D).astype(F32))
        specs.append(_const_spec((1, D)))

    return pl.pallas_call(
        functools.partial(_ffn_kernel, n_chunks=n_chunks, has_proj=proj is not None,
                          has_final=final_g is not None),
        grid=(T // tm,),
        in_specs=specs,
        out_specs=pl.BlockSpec((tm, D), row),
        out_shape=jax.ShapeDtypeStruct((T, D), F32),
        compiler_params=pltpu.CompilerParams(dimension_semantics=("parallel",),
                                             vmem_limit_bytes=VMEM_LIMIT_BYTES),
        name="ffn",
    )(*args)


def _proj_kernel(x_ref, g_ref, wqT_ref, wk_ref, wvT_ref, qT_ref, k_ref, vT_ref, *, n_pairs, n_sub, v_dim):
    hn = _rms(x_ref[0], g_ref[...], RMS_EPS).astype(BF16)
    k_ref[0] = jnp.dot(hn, wk_ref[...], preferred_element_type=F32).astype(BF16)
    qT_ref[0] = lax.dot_general(wqT_ref[...], hn, _NT, preferred_element_type=F32).astype(BF16)
    vT = lax.dot_general(wvT_ref[...], hn, _NT, preferred_element_type=F32).astype(BF16)
    ones_rows = jnp.where(lax.broadcasted_iota(jnp.int32, (ONES_ROWS, ATT_TILE), 0) == 0, 1.0, 0.0).astype(BF16)
    group = v_dim + ONES_ROWS
    for p in range(n_pairs):
        for c in range(n_sub):
            cols = slice(c * ATT_TILE, (c + 1) * ATT_TILE)
            for i in range(PAIR // v_dim):
                vT_ref[0, p, c, i * group:i * group + v_dim, :] = vT[p * PAIR + i * v_dim:p * PAIR + (i + 1) * v_dim, cols]
                vT_ref[0, p, c, i * group + v_dim:(i + 1) * group, :] = ones_rows


def _v_rows(v_dim):
    return (PAIR // v_dim) * (v_dim + ONES_ROWS)


def _qkv_proj(h, g, w_qkv, v_dim):
    B, S, D = h.shape
    tm = TOKEN_TILE
    n_pairs, n_sub, nk, rows = D // PAIR, tm // ATT_TILE, S // ATT_TILE, _v_rows(v_dim)
    wq, wk, wv = w_qkv[:, :D], w_qkv[:, D:2 * D], w_qkv[:, 2 * D:]
    wqT = (wq * (HEAD_DIM ** -0.5 * LOG2E)).T.astype(BF16)
    wvT = wv.T.astype(BF16)
    return pl.pallas_call(
        functools.partial(_proj_kernel, n_pairs=n_pairs, n_sub=n_sub, v_dim=v_dim),
        grid=(B, S // tm),
        in_specs=[pl.BlockSpec((1, tm, D), lambda b, s: (b, s, 0)),
                  _const_spec((1, D)), _const_spec((D, D)), _const_spec((D, D)), _const_spec((D, D))],
        out_specs=[pl.BlockSpec((1, D, tm), lambda b, s: (b, 0, s)),
                   pl.BlockSpec((1, tm, D), lambda b, s: (b, s, 0)),
                   pl.BlockSpec((1, n_pairs, n_sub, rows, ATT_TILE), lambda b, s: (b, 0, s, 0, 0))],
        out_shape=[jax.ShapeDtypeStruct((B, D, S), BF16),
                   jax.ShapeDtypeStruct((B, S, D), BF16),
                   jax.ShapeDtypeStruct((B, n_pairs, nk, rows, ATT_TILE), BF16)],
        compiler_params=pltpu.CompilerParams(dimension_semantics=("parallel", "parallel"),
                                             vmem_limit_bytes=VMEM_LIMIT_BYTES),
        name="qkv_proj",
    )(h, g.reshape(1, D).astype(F32), wqT, wk.astype(BF16), wvT)


def _split_pairs(qT_ref, qz_ref):
    zeros = jnp.zeros((HEAD_DIM, qT_ref.shape[2]), qT_ref.dtype)
    for g in range(PAIRS_PER_STEP):
        q = qT_ref[0, g * PAIR:(g + 1) * PAIR, :]
        qz_ref[2 * g, :HEAD_DIM, :] = q[:HEAD_DIM]
        qz_ref[2 * g, HEAD_DIM:, :] = zeros
        qz_ref[2 * g + 1, :HEAD_DIM, :] = zeros
        qz_ref[2 * g + 1, HEAD_DIM:, :] = q[HEAD_DIM:]


def _update(m, scores, vT_tiles, biases, m_ref, acc_ref, first):
    scores = [s if b is None else b + s for s, (b, _) in zip(scores, biases)]
    masks = [r for _, r in biases]
    if all(r is None for r in masks):
        m_cur = jnp.max(functools.reduce(jnp.maximum, scores), axis=0, keepdims=True)
    else:
        m_cur = functools.reduce(jnp.maximum, [jnp.max(s, axis=0, keepdims=True) + (0.0 if r is None else r)
                                               for s, r in zip(scores, masks)])
    if first:
        m_new = m_cur
    else:
        m_old = m_ref[m]
        m_new = jnp.maximum(m_old, m_cur)
    shifts = [m_new if r is None else m_new - r for r in masks]
    pv = functools.reduce(jnp.add, [jnp.dot(v, jnp.exp2(s - sh).astype(BF16), preferred_element_type=F32)
                                    for v, s, sh in zip(vT_tiles, scores, shifts)])
    if first:
        acc_ref[m] = pv
    else:
        acc_ref[m] = jnp.exp2(m_old - m_new) * acc_ref[m] + pv
    m_ref[m] = m_new


def _sweep(qi, k_ref, vT_ref, v_rows, near_bias, far_bias, state, pre_ref):
    qz_ref, m_ref, acc_ref = state
    n_maps = 2 * PAIRS_PER_STEP

    def raw_scores(j, m):
        rows = pl.ds(pl.multiple_of(j * ATT_TILE, ATT_TILE), ATT_TILE)
        k_t = k_ref[0, rows, (m // 2) * PAIR:(m // 2 + 1) * PAIR]
        return jnp.dot(k_t, qz_ref[m], preferred_element_type=F32)

    def block(tiles, biases, first, ahead, preloaded, next_tiles, next_ahead):
        scores = {}
        for m in range(ahead):
            scores[m] = ([pre_ref[m, t] for t in range(len(tiles))] if preloaded
                         else [raw_scores(j, m) for j in tiles])
        for m in range(n_maps):
            nxt = m + ahead
            if nxt < n_maps:
                scores[nxt] = [raw_scores(j, nxt) for j in tiles]
            elif nxt - n_maps < next_ahead:
                for t, j in enumerate(next_tiles):
                    pre_ref[nxt - n_maps, t] = raw_scores(j, nxt - n_maps)
            _update(m, scores.pop(m), [v_rows(vT_ref[0, m // 2, j], m) for j in tiles], biases(m),
                    m_ref, acc_ref, first)

    n_far = jnp.maximum(qi - 1, 0)
    n_pairs = n_far // 2

    block([qi], lambda m: [near_bias(m, 0, qi)], True, QK_AHEAD, False, [n_far], QK_AHEAD)

    @pl.when(qi >= 1)
    def _():
        block([qi - 1], lambda m: [near_bias(m, 1, qi - 1)], False, QK_AHEAD, True, [0, 1], QK_AHEAD_PAIR)

    @pl.when(n_far % 2 == 1)
    def _():
        block([n_far - 1], lambda m: [far_bias(m, n_far - 1)], False, QK_AHEAD, False, [], 0)

    def far_pair(p, carry):
        j, jn = 2 * p, 2 * jnp.minimum(p + 1, n_pairs - 1)
        block([j, j + 1], lambda m: [far_bias(m, j), far_bias(m, j + 1)], False, QK_AHEAD_PAIR, True,
              [jn, jn + 1], QK_AHEAD_PAIR)
        return carry

    lax.fori_loop(0, n_pairs, far_pair, 0)


def _diff_kernel(qT_ref, k_ref, vT_ref, bias_ref, lam_ref, g_ref, o_ref,
                 qz_ref, m_ref, pre_ref, acc_ref, *, lam_init):
    qi = pl.program_id(2)
    _split_pairs(qT_ref, qz_ref)
    _sweep(qi, k_ref, vT_ref, lambda v, m: v,
           lambda m, t, j: (bias_ref[m, t], None), lambda m, j: (None, None),
           (qz_ref, m_ref, acc_ref), pre_ref)

    lp = lam_ref[...]
    lam = (jnp.exp(jnp.sum(lp[0:1] * lp[1:2], axis=-1, keepdims=True))
           - jnp.exp(jnp.sum(lp[2:3] * lp[3:4], axis=-1, keepdims=True)) + lam_init)

    def weighted(m, scale):
        return acc_ref[m, :PAIR, :] * (scale / acc_ref[m, PAIR:PAIR + 1, :])

    for g in range(PAIRS_PER_STEP):
        o = weighted(2 * g, 1.0) - weighted(2 * g + 1, lam)
        o = o * lax.rsqrt(jnp.mean(o * o, axis=0, keepdims=True) + SUBLN_EPS) * g_ref[...]
        o_ref[0, :, g * PAIR:(g + 1) * PAIR] = o.T.astype(BF16)


def _moba_kernel(qT_ref, k_ref, vT_ref, bias_ref, o_ref,
                 qz_ref, m_ref, pre_ref, acc_ref, kmean_ref, selb_ref, *, nk):
    qi = pl.program_id(2)
    tq = qT_ref.shape[2]

    @pl.when(qi == 0)
    def _():
        for j in range(nk):
            kb = k_ref[0, j * MOBA_BLOCK:(j + 1) * MOBA_BLOCK, :].astype(F32)
            kmean_ref[j:j + 1, :] = jnp.mean(kb, axis=0, keepdims=True)

    _split_pairs(qT_ref, qz_ref)
    blk = lax.broadcasted_iota(jnp.int32, (nk, tq), 0).astype(F32)
    eligible = blk < qi.astype(F32)
    for m in range(2 * PAIRS_PER_STEP):
        rest = kmean_ref[:, (m // 2) * PAIR:(m // 2 + 1) * PAIR]
        gate = jnp.zeros((nk, tq), F32)
        for _ in range(KMEAN_TERMS):
            term = rest.astype(BF16)
            gate = gate + jnp.dot(term, qz_ref[m], preferred_element_type=F32)
            rest = rest - term.astype(F32)
        gate = jnp.where(eligible, gate, -jnp.inf)
        picked = jnp.zeros((nk, tq), F32)
        for _ in range(MOBA_TOPK):
            best = jnp.max(gate, axis=0, keepdims=True)
            pick = blk == jnp.min(jnp.where(gate == best, blk, float(nk)), axis=0, keepdims=True)
            picked = jnp.where(pick, 1.0, picked)
            gate = jnp.where(pick, -jnp.inf, gate)
        selb_ref[m] = jnp.where(eligible, jnp.where(picked > 0.0, 0.0, MASK_VALUE), MASK_VALUE)

    head_rows = HEAD_DIM + ONES_ROWS
    _sweep(qi, k_ref, vT_ref, lambda v, m: v[(m % 2) * head_rows:(m % 2 + 1) * head_rows],
           lambda m, t, j: (bias_ref[m, t], None if t == 0 else selb_ref[m, pl.ds(j, 1), :]),
           lambda m, j: (None, selb_ref[m, pl.ds(j, 1), :]),
           (qz_ref, m_ref, acc_ref), pre_ref)

    for g in range(PAIRS_PER_STEP):
        o = jnp.concatenate([acc_ref[m, :HEAD_DIM, :] * (1.0 / acc_ref[m, HEAD_DIM:HEAD_DIM + 1, :])
                             for m in (2 * g, 2 * g + 1)], axis=0)
        o_ref[0, :, g * PAIR:(g + 1) * PAIR] = o.T.astype(BF16)


def _attention(kind, qT, k, vT, bias, extra, *, lam_init=None):
    B, D, S = qT.shape
    G, nk, t = PAIRS_PER_STEP, S // ATT_TILE, ATT_TILE
    n_maps = 2 * G
    assert D % (G * PAIR) == 0 and nk >= 2 and QK_AHEAD_PAIR <= QK_AHEAD <= n_maps
    in_specs = [pl.BlockSpec((1, G * PAIR, t), lambda b, p, q: (b, p, q)),
                pl.BlockSpec((1, S, G * PAIR), lambda b, p, q: (b, 0, p)),
                pl.BlockSpec((1, G, nk, vT.shape[3], t), lambda b, p, q: (b, p, 0, 0, 0)),
                pl.BlockSpec((n_maps, 2, t, t), lambda b, p, q: (p, 0, 0, 0), pipeline_mode=pl.Buffered(1))]
    scratch = [pltpu.VMEM((n_maps, PAIR, t), BF16),
               pltpu.VMEM((n_maps, 1, t), F32),
               pltpu.VMEM((max(QK_AHEAD, QK_AHEAD_PAIR), 2, t, t), F32)]
    if kind == "diff":
        body = functools.partial(_diff_kernel, lam_init=lam_init)
        in_specs += [pl.BlockSpec(e.shape, lambda b, p, q: (0, 0)) for e in extra]
        scratch += [pltpu.VMEM((n_maps, PAIR + ONES_ROWS, t), F32)]
    else:
        body = functools.partial(_moba_kernel, nk=nk)
        scratch += [pltpu.VMEM((n_maps, HEAD_DIM + ONES_ROWS, t), F32),
                    pltpu.VMEM((nk, G * PAIR), F32),
                    pltpu.VMEM((n_maps, nk, t), F32)]
    return pl.pallas_call(
        body,
        grid=(B, D // (G * PAIR), nk),
        in_specs=in_specs,
        out_specs=pl.BlockSpec((1, t, G * PAIR), lambda b, p, q: (b, q, p)),
        out_shape=jax.ShapeDtypeStruct((B, S, D), BF16),
        scratch_shapes=scratch,
        compiler_params=pltpu.CompilerParams(dimension_semantics=("parallel", "parallel", "arbitrary"),
                                             vmem_limit_bytes=VMEM_LIMIT_BYTES),
        name=kind + "_attention",
    )(qT, k, vT, bias, *extra)


def kernel(x, rel_bias, norm_g, final_norm_g, ffn_w_in, ffn_w_out, diff_w_qkv, diff_lambda,
           diff_subln_g, diff_w_o, moba_w_qkv, moba_w_o):
    B, S, D = x.shape
    depth = norm_g.shape[0]
    assert S % TOKEN_TILE == 0 and D % PAIR == 0 and MOBA_BLOCK == ATT_TILE
    assert rel_bias.shape == (REL_BUCKETS, D // HEAD_DIM)
    bias = _bias_tiles(rel_bias, ATT_TILE)

    h = x.reshape(B * S, D)
    for i in range(depth):
        g = norm_g[i]
        h = _ffn(h, g[0], ffn_w_in[i, 0], ffn_w_out[i, 0])
        j = i // 2
        if i % 2 == 0:
            qT, k, vT = _qkv_proj(h.reshape(B, S, D), g[1], diff_w_qkv[j], PAIR)
            lam_init = 0.8 - 0.6 * math.exp(-0.3 * i)
            g_sub = jnp.broadcast_to((diff_subln_g[j].astype(F32) * (1.0 - lam_init))[:, None], (PAIR, ATT_TILE))
            o = _attention("diff", qT, k, vT, bias, (diff_lambda[j].astype(F32), g_sub), lam_init=lam_init)
            w_o = diff_w_o[j]
        else:
            qT, k, vT = _qkv_proj(h.reshape(B, S, D), g[1], moba_w_qkv[j], HEAD_DIM)
            o = _attention("moba", qT, k, vT, bias, ())
            w_o = moba_w_o[j]
        last = i == depth - 1
        h = _ffn(h, g[2], ffn_w_in[i, 1], ffn_w_out[i, 1], proj=(o.reshape(B * S, D), w_o),
                 final_g=final_norm_g if last else None)
    return h.reshape(B, S, D)
```

```python
import functools
import math

import numpy as np
import jax
import jax.numpy as jnp
from jax import lax
from jax.experimental import pallas as pl
from jax.experimental.pallas import tpu as pltpu

HEAD_DIM = 64
PAIR = 2 * HEAD_DIM
MOBA_BLOCK = 256
MOBA_TOPK = 3
KMEAN_TERMS = 3
ONES_ROWS = 16
REL_BUCKETS = 32
REL_MAX_DIST = 128
FFN_RESIDUAL = 0.5
RMS_EPS = 1e-6
SUBLN_EPS = 1e-5

ATT_TILE = 256
PAIRS_PER_STEP = 8
QK_AHEAD = 4
QK_AHEAD_PAIR = 2
FFN_CHUNK = 256
TOKEN_TILE = 512
FFN_TOKEN_TILE = 1024
MASK_VALUE = -1e30
LOG2E = math.log2(math.e)
VMEM_LIMIT_BYTES = 56 * 1024 * 1024

F32 = jnp.float32
BF16 = jnp.bfloat16
_NT = (((1,), (1,)), ((), ()))


def _rms(x, g, eps):
    return x * lax.rsqrt(jnp.mean(x * x, axis=-1, keepdims=True) + eps) * g


def _const_spec(shape):
    return pl.BlockSpec(shape, lambda *_: (0,) * len(shape), pipeline_mode=pl.Buffered(1))


def _rel_bucket_np(dist):
    n = np.maximum(dist, 0)
    max_exact = REL_BUCKETS // 2
    nf = np.maximum(n, 1).astype(np.float32)
    large = max_exact + (np.log(nf / np.float32(max_exact)) / np.float32(math.log(REL_MAX_DIST / max_exact))
                         * np.float32(REL_BUCKETS - max_exact)).astype(np.int32)
    large = np.minimum(large, REL_BUCKETS - 1)
    return np.where(n < max_exact, n, large).astype(np.int32)


def _bucket_tiles(t):
    j = np.arange(t)[:, None]
    i = np.arange(t)[None, :]
    diag = np.where(i - j >= 0, _rel_bucket_np(i - j), -1)
    prev = _rel_bucket_np(i - j + t)
    assert _rel_bucket_np(np.arange(t + 1, 8 * t)).min() == REL_BUCKETS - 1
    return np.stack([diag, prev]).astype(np.int32)


def _bias_kernel(rb_ref, idx_ref, out_ref):
    m = pl.program_id(0)
    far = rb_ref[REL_BUCKETS - 1, m]
    for t in range(2):
        idx = idx_ref[t]
        acc = jnp.zeros(idx.shape, F32)
        for b in range(REL_BUCKETS - 1):
            acc = jnp.where(idx == b, (rb_ref[b, m] - far) * LOG2E, acc)
        out_ref[0, t] = jnp.where(idx < 0, MASK_VALUE, acc)


def _bias_tiles(rel_bias, t):
    n_maps = rel_bias.shape[1]
    idx = jnp.asarray(_bucket_tiles(t))
    return pl.pallas_call(
        _bias_kernel,
        grid=(n_maps,),
        in_specs=[pl.BlockSpec(memory_space=pltpu.SMEM),
                  pl.BlockSpec((2, t, t), lambda m: (0, 0, 0))],
        out_specs=pl.BlockSpec((1, 2, t, t), lambda m: (m, 0, 0, 0)),
        out_shape=jax.ShapeDtypeStruct((n_maps, 2, t, t), F32),
        name="rel_bias_tiles",
    )(rel_bias.astype(F32), idx)


def _ffn_kernel(*refs, n_chunks, has_proj, has_final):
    refs = list(refs)
    x_ref = refs.pop(0)
    if has_proj:
        o_ref, wo_ref = refs.pop(0), refs.pop(0)
    g_ref, win_ref, wout_ref = refs.pop(0), refs.pop(0), refs.pop(0)
    if has_final:
        gf_ref = refs.pop(0)
    (out_ref,) = refs

    x = x_ref[...]
    if has_proj:
        x = x + jnp.dot(o_ref[...], wo_ref[...], preferred_element_type=F32)
    hn = _rms(x, g_ref[...], RMS_EPS).astype(BF16)
    acc = None
    d_ff = n_chunks * FFN_CHUNK
    for c in range(n_chunks):
        cols = slice(c * FFN_CHUNK, (c + 1) * FFN_CHUNK)
        gate = jnp.dot(hn, win_ref[:, cols], preferred_element_type=F32)
        up = jnp.dot(hn, win_ref[:, d_ff + c * FFN_CHUNK:d_ff + (c + 1) * FFN_CHUNK], preferred_element_type=F32)
        a = (gate * (1.0 / (1.0 + jnp.exp(-gate))) * up).astype(BF16)
        part = jnp.dot(a, wout_ref[cols, :], preferred_element_type=F32)
        acc = part if acc is None else acc + part
    y = x + FFN_RESIDUAL * acc
    if has_final:
        y = _rms(y, gf_ref[...], RMS_EPS)
    out_ref[...] = y


def _ffn(x2d, g, w_in, w_out, proj=None, final_g=None):
    T, D = x2d.shape
    F = w_out.shape[0]
    n_chunks = F // FFN_CHUNK
    assert n_chunks * FFN_CHUNK == F
    tm = math.gcd(T, FFN_TOKEN_TILE)
    w_in_c = w_in.astype(BF16)
    w_out_c = w_out.astype(BF16)

    row = lambda i: (i, 0)
    args, specs = [x2d], [pl.BlockSpec((tm, D), row)]
    if proj is not None:
        o2d, w_o = proj
        args += [o2d, w_o.astype(BF16)]
        specs += [pl.BlockSpec((tm, o2d.shape[1]), row), _const_spec(w_o.shape)]
    args += [g.reshape(1, D).astype(F32), w_in_c, w_out_c]
    specs += [_const_spec((1, D)), _const_spec(w_in_c.shape), _const_spec(w_out_c.shape)]
    if final_g is not None:
        args.append(final_g.reshape(1, D).astype(F32))
        specs.append(_const_spec((1, D)))

    return pl.pallas_call(
        functools.partial(_ffn_kernel, n_chunks=n_chunks, has_proj=proj is not None,
                          has_final=final_g is not None),
        grid=(T // tm,),
        in_specs=specs,
        out_specs=pl.BlockSpec((tm, D), row),
        out_shape=jax.ShapeDtypeStruct((T, D), F32),
        compiler_params=pltpu.CompilerParams(dimension_semantics=("parallel",),
                                             vmem_limit_bytes=VMEM_LIMIT_BYTES),
        name="ffn",
    )(*args)


def _proj_kernel(x_ref, g_ref, wqT_ref, wk_ref, wvT_ref, qT_ref, k_ref, vT_ref, *, n_pairs, n_sub, v_dim):
    hn = _rms(x_ref[0], g_ref[...], RMS_EPS).astype(BF16)
    k_ref[0] = jnp.dot(hn, wk_ref[...], preferred_element_type=F32).astype(BF16)
    qT_ref[0] = lax.dot_general(wqT_ref[...], hn, _NT, preferred_element_type=F32).astype(BF16)
    vT = lax.dot_general(wvT_ref[...], hn, _NT, preferred_element_type=F32).astype(BF16)
    ones_rows = jnp.where(lax.broadcasted_iota(jnp.int32, (ONES_ROWS, ATT_TILE), 0) == 0, 1.0, 0.0).astype(BF16)
    group = v_dim + ONES_ROWS
    for p in range(n_pairs):
        for c in range(n_sub):
            cols = slice(c * ATT_TILE, (c + 1) * ATT_TILE)
            for i in range(PAIR // v_dim):
                vT_ref[0, p, c, i * group:i * group + v_dim, :] = vT[p * PAIR + i * v_dim:p * PAIR + (i + 1) * v_dim, cols]
                vT_ref[0, p, c, i * group + v_dim:(i + 1) * group, :] = ones_rows


def _v_rows(v_dim):
    return (PAIR // v_dim) * (v_dim + ONES_ROWS)


def _qkv_proj(h, g, w_qkv, v_dim):
    B, S, D = h.shape
    tm = TOKEN_TILE
    n_pairs, n_sub, nk, rows = D // PAIR, tm // ATT_TILE, S // ATT_TILE, _v_rows(v_dim)
    wq, wk, wv = w_qkv[:, :D], w_qkv[:, D:2 * D], w_qkv[:, 2 * D:]
    wqT = (wq * (HEAD_DIM ** -0.5 * LOG2E)).T.astype(BF16)
    wvT = wv.T.astype(BF16)
    return pl.pallas_call(
        functools.partial(_proj_kernel, n_pairs=n_pairs, n_sub=n_sub, v_dim=v_dim),
        grid=(B, S // tm),
        in_specs=[pl.BlockSpec((1, tm, D), lambda b, s: (b, s, 0)),
                  _const_spec((1, D)), _const_spec((D, D)), _const_spec((D, D)), _const_spec((D, D))],
        out_specs=[pl.BlockSpec((1, D, tm), lambda b, s: (b, 0, s)),
                   pl.BlockSpec((1, tm, D), lambda b, s: (b, s, 0)),
                   pl.BlockSpec((1, n_pairs, n_sub, rows, ATT_TILE), lambda b, s: (b, 0, s, 0, 0))],
        out_shape=[jax.ShapeDtypeStruct((B, D, S), BF16),
                   jax.ShapeDtypeStruct((B, S, D), BF16),
                   jax.ShapeDtypeStruct((B, n_pairs, nk, rows, ATT_TILE), BF16)],
        compiler_params=pltpu.CompilerParams(dimension_semantics=("parallel", "parallel"),
                                             vmem_limit_bytes=VMEM_LIMIT_BYTES),
        name="qkv_proj",
    )(h, g.reshape(1, D).astype(F32), wqT, wk.astype(BF16), wvT)


def _split_pairs(qT_ref, qz_ref):
    zeros = jnp.zeros((HEAD_DIM, qT_ref.shape[2]), qT_ref.dtype)
    for g in range(PAIRS_PER_STEP):
        q = qT_ref[0, g * PAIR:(g + 1) * PAIR, :]
        qz_ref[2 * g, :HEAD_DIM, :] = q[:HEAD_DIM]
        qz_ref[2 * g, HEAD_DIM:, :] = zeros
        qz_ref[2 * g + 1, :HEAD_DIM, :] = zeros
        qz_ref[2 * g + 1, HEAD_DIM:, :] = q[HEAD_DIM:]


def _update(m, scores, vT_tiles, biases, m_ref, acc_ref, first):
    scores = [s if b is None else b + s for s, (b, _) in zip(scores, biases)]
    masks = [r for _, r in biases]
    if all(r is None for r in masks):
        m_cur = jnp.max(functools.reduce(jnp.maximum, scores), axis=0, keepdims=True)
    else:
        m_cur = functools.reduce(jnp.maximum, [jnp.max(s, axis=0, keepdims=True) + (0.0 if r is None else r)
                                               for s, r in zip(scores, masks)])
    if first:
        m_new = m_cur
    else:
        m_old = m_ref[m]
        m_new = jnp.maximum(m_old, m_cur)
    shifts = [m_new if r is None else m_new - r for r in masks]
    pv = functools.reduce(jnp.add, [jnp.dot(v, jnp.exp2(s - sh).astype(BF16), preferred_element_type=F32)
                                    for v, s, sh in zip(vT_tiles, scores, shifts)])
    if first:
        acc_ref[m] = pv
    else:
        acc_ref[m] = jnp.exp2(m_old - m_new) * acc_ref[m] + pv
    m_ref[m] = m_new


def _sweep(qi, k_ref, vT_ref, v_rows, near_bias, far_bias, state, pre_ref):
    qz_ref, m_ref, acc_ref = state
    n_maps = 2 * PAIRS_PER_STEP

    def raw_scores(j, m):
        rows = pl.ds(pl.multiple_of(j * ATT_TILE, ATT_TILE), ATT_TILE)
        k_t = k_ref[0, rows, (m // 2) * PAIR:(m // 2 + 1) * PAIR]
        return jnp.dot(k_t, qz_ref[m], preferred_element_type=F32)

    def block(tiles, biases, first, ahead, preloaded, next_tiles, next_ahead):
        scores = {}
        for m in range(ahead):
            scores[m] = ([pre_ref[m, t] for t in range(len(tiles))] if preloaded
                         else [raw_scores(j, m) for j in tiles])
        for m in range(n_maps):
            nxt = m + ahead
            if nxt < n_maps:
                scores[nxt] = [raw_scores(j, nxt) for j in tiles]
            elif nxt - n_maps < next_ahead:
                for t, j in enumerate(next_tiles):
                    pre_ref[nxt - n_maps, t] = raw_scores(j, nxt - n_maps)
            _update(m, scores.pop(m), [v_rows(vT_ref[0, m // 2, j], m) for j in tiles], biases(m),
                    m_ref, acc_ref, first)

    n_far = jnp.maximum(qi - 1, 0)
    n_pairs = n_far // 2

    @pl.when(qi == 0)
    def _():
        block([qi], lambda m: [near_bias(m, 0, qi)], True, QK_AHEAD, False, [], 0)

    @pl.when(qi >= 1)
    def _():
        block([qi - 1, qi], lambda m: [near_bias(m, 1, qi - 1), near_bias(m, 0, qi)], True, QK_AHEAD_PAIR, False,
              [0, 1], QK_AHEAD_PAIR)

    @pl.when(n_far % 2 == 1)
    def _():
        block([n_far - 1], lambda m: [far_bias(m, n_far - 1)], False, QK_AHEAD, False, [], 0)

    def far_pair(p, carry):
        j, jn = 2 * p, 2 * jnp.minimum(p + 1, n_pairs - 1)
        block([j, j + 1], lambda m: [far_bias(m, j), far_bias(m, j + 1)], False, QK_AHEAD_PAIR, True,
              [jn, jn + 1], QK_AHEAD_PAIR)
        return carry

    lax.fori_loop(0, n_pairs, far_pair, 0)


def _diff_kernel(qT_ref, k_ref, vT_ref, bias_ref, lam_ref, g_ref, o_ref,
                 qz_ref, m_ref, pre_ref, acc_ref, *, lam_init):
    qi = pl.program_id(2)
    _split_pairs(qT_ref, qz_ref)
    _sweep(qi, k_ref, vT_ref, lambda v, m: v,
           lambda m, t, j: (bias_ref[m, t], None), lambda m, j: (None, None),
           (qz_ref, m_ref, acc_ref), pre_ref)

    lp = lam_ref[...]
    lam = (jnp.exp(jnp.sum(lp[0:1] * lp[1:2], axis=-1, keepdims=True))
           - jnp.exp(jnp.sum(lp[2:3] * lp[3:4], axis=-1, keepdims=True)) + lam_init)

    def weighted(m, scale):
        return acc_ref[m, :PAIR, :] * (scale / acc_ref[m, PAIR:PAIR + 1, :])

    for g in range(PAIRS_PER_STEP):
        o = weighted(2 * g, 1.0) - weighted(2 * g + 1, lam)
        o = o * lax.rsqrt(jnp.mean(o * o, axis=0, keepdims=True) + SUBLN_EPS) * g_ref[...]
        o_ref[0, :, g * PAIR:(g + 1) * PAIR] = o.T.astype(BF16)


def _moba_kernel(qT_ref, k_ref, vT_ref, bias_ref, o_ref,
                 qz_ref, m_ref, pre_ref, acc_ref, kmean_ref, selb_ref, *, nk):
    qi = pl.program_id(2)
    tq = qT_ref.shape[2]

    @pl.when(qi == 0)
    def _():
        for j in range(nk):
            kb = k_ref[0, j * MOBA_BLOCK:(j + 1) * MOBA_BLOCK, :].astype(F32)
            kmean_ref[j:j + 1, :] = jnp.mean(kb, axis=0, keepdims=True)

    _split_pairs(qT_ref, qz_ref)
    blk = lax.broadcasted_iota(jnp.int32, (nk, tq), 0).astype(F32)
    eligible = blk < qi.astype(F32)
    for m in range(2 * PAIRS_PER_STEP):
        rest = kmean_ref[:, (m // 2) * PAIR:(m // 2 + 1) * PAIR]
        gate = jnp.zeros((nk, tq), F32)
        for _ in range(KMEAN_TERMS):
            term = rest.astype(BF16)
            gate = gate + jnp.dot(term, qz_ref[m], preferred_element_type=F32)
            rest = rest - term.astype(F32)
        gate = jnp.where(eligible, gate, -jnp.inf)
        picked = jnp.zeros((nk, tq), F32)
        for _ in range(MOBA_TOPK):
            best = jnp.max(gate, axis=0, keepdims=True)
            pick = blk == jnp.min(jnp.where(gate == best, blk, float(nk)), axis=0, keepdims=True)
            picked = jnp.where(pick, 1.0, picked)
            gate = jnp.where(pick, -jnp.inf, gate)
        selb_ref[m] = jnp.where(eligible, jnp.where(picked > 0.0, 0.0, MASK_VALUE), MASK_VALUE)

    head_rows = HEAD_DIM + ONES_ROWS
    _sweep(qi, k_ref, vT_ref, lambda v, m: v[(m % 2) * head_rows:(m % 2 + 1) * head_rows],
           lambda m, t, j: (bias_ref[m, t], None if t == 0 else selb_ref[m, pl.ds(j, 1), :]),
           lambda m, j: (None, selb_ref[m, pl.ds(j, 1), :]),
           (qz_ref, m_ref, acc_ref), pre_ref)

    for g in range(PAIRS_PER_STEP):
        o = jnp.concatenate([acc_ref[m, :HEAD_DIM, :] * (1.0 / acc_ref[m, HEAD_DIM:HEAD_DIM + 1, :])
                             for m in (2 * g, 2 * g + 1)], axis=0)
        o_ref[0, :, g * PAIR:(g + 1) * PAIR] = o.T.astype(BF16)


def _attention(kind, qT, k, vT, bias, extra, *, lam_init=None):
    B, D, S = qT.shape
    G, nk, t = PAIRS_PER_STEP, S // ATT_TILE, ATT_TILE
    n_maps = 2 * G
    assert D % (G * PAIR) == 0 and nk >= 2 and QK_AHEAD_PAIR <= QK_AHEAD <= n_maps
    in_specs = [pl.BlockSpec((1, G * PAIR, t), lambda b, p, q: (b, p, q)),
                pl.BlockSpec((1, S, G * PAIR), lambda b, p, q: (b, 0, p)),
                pl.BlockSpec((1, G, nk, vT.shape[3], t), lambda b, p, q: (b, p, 0, 0, 0)),
                pl.BlockSpec((n_maps, 2, t, t), lambda b, p, q: (p, 0, 0, 0), pipeline_mode=pl.Buffered(1))]
    scratch = [pltpu.VMEM((n_maps, PAIR, t), BF16),
               pltpu.VMEM((n_maps, 1, t), F32),
               pltpu.VMEM((max(QK_AHEAD, QK_AHEAD_PAIR), 2, t, t), F32)]
    if kind == "diff":
        body = functools.partial(_diff_kernel, lam_init=lam_init)
        in_specs += [pl.BlockSpec(e.shape, lambda b, p, q: (0, 0)) for e in extra]
        scratch += [pltpu.VMEM((n_maps, PAIR + ONES_ROWS, t), F32)]
    else:
        body = functools.partial(_moba_kernel, nk=nk)
        scratch += [pltpu.VMEM((n_maps, HEAD_DIM + ONES_ROWS, t), F32),
                    pltpu.VMEM((nk, G * PAIR), F32),
                    pltpu.VMEM((n_maps, nk, t), F32)]
    return pl.pallas_call(
        body,
        grid=(B, D // (G * PAIR), nk),
        in_specs=in_specs,
        out_specs=pl.BlockSpec((1, t, G * PAIR), lambda b, p, q: (b, q, p)),
        out_shape=jax.ShapeDtypeStruct((B, S, D), BF16),
        scratch_shapes=scratch,
        compiler_params=pltpu.CompilerParams(dimension_semantics=("parallel", "parallel", "arbitrary"),
                                             vmem_limit_bytes=VMEM_LIMIT_BYTES),
        name=kind + "_attention",
    )(qT, k, vT, bias, *extra)


def kernel(x, rel_bias, norm_g, final_norm_g, ffn_w_in, ffn_w_out, diff_w_qkv, diff_lambda,
           diff_subln_g, diff_w_o, moba_w_qkv, moba_w_o):
    B, S, D = x.shape
    depth = norm_g.shape[0]
    assert S % TOKEN_TILE == 0 and D % PAIR == 0 and MOBA_BLOCK == ATT_TILE
    assert rel_bias.shape == (REL_BUCKETS, D // HEAD_DIM)
    bias = _bias_tiles(rel_bias, ATT_TILE)

    h = x.reshape(B * S, D)
    for i in range(depth):
        g = norm_g[i]
        h = _ffn(h, g[0], ffn_w_in[i, 0], ffn_w_out[i, 0])
        j = i // 2
        if i % 2 == 0:
            qT, k, vT = _qkv_proj(h.reshape(B, S, D), g[1], diff_w_qkv[j], PAIR)
            lam_init = 0.8 - 0.6 * math.exp(-0.3 * i)
            g_sub = jnp.broadcast_to((diff_subln_g[j].astype(F32) * (1.0 - lam_init))[:, None], (PAIR, ATT_TILE))
            o = _attention("diff", qT, k, vT, bias, (diff_lambda[j].astype(F32), g_sub), lam_init=lam_init)
            w_o = diff_w_o[j]
        else:
            qT, k, vT = _qkv_proj(h.reshape(B, S, D), g[1], moba_w_qkv[j], HEAD_DIM)
            o = _attention("moba", qT, k, vT, bias, ())
            w_o = moba_w_o[j]
        last = i == depth - 1
        h = _ffn(h, g[2], ffn_w_in[i, 1], ffn_w_out[i, 1], proj=(o.reshape(B * S, D), w_o),
                 final_g=final_norm_g if last else None)
    return h.reshape(B, S, D)
```

```python
import functools
import math

import numpy as np
import jax
import jax.numpy as jnp
from jax import lax
from jax.experimental import pallas as pl
from jax.experimental.pallas import tpu as pltpu

HEAD_DIM = 64
PAIR = 2 * HEAD_DIM
MOBA_BLOCK = 256
MOBA_TOPK = 3
KMEAN_TERMS = 3
ONES_ROWS = 16
REL_BUCKETS = 32
REL_MAX_DIST = 128
FFN_RESIDUAL = 0.5
RMS_EPS = 1e-6
SUBLN_EPS = 1e-5

ATT_TILE = 256
PAIRS_PER_STEP = 8
QK_AHEAD = 4
QK_AHEAD_PAIR = 2
FFN_CHUNK = 256
TOKEN_TILE = 512
FFN_TOKEN_TILE = 1024
MASK_VALUE = -1e30
LOG2E = math.log2(math.e)
VMEM_LIMIT_BYTES = 56 * 1024 * 1024

F32 = jnp.float32
BF16 = jnp.bfloat16
_NT = (((1,), (1,)), ((), ()))


def _rms(x, g, eps):
    return x * lax.rsqrt(jnp.mean(x * x, axis=-1, keepdims=True) + eps) * g


def _const_spec(shape):
    return pl.BlockSpec(shape, lambda *_: (0,) * len(shape), pipeline_mode=pl.Buffered(1))


def _rel_bucket_np(dist):
    n = np.maximum(dist, 0)
    max_exact = REL_BUCKETS // 2
    nf = np.maximum(n, 1).astype(np.float32)
    large = max_exact + (np.log(nf / np.float32(max_exact)) / np.float32(math.log(REL_MAX_DIST / max_exact))
                         * np.float32(REL_BUCKETS - max_exact)).astype(np.int32)
    large = np.minimum(large, REL_BUCKETS - 1)
    return np.where(n < max_exact, n, large).astype(np.int32)


def _bucket_tiles(t):
    j = np.arange(t)[:, None]
    i = np.arange(t)[None, :]
    diag = np.where(i - j >= 0, _rel_bucket_np(i - j), -1)
    prev = _rel_bucket_np(i - j + t)
    assert _rel_bucket_np(np.arange(t + 1, 8 * t)).min() == REL_BUCKETS - 1
    return np.stack([diag, prev]).astype(np.int32)


def _bias_kernel(rb_ref, idx_ref, out_ref):
    m = pl.program_id(0)
    far = rb_ref[REL_BUCKETS - 1, m]
    for t in range(2):
        idx = idx_ref[t]
        acc = jnp.zeros(idx.shape, F32)
        for b in range(REL_BUCKETS - 1):
            acc = jnp.where(idx == b, (rb_ref[b, m] - far) * LOG2E, acc)
        out_ref[0, t] = jnp.where(idx < 0, MASK_VALUE, acc)


def _bias_tiles(rel_bias, t):
    n_maps = rel_bias.shape[1]
    idx = jnp.asarray(_bucket_tiles(t))
    return pl.pallas_call(
        _bias_kernel,
        grid=(n_maps,),
        in_specs=[pl.BlockSpec(memory_space=pltpu.SMEM),
                  pl.BlockSpec((2, t, t), lambda m: (0, 0, 0))],
        out_specs=pl.BlockSpec((1, 2, t, t), lambda m: (m, 0, 0, 0)),
        out_shape=jax.ShapeDtypeStruct((n_maps, 2, t, t), F32),
        name="rel_bias_tiles",
    )(rel_bias.astype(F32), idx)


def _ffn_kernel(*refs, n_chunks, has_proj, has_final):
    refs = list(refs)
    x_ref = refs.pop(0)
    if has_proj:
        o_ref, wo_ref = refs.pop(0), refs.pop(0)
    g_ref, win_ref, wout_ref = refs.pop(0), refs.pop(0), refs.pop(0)
    if has_final:
        gf_ref = refs.pop(0)
    (out_ref,) = refs

    x = x_ref[...]
    if has_proj:
        x = x + jnp.dot(o_ref[...], wo_ref[...], preferred_element_type=F32)
    hn = _rms(x, g_ref[...], RMS_EPS).astype(BF16)
    acc = None
    d_ff = n_chunks * FFN_CHUNK
    for c in range(n_chunks):
        cols = slice(c * FFN_CHUNK, (c + 1) * FFN_CHUNK)
        gate = jnp.dot(hn, win_ref[:, cols], preferred_element_type=F32)
        up = jnp.dot(hn, win_ref[:, d_ff + c * FFN_CHUNK:d_ff + (c + 1) * FFN_CHUNK], preferred_element_type=F32)
        a = (gate * (1.0 / (1.0 + jnp.exp(-gate))) * up).astype(BF16)
        part = jnp.dot(a, wout_ref[cols, :], preferred_element_type=F32)
        acc = part if acc is None else acc + part
    y = x + FFN_RESIDUAL * acc
    if has_final:
        y = _rms(y, gf_ref[...], RMS_EPS)
    out_ref[...] = y


def _ffn(x2d, g, w_in, w_out, proj=None, final_g=None):
    T, D = x2d.shape
    F = w_out.shape[0]
    n_chunks = F // FFN_CHUNK
    assert n_chunks * FFN_CHUNK == F
    tm = math.gcd(T, FFN_TOKEN_TILE)
    w_in_c = w_in.astype(BF16)
    w_out_c = w_out.astype(BF16)

    row = lambda i: (i, 0)
    args, specs = [x2d], [pl.BlockSpec((tm, D), row)]
    if proj is not None:
        o2d, w_o = proj
        args += [o2d, w_o.astype(BF16)]
        specs += [pl.BlockSpec((tm, o2d.shape[1]), row), _const_spec(w_o.shape)]
    args += [g.reshape(1, D).astype(F32), w_in_c, w_out_c]
    specs += [_const_spec((1, D)), _const_spec(w_in_c.shape), _const_spec(w_out_c.shape)]
    if final_g is not None:
        args.append(final_g.reshape(1, D).astype(F32))
        specs.append(_const_spec((1, D)))

    return pl.pallas_call(
        functools.partial(_ffn_kernel, n_chunks=n_chunks, has_proj=proj is not None,
                          has_final=final_g is not None),
        grid=(T // tm,),
        in_specs=specs,
        out_specs=pl.BlockSpec((tm, D), row),
        out_shape=jax.ShapeDtypeStruct((T, D), F32),
        compiler_params=pltpu.CompilerParams(dimension_semantics=("parallel",),
                                             vmem_limit_bytes=VMEM_LIMIT_BYTES),
        name="ffn",
    )(*args)


def _proj_kernel(x_ref, g_ref, wqT_ref, wk_ref, wvT_ref, qT_ref, k_ref, vT_ref, *, n_pairs, n_sub, v_dim):
    hn = _rms(x_ref[0], g_ref[...], RMS_EPS).astype(BF16)
    k_ref[0] = jnp.dot(hn, wk_ref[...], preferred_element_type=F32).astype(BF16)
    qT_ref[0] = lax.dot_general(wqT_ref[...], hn, _NT, preferred_element_type=F32).astype(BF16)
    vT = lax.dot_general(wvT_ref[...], hn, _NT, preferred_element_type=F32).astype(BF16)
    ones_rows = jnp.where(lax.broadcasted_iota(jnp.int32, (ONES_ROWS, ATT_TILE), 0) == 0, 1.0, 0.0).astype(BF16)
    group = v_dim + ONES_ROWS
    for p in range(n_pairs):
        for c in range(n_sub):
            cols = slice(c * ATT_TILE, (c + 1) * ATT_TILE)
            for i in range(PAIR // v_dim):
                vT_ref[0, p, c, i * group:i * group + v_dim, :] = vT[p * PAIR + i * v_dim:p * PAIR + (i + 1) * v_dim, cols]
                vT_ref[0, p, c, i * group + v_dim:(i + 1) * group, :] = ones_rows


def _v_rows(v_dim):
    return (PAIR // v_dim) * (v_dim + ONES_ROWS)


def _qkv_proj(h, g, w_qkv, v_dim):
    B, S, D = h.shape
    tm = TOKEN_TILE
    n_pairs, n_sub, nk, rows = D // PAIR, tm // ATT_TILE, S // ATT_TILE, _v_rows(v_dim)
    wq, wk, wv = w_qkv[:, :D], w_qkv[:, D:2 * D], w_qkv[:, 2 * D:]
    wqT = (wq * (HEAD_DIM ** -0.5 * LOG2E)).T.astype(BF16)
    wvT = wv.T.astype(BF16)
    return pl.pallas_call(
        functools.partial(_proj_kernel, n_pairs=n_pairs, n_sub=n_sub, v_dim=v_dim),
        grid=(B, S // tm),
        in_specs=[pl.BlockSpec((1, tm, D), lambda b, s: (b, s, 0)),
                  _const_spec((1, D)), _const_spec((D, D)), _const_spec((D, D)), _const_spec((D, D))],
        out_specs=[pl.BlockSpec((1, D, tm), lambda b, s: (b, 0, s)),
                   pl.BlockSpec((1, tm, D), lambda b, s: (b, s, 0)),
                   pl.BlockSpec((1, n_pairs, n_sub, rows, ATT_TILE), lambda b, s: (b, 0, s, 0, 0))],
        out_shape=[jax.ShapeDtypeStruct((B, D, S), BF16),
                   jax.ShapeDtypeStruct((B, S, D), BF16),
                   jax.ShapeDtypeStruct((B, n_pairs, nk, rows, ATT_TILE), BF16)],
        compiler_params=pltpu.CompilerParams(dimension_semantics=("parallel", "parallel"),
                                             vmem_limit_bytes=VMEM_LIMIT_BYTES),
        name="qkv_proj",
    )(h, g.reshape(1, D).astype(F32), wqT, wk.astype(BF16), wvT)


def _split_pairs(qT_ref, qz_ref):
    zeros = jnp.zeros((HEAD_DIM, qT_ref.shape[2]), qT_ref.dtype)
    for g in range(PAIRS_PER_STEP):
        q = qT_ref[0, g * PAIR:(g + 1) * PAIR, :]
        qz_ref[2 * g, :HEAD_DIM, :] = q[:HEAD_DIM]
        qz_ref[2 * g, HEAD_DIM:, :] = zeros
        qz_ref[2 * g + 1, :HEAD_DIM, :] = zeros
        qz_ref[2 * g + 1, HEAD_DIM:, :] = q[HEAD_DIM:]


def _update(m, scores, vT_tiles, biases, m_ref, acc_ref, first):
    scores = [s if b is None else b + s for s, (b, _) in zip(scores, biases)]
    masks = [r for _, r in biases]
    if all(r is None for r in masks):
        m_cur = jnp.max(functools.reduce(jnp.maximum, scores), axis=0, keepdims=True)
    else:
        m_cur = functools.reduce(jnp.maximum, [jnp.max(s, axis=0, keepdims=True) + (0.0 if r is None else r)
                                               for s, r in zip(scores, masks)])
    if first:
        m_new = m_cur
    else:
        m_old = m_ref[m]
        m_new = jnp.maximum(m_old, m_cur)
    shifts = [m_new if r is None else m_new - r for r in masks]
    pv = functools.reduce(jnp.add, [jnp.dot(v, jnp.exp2(s - sh).astype(BF16), preferred_element_type=F32)
                                    for v, s, sh in zip(vT_tiles, scores, shifts)])
    if first:
        acc_ref[m] = pv
    else:
        acc_ref[m] = jnp.exp2(m_old - m_new) * acc_ref[m] + pv
    m_ref[m] = m_new


def _sweep(qi, k_ref, vT_ref, v_rows, near_bias, far_bias, state, pre_ref, before_past=None):
    qz_ref, m_ref, acc_ref = state
    n_maps = 2 * PAIRS_PER_STEP

    def raw_scores(j, m):
        rows = pl.ds(pl.multiple_of(j * ATT_TILE, ATT_TILE), ATT_TILE)
        k_t = k_ref[0, rows, (m // 2) * PAIR:(m // 2 + 1) * PAIR]
        return jnp.dot(k_t, qz_ref[m], preferred_element_type=F32)

    def block(tiles, biases, first, ahead, preloaded, next_tiles, next_ahead):
        scores = {}
        for m in range(ahead):
            scores[m] = ([pre_ref[m, t] for t in range(len(tiles))] if preloaded
                         else [raw_scores(j, m) for j in tiles])
        for m in range(n_maps):
            nxt = m + ahead
            if nxt < n_maps:
                scores[nxt] = [raw_scores(j, nxt) for j in tiles]
            elif nxt - n_maps < next_ahead:
                for t, j in enumerate(next_tiles):
                    pre_ref[nxt - n_maps, t] = raw_scores(j, nxt - n_maps)
            _update(m, scores.pop(m), [v_rows(vT_ref[0, m // 2, j], m) for j in tiles], biases(m),
                    m_ref, acc_ref, first)

    n_far = jnp.maximum(qi - 1, 0)
    n_pairs = n_far // 2

    @pl.when(qi == 0)
    def _():
        block([qi], lambda m: [near_bias(m, 0, qi)], True, QK_AHEAD, False, [], 0)

    @pl.when(qi >= 1)
    def _():
        if before_past is not None:
            before_past()
        block([qi - 1, qi], lambda m: [near_bias(m, 1, qi - 1), near_bias(m, 0, qi)], True, QK_AHEAD_PAIR, False,
              [0, 1], QK_AHEAD_PAIR)

    @pl.when(n_far % 2 == 1)
    def _():
        block([n_far - 1], lambda m: [far_bias(m, n_far - 1)], False, QK_AHEAD, False, [], 0)

    def far_pair(p, carry):
        j, jn = 2 * p, 2 * jnp.minimum(p + 1, n_pairs - 1)
        block([j, j + 1], lambda m: [far_bias(m, j), far_bias(m, j + 1)], False, QK_AHEAD_PAIR, True,
              [jn, jn + 1], QK_AHEAD_PAIR)
        return carry

    lax.fori_loop(0, n_pairs, far_pair, 0)


def _diff_kernel(qT_ref, k_ref, vT_ref, bias_ref, lam_ref, g_ref, o_ref,
                 qz_ref, m_ref, pre_ref, acc_ref, *, lam_init):
    qi = pl.program_id(2)
    _split_pairs(qT_ref, qz_ref)
    _sweep(qi, k_ref, vT_ref, lambda v, m: v,
           lambda m, t, j: (bias_ref[m, t], None), lambda m, j: (None, None),
           (qz_ref, m_ref, acc_ref), pre_ref)

    lp = lam_ref[...]
    lam = (jnp.exp(jnp.sum(lp[0:1] * lp[1:2], axis=-1, keepdims=True))
           - jnp.exp(jnp.sum(lp[2:3] * lp[3:4], axis=-1, keepdims=True)) + lam_init)

    def weighted(m, scale):
        return acc_ref[m, :PAIR, :] * (scale / acc_ref[m, PAIR:PAIR + 1, :])

    for g in range(PAIRS_PER_STEP):
        o = weighted(2 * g, 1.0) - weighted(2 * g + 1, lam)
        o = o * lax.rsqrt(jnp.mean(o * o, axis=0, keepdims=True) + SUBLN_EPS) * g_ref[...]
        o_ref[0, :, g * PAIR:(g + 1) * PAIR] = o.T.astype(BF16)


def _moba_kernel(qT_ref, k_ref, vT_ref, bias_ref, o_ref,
                 qz_ref, m_ref, pre_ref, acc_ref, kmean_ref, selb_ref, *, nk):
    qi = pl.program_id(2)
    tq = qT_ref.shape[2]

    @pl.when(qi == 0)
    def _():
        for j in range(nk):
            kb = k_ref[0, j * MOBA_BLOCK:(j + 1) * MOBA_BLOCK, :].astype(F32)
            kmean_ref[j:j + 1, :] = jnp.mean(kb, axis=0, keepdims=True)

    _split_pairs(qT_ref, qz_ref)

    def select_blocks():
        blk = lax.broadcasted_iota(jnp.int32, (nk, tq), 0).astype(F32)
        eligible = blk < qi.astype(F32)
        for m in range(2 * PAIRS_PER_STEP):
            rest = kmean_ref[:, (m // 2) * PAIR:(m // 2 + 1) * PAIR]
            gate = jnp.zeros((nk, tq), F32)
            for _ in range(KMEAN_TERMS):
                term = rest.astype(BF16)
                gate = gate + jnp.dot(term, qz_ref[m], preferred_element_type=F32)
                rest = rest - term.astype(F32)
            gate = jnp.where(eligible, gate, -jnp.inf)
            picked = jnp.zeros((nk, tq), F32)
            for _ in range(MOBA_TOPK):
                best = jnp.max(gate, axis=0, keepdims=True)
                pick = blk == jnp.min(jnp.where(gate == best, blk, float(nk)), axis=0, keepdims=True)
                picked = jnp.where(pick, 1.0, picked)
                gate = jnp.where(pick, -jnp.inf, gate)
            selb_ref[m] = jnp.where(eligible, jnp.where(picked > 0.0, 0.0, MASK_VALUE), MASK_VALUE)

    head_rows = HEAD_DIM + ONES_ROWS
    _sweep(qi, k_ref, vT_ref, lambda v, m: v[(m % 2) * head_rows:(m % 2 + 1) * head_rows],
           lambda m, t, j: (bias_ref[m, t], None if t == 0 else selb_ref[m, pl.ds(j, 1), :]),
           lambda m, j: (None, selb_ref[m, pl.ds(j, 1), :]),
           (qz_ref, m_ref, acc_ref), pre_ref, before_past=select_blocks)

    for g in range(PAIRS_PER_STEP):
        o = jnp.concatenate([acc_ref[m, :HEAD_DIM, :] * (1.0 / acc_ref[m, HEAD_DIM:HEAD_DIM + 1, :])
                             for m in (2 * g, 2 * g + 1)], axis=0)
        o_ref[0, :, g * PAIR:(g + 1) * PAIR] = o.T.astype(BF16)


def _attention(kind, qT, k, vT, bias, extra, *, lam_init=None):
    B, D, S = qT.shape
    G, nk, t = PAIRS_PER_STEP, S // ATT_TILE, ATT_TILE
    n_maps = 2 * G
    assert D % (G * PAIR) == 0 and nk >= 2 and QK_AHEAD_PAIR <= QK_AHEAD <= n_maps
    in_specs = [pl.BlockSpec((1, G * PAIR, t), lambda b, p, q: (b, p, q)),
                pl.BlockSpec((1, S, G * PAIR), lambda b, p, q: (b, 0, p)),
                pl.BlockSpec((1, G, nk, vT.shape[3], t), lambda b, p, q: (b, p, 0, 0, 0)),
                pl.BlockSpec((n_maps, 2, t, t), lambda b, p, q: (p, 0, 0, 0), pipeline_mode=pl.Buffered(1))]
    scratch = [pltpu.VMEM((n_maps, PAIR, t), BF16),
               pltpu.VMEM((n_maps, 1, t), F32),
               pltpu.VMEM((max(QK_AHEAD, QK_AHEAD_PAIR), 2, t, t), F32)]
    if kind == "diff":
        body = functools.partial(_diff_kernel, lam_init=lam_init)
        in_specs += [pl.BlockSpec(e.shape, lambda b, p, q: (0, 0)) for e in extra]
        scratch += [pltpu.VMEM((n_maps, PAIR + ONES_ROWS, t), F32)]
    else:
        body = functools.partial(_moba_kernel, nk=nk)
        scratch += [pltpu.VMEM((n_maps, HEAD_DIM + ONES_ROWS, t), F32),
                    pltpu.VMEM((nk, G * PAIR), F32),
                    pltpu.VMEM((n_maps, nk, t), F32)]
    return pl.pallas_call(
        body,
        grid=(B, D // (G * PAIR), nk),
        in_specs=in_specs,
        out_specs=pl.BlockSpec((1, t, G * PAIR), lambda b, p, q: (b, q, p)),
        out_shape=jax.ShapeDtypeStruct((B, S, D), BF16),
        scratch_shapes=scratch,
        compiler_params=pltpu.CompilerParams(dimension_semantics=("parallel", "parallel", "arbitrary"),
                                             vmem_limit_bytes=VMEM_LIMIT_BYTES),
        name=kind + "_attention",
    )(qT, k, vT, bias, *extra)


def kernel(x, rel_bias, norm_g, final_norm_g, ffn_w_in, ffn_w_out, diff_w_qkv, diff_lambda,
           diff_subln_g, diff_w_o, moba_w_qkv, moba_w_o):
    B, S, D = x.shape
    depth = norm_g.shape[0]
    assert S % TOKEN_TILE == 0 and D % PAIR == 0 and MOBA_BLOCK == ATT_TILE
    assert rel_bias.shape == (REL_BUCKETS, D // HEAD_DIM)
    bias = _bias_tiles(rel_bias, ATT_TILE)

    h = x.reshape(B * S, D)
    for i in range(depth):
        g = norm_g[i]
        h = _ffn(h, g[0], ffn_w_in[i, 0], ffn_w_out[i, 0])
        j = i // 2
        if i % 2 == 0:
            qT, k, vT = _qkv_proj(h.reshape(B, S, D), g[1], diff_w_qkv[j], PAIR)
            lam_init = 0.8 - 0.6 * math.exp(-0.3 * i)
            g_sub = jnp.broadcast_to((diff_subln_g[j].astype(F32) * (1.0 - lam_init))[:, None], (PAIR, ATT_TILE))
            o = _attention("diff", qT, k, vT, bias, (diff_lambda[j].astype(F32), g_sub), lam_init=lam_init)
            w_o = diff_w_o[j]
        else:
            qT, k, vT = _qkv_proj(h.reshape(B, S, D), g[1], moba_w_qkv[j], HEAD_DIM)
            o = _attention("moba", qT, k, vT, bias, ())
            w_o = moba_w_o[j]
        last = i == depth - 1
        h = _ffn(h, g[2], ffn_w_in[i, 1], ffn_w_out[i, 1], proj=(o.reshape(B * S, D), w_o),
                 final_g=final_norm_g if last else None)
    return h.reshape(B, S, D)
```

```python
import functools
import math

import numpy as np
import jax
import jax.numpy as jnp
from jax import lax
from jax.experimental import pallas as pl
from jax.experimental.pallas import tpu as pltpu

HEAD_DIM = 64
PAIR = 2 * HEAD_DIM
MOBA_BLOCK = 256
MOBA_TOPK = 3
KMEAN_TERMS = 3
ONES_ROWS = 16
REL_BUCKETS = 32
REL_MAX_DIST = 128
FFN_RESIDUAL = 0.5
RMS_EPS = 1e-6
SUBLN_EPS = 1e-5

ATT_TILE = 256
PAIRS_PER_STEP = 4
QK_AHEAD = 4
QK_AHEAD_PAIR = 2
FFN_CHUNK = 256
TOKEN_TILE = 512
FFN_TOKEN_TILE = 1024
MASK_VALUE = -1e30
LOG2E = math.log2(math.e)
VMEM_LIMIT_BYTES = 56 * 1024 * 1024

F32 = jnp.float32
BF16 = jnp.bfloat16
_NT = (((1,), (1,)), ((), ()))


def _rms(x, g, eps):
    return x * lax.rsqrt(jnp.mean(x * x, axis=-1, keepdims=True) + eps) * g


def _const_spec(shape):
    return pl.BlockSpec(shape, lambda *_: (0,) * len(shape), pipeline_mode=pl.Buffered(1))


def _rel_bucket_np(dist):
    n = np.maximum(dist, 0)
    max_exact = REL_BUCKETS // 2
    nf = np.maximum(n, 1).astype(np.float32)
    large = max_exact + (np.log(nf / np.float32(max_exact)) / np.float32(math.log(REL_MAX_DIST / max_exact))
                         * np.float32(REL_BUCKETS - max_exact)).astype(np.int32)
    large = np.minimum(large, REL_BUCKETS - 1)
    return np.where(n < max_exact, n, large).astype(np.int32)


def _bucket_tiles(t):
    j = np.arange(t)[:, None]
    i = np.arange(t)[None, :]
    diag = np.where(i - j >= 0, _rel_bucket_np(i - j), -1)
    prev = _rel_bucket_np(i - j + t)
    assert _rel_bucket_np(np.arange(t + 1, 8 * t)).min() == REL_BUCKETS - 1
    return np.stack([diag, prev]).astype(np.int32)


def _bias_kernel(rb_ref, idx_ref, out_ref):
    m = pl.program_id(0)
    far = rb_ref[REL_BUCKETS - 1, m]
    for t in range(2):
        idx = idx_ref[t]
        acc = jnp.zeros(idx.shape, F32)
        for b in range(REL_BUCKETS - 1):
            acc = jnp.where(idx == b, (rb_ref[b, m] - far) * LOG2E, acc)
        out_ref[0, t] = jnp.where(idx < 0, MASK_VALUE, acc)


def _bias_tiles(rel_bias, t):
    n_maps = rel_bias.shape[1]
    idx = jnp.asarray(_bucket_tiles(t))
    return pl.pallas_call(
        _bias_kernel,
        grid=(n_maps,),
        in_specs=[pl.BlockSpec(memory_space=pltpu.SMEM),
                  pl.BlockSpec((2, t, t), lambda m: (0, 0, 0))],
        out_specs=pl.BlockSpec((1, 2, t, t), lambda m: (m, 0, 0, 0)),
        out_shape=jax.ShapeDtypeStruct((n_maps, 2, t, t), F32),
        name="rel_bias_tiles",
    )(rel_bias.astype(F32), idx)


def _ffn_kernel(*refs, n_chunks, has_proj, has_final):
    refs = list(refs)
    x_ref = refs.pop(0)
    if has_proj:
        o_ref, wo_ref = refs.pop(0), refs.pop(0)
    g_ref, win_ref, wout_ref = refs.pop(0), refs.pop(0), refs.pop(0)
    if has_final:
        gf_ref = refs.pop(0)
    (out_ref,) = refs

    x = x_ref[...]
    if has_proj:
        x = x + jnp.dot(o_ref[...], wo_ref[...], preferred_element_type=F32)
    hn = _rms(x, g_ref[...], RMS_EPS).astype(BF16)
    acc = None
    d_ff = n_chunks * FFN_CHUNK
    for c in range(n_chunks):
        cols = slice(c * FFN_CHUNK, (c + 1) * FFN_CHUNK)
        gate = jnp.dot(hn, win_ref[:, cols], preferred_element_type=F32)
        up = jnp.dot(hn, win_ref[:, d_ff + c * FFN_CHUNK:d_ff + (c + 1) * FFN_CHUNK], preferred_element_type=F32)
        a = (gate * (1.0 / (1.0 + jnp.exp(-gate))) * up).astype(BF16)
        part = jnp.dot(a, wout_ref[cols, :], preferred_element_type=F32)
        acc = part if acc is None else acc + part
    y = x + FFN_RESIDUAL * acc
    if has_final:
        y = _rms(y, gf_ref[...], RMS_EPS)
    out_ref[...] = y


def _ffn(x2d, g, w_in, w_out, proj=None, final_g=None):
    T, D = x2d.shape
    F = w_out.shape[0]
    n_chunks = F // FFN_CHUNK
    assert n_chunks * FFN_CHUNK == F
    tm = math.gcd(T, FFN_TOKEN_TILE)
    w_in_c = w_in.astype(BF16)
    w_out_c = w_out.astype(BF16)

    row = lambda i: (i, 0)
    args, specs = [x2d], [pl.BlockSpec((tm, D), row)]
    if proj is not None:
        o2d, w_o = proj
        args += [o2d, w_o.astype(BF16)]
        specs += [pl.BlockSpec((tm, o2d.shape[1]), row), _const_spec(w_o.shape)]
    args += [g.reshape(1, D).astype(F32), w_in_c, w_out_c]
    specs += [_const_spec((1, D)), _const_spec(w_in_c.shape), _const_spec(w_out_c.shape)]
    if final_g is not None:
        args.append(final_g.reshape(1, D).astype(F32))
        specs.append(_const_spec((1, D)))

    return pl.pallas_call(
        functools.partial(_ffn_kernel, n_chunks=n_chunks, has_proj=proj is not None,
                          has_final=final_g is not None),
        grid=(T // tm,),
        in_specs=specs,
        out_specs=pl.BlockSpec((tm, D), row),
        out_shape=jax.ShapeDtypeStruct((T, D), F32),
        compiler_params=pltpu.CompilerParams(dimension_semantics=("parallel",),
                                             vmem_limit_bytes=VMEM_LIMIT_BYTES),
        name="ffn",
    )(*args)


def _proj_kernel(x_ref, g_ref, wqT_ref, wk_ref, wvT_ref, qT_ref, k_ref, vT_ref, *, n_pairs, n_sub, v_dim):
    hn = _rms(x_ref[0], g_ref[...], RMS_EPS).astype(BF16)
    k_ref[0] = jnp.dot(hn, wk_ref[...], preferred_element_type=F32).astype(BF16)
    qT_ref[0] = lax.dot_general(wqT_ref[...], hn, _NT, preferred_element_type=F32).astype(BF16)
    vT = lax.dot_general(wvT_ref[...], hn, _NT, preferred_element_type=F32).astype(BF16)
    ones_rows = jnp.where(lax.broadcasted_iota(jnp.int32, (ONES_ROWS, ATT_TILE), 0) == 0, 1.0, 0.0).astype(BF16)
    group = v_dim + ONES_ROWS
    for p in range(n_pairs):
        for c in range(n_sub):
            cols = slice(c * ATT_TILE, (c + 1) * ATT_TILE)
            for i in range(PAIR // v_dim):
                vT_ref[0, p, c, i * group:i * group + v_dim, :] = vT[p * PAIR + i * v_dim:p * PAIR + (i + 1) * v_dim, cols]
                vT_ref[0, p, c, i * group + v_dim:(i + 1) * group, :] = ones_rows


def _v_rows(v_dim):
    return (PAIR // v_dim) * (v_dim + ONES_ROWS)


def _qkv_proj(h, g, w_qkv, v_dim):
    B, S, D = h.shape
    tm = TOKEN_TILE
    n_pairs, n_sub, nk, rows = D // PAIR, tm // ATT_TILE, S // ATT_TILE, _v_rows(v_dim)
    wq, wk, wv = w_qkv[:, :D], w_qkv[:, D:2 * D], w_qkv[:, 2 * D:]
    wqT = (wq * (HEAD_DIM ** -0.5 * LOG2E)).T.astype(BF16)
    wvT = wv.T.astype(BF16)
    return pl.pallas_call(
        functools.partial(_proj_kernel, n_pairs=n_pairs, n_sub=n_sub, v_dim=v_dim),
        grid=(B, S // tm),
        in_specs=[pl.BlockSpec((1, tm, D), lambda b, s: (b, s, 0)),
                  _const_spec((1, D)), _const_spec((D, D)), _const_spec((D, D)), _const_spec((D, D))],
        out_specs=[pl.BlockSpec((1, D, tm), lambda b, s: (b, 0, s)),
                   pl.BlockSpec((1, tm, D), lambda b, s: (b, s, 0)),
                   pl.BlockSpec((1, n_pairs, n_sub, rows, ATT_TILE), lambda b, s: (b, 0, s, 0, 0))],
        out_shape=[jax.ShapeDtypeStruct((B, D, S), BF16),
                   jax.ShapeDtypeStruct((B, S, D), BF16),
                   jax.ShapeDtypeStruct((B, n_pairs, nk, rows, ATT_TILE), BF16)],
        compiler_params=pltpu.CompilerParams(dimension_semantics=("parallel", "parallel"),
                                             vmem_limit_bytes=VMEM_LIMIT_BYTES),
        name="qkv_proj",
    )(h, g.reshape(1, D).astype(F32), wqT, wk.astype(BF16), wvT)


def _split_pairs(qT_ref, qz_ref):
    zeros = jnp.zeros((HEAD_DIM, qT_ref.shape[2]), qT_ref.dtype)
    for g in range(PAIRS_PER_STEP):
        q = qT_ref[0, g * PAIR:(g + 1) * PAIR, :]
        qz_ref[2 * g, :HEAD_DIM, :] = q[:HEAD_DIM]
        qz_ref[2 * g, HEAD_DIM:, :] = zeros
        qz_ref[2 * g + 1, :HEAD_DIM, :] = zeros
        qz_ref[2 * g + 1, HEAD_DIM:, :] = q[HEAD_DIM:]


def _update(m, scores, vT_tiles, biases, m_ref, acc_ref, first):
    scores = [s if b is None else b + s for s, (b, _) in zip(scores, biases)]
    masks = [r for _, r in biases]
    if all(r is None for r in masks):
        m_cur = jnp.max(functools.reduce(jnp.maximum, scores), axis=0, keepdims=True)
    else:
        m_cur = functools.reduce(jnp.maximum, [jnp.max(s, axis=0, keepdims=True) + (0.0 if r is None else r)
                                               for s, r in zip(scores, masks)])
    if first:
        m_new = m_cur
    else:
        m_old = m_ref[m]
        m_new = jnp.maximum(m_old, m_cur)
    shifts = [m_new if r is None else m_new - r for r in masks]
    pv = functools.reduce(jnp.add, [jnp.dot(v, jnp.exp2(s - sh).astype(BF16), preferred_element_type=F32)
                                    for v, s, sh in zip(vT_tiles, scores, shifts)])
    if first:
        acc_ref[m] = pv
    else:
        acc_ref[m] = jnp.exp2(m_old - m_new) * acc_ref[m] + pv
    m_ref[m] = m_new


def _sweep(qi, k_ref, vT_ref, v_rows, near_bias, far_bias, state, pre_ref, before_past=None):
    qz_ref, m_ref, acc_ref = state
    n_maps = 2 * PAIRS_PER_STEP

    def raw_scores(j, m):
        rows = pl.ds(pl.multiple_of(j * ATT_TILE, ATT_TILE), ATT_TILE)
        k_t = k_ref[0, rows, (m // 2) * PAIR:(m // 2 + 1) * PAIR]
        return jnp.dot(k_t, qz_ref[m], preferred_element_type=F32)

    def block(tiles, biases, first, ahead, preloaded, next_tiles, next_ahead):
        scores = {}
        for m in range(ahead):
            scores[m] = ([pre_ref[m, t] for t in range(len(tiles))] if preloaded
                         else [raw_scores(j, m) for j in tiles])
        for m in range(n_maps):
            nxt = m + ahead
            if nxt < n_maps:
                scores[nxt] = [raw_scores(j, nxt) for j in tiles]
            elif nxt - n_maps < next_ahead:
                for t, j in enumerate(next_tiles):
                    pre_ref[nxt - n_maps, t] = raw_scores(j, nxt - n_maps)
            _update(m, scores.pop(m), [v_rows(vT_ref[0, m // 2, j], m) for j in tiles], biases(m),
                    m_ref, acc_ref, first)

    n_far = jnp.maximum(qi - 1, 0)
    n_pairs = n_far // 2

    @pl.when(qi == 0)
    def _():
        block([qi], lambda m: [near_bias(m, 0, qi)], True, QK_AHEAD, False, [], 0)

    @pl.when(qi >= 1)
    def _():
        if before_past is not None:
            before_past()
        block([qi - 1, qi], lambda m: [near_bias(m, 1, qi - 1), near_bias(m, 0, qi)], True, QK_AHEAD_PAIR, False,
              [0, 1], QK_AHEAD_PAIR)

    @pl.when(n_far % 2 == 1)
    def _():
        block([n_far - 1], lambda m: [far_bias(m, n_far - 1)], False, QK_AHEAD, False, [], 0)

    def far_pair(p, carry):
        j, jn = 2 * p, 2 * jnp.minimum(p + 1, n_pairs - 1)
        block([j, j + 1], lambda m: [far_bias(m, j), far_bias(m, j + 1)], False, QK_AHEAD_PAIR, True,
              [jn, jn + 1], QK_AHEAD_PAIR)
        return carry

    lax.fori_loop(0, n_pairs, far_pair, 0)


def _diff_kernel(qT_ref, k_ref, vT_ref, bias_ref, lam_ref, g_ref, o_ref,
                 qz_ref, m_ref, pre_ref, acc_ref, *, lam_init):
    qi = pl.program_id(2)
    _split_pairs(qT_ref, qz_ref)
    _sweep(qi, k_ref, vT_ref, lambda v, m: v,
           lambda m, t, j: (bias_ref[m, t], None), lambda m, j: (None, None),
           (qz_ref, m_ref, acc_ref), pre_ref)

    lp = lam_ref[...]
    lam = (jnp.exp(jnp.sum(lp[0:1] * lp[1:2], axis=-1, keepdims=True))
           - jnp.exp(jnp.sum(lp[2:3] * lp[3:4], axis=-1, keepdims=True)) + lam_init)

    def weighted(m, scale):
        return acc_ref[m, :PAIR, :] * (scale / acc_ref[m, PAIR:PAIR + 1, :])

    for g in range(PAIRS_PER_STEP):
        o = weighted(2 * g, 1.0) - weighted(2 * g + 1, lam)
        o = o * lax.rsqrt(jnp.mean(o * o, axis=0, keepdims=True) + SUBLN_EPS) * g_ref[...]
        o_ref[0, :, g * PAIR:(g + 1) * PAIR] = o.T.astype(BF16)


def _moba_kernel(qT_ref, k_ref, vT_ref, bias_ref, o_ref,
                 qz_ref, m_ref, pre_ref, acc_ref, kmean_ref, selb_ref, *, nk):
    qi = pl.program_id(2)
    tq = qT_ref.shape[2]

    @pl.when(qi == 0)
    def _():
        for j in range(nk):
            kb = k_ref[0, j * MOBA_BLOCK:(j + 1) * MOBA_BLOCK, :].astype(F32)
            kmean_ref[j:j + 1, :] = jnp.mean(kb, axis=0, keepdims=True)

    _split_pairs(qT_ref, qz_ref)

    def select_blocks():
        blk = lax.broadcasted_iota(jnp.int32, (nk, tq), 0).astype(F32)
        eligible = blk < qi.astype(F32)
        for m in range(2 * PAIRS_PER_STEP):
            rest = kmean_ref[:, (m // 2) * PAIR:(m // 2 + 1) * PAIR]
            gate = jnp.zeros((nk, tq), F32)
            for _ in range(KMEAN_TERMS):
                term = rest.astype(BF16)
                gate = gate + jnp.dot(term, qz_ref[m], preferred_element_type=F32)
                rest = rest - term.astype(F32)
            gate = jnp.where(eligible, gate, -jnp.inf)
            picked = jnp.zeros((nk, tq), F32)
            for _ in range(MOBA_TOPK):
                best = jnp.max(gate, axis=0, keepdims=True)
                pick = blk == jnp.min(jnp.where(gate == best, blk, float(nk)), axis=0, keepdims=True)
                picked = jnp.where(pick, 1.0, picked)
                gate = jnp.where(pick, -jnp.inf, gate)
            selb_ref[m] = jnp.where(eligible, jnp.where(picked > 0.0, 0.0, MASK_VALUE), MASK_VALUE)

    head_rows = HEAD_DIM + ONES_ROWS
    _sweep(qi, k_ref, vT_ref, lambda v, m: v[(m % 2) * head_rows:(m % 2 + 1) * head_rows],
           lambda m, t, j: (bias_ref[m, t], None if t == 0 else selb_ref[m, pl.ds(j, 1), :]),
           lambda m, j: (None, selb_ref[m, pl.ds(j, 1), :]),
           (qz_ref, m_ref, acc_ref), pre_ref, before_past=select_blocks)

    for g in range(PAIRS_PER_STEP):
        o = jnp.concatenate([acc_ref[m, :HEAD_DIM, :] * (1.0 / acc_ref[m, HEAD_DIM:HEAD_DIM + 1, :])
                             for m in (2 * g, 2 * g + 1)], axis=0)
        o_ref[0, :, g * PAIR:(g + 1) * PAIR] = o.T.astype(BF16)


def _attention(kind, qT, k, vT, bias, extra, *, lam_init=None):
    B, D, S = qT.shape
    G, nk, t = PAIRS_PER_STEP, S // ATT_TILE, ATT_TILE
    n_maps = 2 * G
    assert D % (G * PAIR) == 0 and nk >= 2 and QK_AHEAD_PAIR <= QK_AHEAD <= n_maps
    in_specs = [pl.BlockSpec((1, G * PAIR, t), lambda b, p, q: (b, p, q)),
                pl.BlockSpec((1, S, G * PAIR), lambda b, p, q: (b, 0, p)),
                pl.BlockSpec((1, G, nk, vT.shape[3], t), lambda b, p, q: (b, p, 0, 0, 0)),
                pl.BlockSpec((n_maps, 2, t, t), lambda b, p, q: (p, 0, 0, 0), pipeline_mode=pl.Buffered(1))]
    scratch = [pltpu.VMEM((n_maps, PAIR, t), BF16),
               pltpu.VMEM((n_maps, 1, t), F32),
               pltpu.VMEM((max(QK_AHEAD, QK_AHEAD_PAIR), 2, t, t), F32)]
    if kind == "diff":
        body = functools.partial(_diff_kernel, lam_init=lam_init)
        in_specs += [pl.BlockSpec(e.shape, lambda b, p, q: (0, 0)) for e in extra]
        scratch += [pltpu.VMEM((n_maps, PAIR + ONES_ROWS, t), F32)]
    else:
        body = functools.partial(_moba_kernel, nk=nk)
        scratch += [pltpu.VMEM((n_maps, HEAD_DIM + ONES_ROWS, t), F32),
                    pltpu.VMEM((nk, G * PAIR), F32),
                    pltpu.VMEM((n_maps, nk, t), F32)]
    return pl.pallas_call(
        body,
        grid=(B, D // (G * PAIR), nk),
        in_specs=in_specs,
        out_specs=pl.BlockSpec((1, t, G * PAIR), lambda b, p, q: (b, q, p)),
        out_shape=jax.ShapeDtypeStruct((B, S, D), BF16),
        scratch_shapes=scratch,
        compiler_params=pltpu.CompilerParams(dimension_semantics=("parallel", "parallel", "arbitrary"),
                                             vmem_limit_bytes=VMEM_LIMIT_BYTES),
        name=kind + "_attention",
    )(qT, k, vT, bias, *extra)


def kernel(x, rel_bias, norm_g, final_norm_g, ffn_w_in, ffn_w_out, diff_w_qkv, diff_lambda,
           diff_subln_g, diff_w_o, moba_w_qkv, moba_w_o):
    B, S, D = x.shape
    depth = norm_g.shape[0]
    assert S % TOKEN_TILE == 0 and D % PAIR == 0 and MOBA_BLOCK == ATT_TILE
    assert rel_bias.shape == (REL_BUCKETS, D // HEAD_DIM)
    bias = _bias_tiles(rel_bias, ATT_TILE)

    h = x.reshape(B * S, D)
    for i in range(depth):
        g = norm_g[i]
        h = _ffn(h, g[0], ffn_w_in[i, 0], ffn_w_out[i, 0])
        j = i // 2
        if i % 2 == 0:
            qT, k, vT = _qkv_proj(h.reshape(B, S, D), g[1], diff_w_qkv[j], PAIR)
            lam_init = 0.8 - 0.6 * math.exp(-0.3 * i)
            g_sub = jnp.broadcast_to((diff_subln_g[j].astype(F32) * (1.0 - lam_init))[:, None], (PAIR, ATT_TILE))
            o = _attention("diff", qT, k, vT, bias, (diff_lambda[j].astype(F32), g_sub), lam_init=lam_init)
            w_o = diff_w_o[j]
        else:
            qT, k, vT = _qkv_proj(h.reshape(B, S, D), g[1], moba_w_qkv[j], HEAD_DIM)
            o = _attention("moba", qT, k, vT, bias, ())
            w_o = moba_w_o[j]
        last = i == depth - 1
        h = _ffn(h, g[2], ffn_w_in[i, 1], ffn_w_out[i, 1], proj=(o.reshape(B * S, D), w_o),
                 final_g=final_norm_g if last else None)
    return h.reshape(B, S, D)
```

```python
import functools
import math

import numpy as np
import jax
import jax.numpy as jnp
from jax import lax
from jax.experimental import pallas as pl
from jax.experimental.pallas import tpu as pltpu

HEAD_DIM = 64
PAIR = 2 * HEAD_DIM
MOBA_BLOCK = 256
MOBA_TOPK = 3
KMEAN_TERMS = 3
ONES_ROWS = 16
REL_BUCKETS = 32
REL_MAX_DIST = 128
FFN_RESIDUAL = 0.5
RMS_EPS = 1e-6
SUBLN_EPS = 1e-5

ATT_TILE = 256
PAIRS_PER_STEP = 8
Q_TILES_PER_STEP = 4
QK_AHEAD = 4
QK_AHEAD_PAIR = 2
FFN_CHUNK = 256
TOKEN_TILE = 512
FFN_TOKEN_TILE = 1024
MASK_VALUE = -1e30
LOG2E = math.log2(math.e)
VMEM_LIMIT_BYTES = 62 * 1024 * 1024

F32 = jnp.float32
BF16 = jnp.bfloat16
_NT = (((1,), (1,)), ((), ()))


def _rms(x, g, eps):
    return x * lax.rsqrt(jnp.mean(x * x, axis=-1, keepdims=True) + eps) * g


def _const_spec(shape):
    return pl.BlockSpec(shape, lambda *_: (0,) * len(shape), pipeline_mode=pl.Buffered(1))


def _rel_bucket_np(dist):
    n = np.maximum(dist, 0)
    max_exact = REL_BUCKETS // 2
    nf = np.maximum(n, 1).astype(np.float32)
    large = max_exact + (np.log(nf / np.float32(max_exact)) / np.float32(math.log(REL_MAX_DIST / max_exact))
                         * np.float32(REL_BUCKETS - max_exact)).astype(np.int32)
    large = np.minimum(large, REL_BUCKETS - 1)
    return np.where(n < max_exact, n, large).astype(np.int32)


def _bucket_tiles(t):
    j = np.arange(t)[:, None]
    i = np.arange(t)[None, :]
    diag = np.where(i - j >= 0, _rel_bucket_np(i - j), -1)
    prev = _rel_bucket_np(i - j + t)
    assert _rel_bucket_np(np.arange(t + 1, 8 * t)).min() == REL_BUCKETS - 1
    return np.stack([diag, prev]).astype(np.int32)


def _bias_kernel(rb_ref, idx_ref, out_ref):
    m = pl.program_id(0)
    far = rb_ref[REL_BUCKETS - 1, m]
    for t in range(2):
        idx = idx_ref[t]
        acc = jnp.zeros(idx.shape, F32)
        for b in range(REL_BUCKETS - 1):
            acc = jnp.where(idx == b, (rb_ref[b, m] - far) * LOG2E, acc)
        out_ref[0, t] = jnp.where(idx < 0, MASK_VALUE, acc)


def _bias_tiles(rel_bias, t):
    n_maps = rel_bias.shape[1]
    idx = jnp.asarray(_bucket_tiles(t))
    return pl.pallas_call(
        _bias_kernel,
        grid=(n_maps,),
        in_specs=[pl.BlockSpec(memory_space=pltpu.SMEM),
                  pl.BlockSpec((2, t, t), lambda m: (0, 0, 0))],
        out_specs=pl.BlockSpec((1, 2, t, t), lambda m: (m, 0, 0, 0)),
        out_shape=jax.ShapeDtypeStruct((n_maps, 2, t, t), F32),
        name="rel_bias_tiles",
    )(rel_bias.astype(F32), idx)


def _ffn_kernel(*refs, n_chunks, has_proj, has_final):
    refs = list(refs)
    x_ref = refs.pop(0)
    if has_proj:
        o_ref, wo_ref = refs.pop(0), refs.pop(0)
    g_ref, win_ref, wout_ref = refs.pop(0), refs.pop(0), refs.pop(0)
    if has_final:
        gf_ref = refs.pop(0)
    (out_ref,) = refs

    x = x_ref[...]
    if has_proj:
        x = x + jnp.dot(o_ref[...], wo_ref[...], preferred_element_type=F32)
    hn = _rms(x, g_ref[...], RMS_EPS).astype(BF16)
    acc = None
    d_ff = n_chunks * FFN_CHUNK
    for c in range(n_chunks):
        cols = slice(c * FFN_CHUNK, (c + 1) * FFN_CHUNK)
        gate = jnp.dot(hn, win_ref[:, cols], preferred_element_type=F32)
        up = jnp.dot(hn, win_ref[:, d_ff + c * FFN_CHUNK:d_ff + (c + 1) * FFN_CHUNK], preferred_element_type=F32)
        a = (gate * (1.0 / (1.0 + jnp.exp(-gate))) * up).astype(BF16)
        part = jnp.dot(a, wout_ref[cols, :], preferred_element_type=F32)
        acc = part if acc is None else acc + part
    y = x + FFN_RESIDUAL * acc
    if has_final:
        y = _rms(y, gf_ref[...], RMS_EPS)
    out_ref[...] = y


def _ffn(x2d, g, w_in, w_out, proj=None, final_g=None):
    T, D = x2d.shape
    F = w_out.shape[0]
    n_chunks = F // FFN_CHUNK
    assert n_chunks * FFN_CHUNK == F
    tm = math.gcd(T, FFN_TOKEN_TILE)
    w_in_c = w_in.astype(BF16)
    w_out_c = w_out.astype(BF16)

    row = lambda i: (i, 0)
    args, specs = [x2d], [pl.BlockSpec((tm, D), row)]
    if proj is not None:
        o2d, w_o = proj
        args += [o2d, w_o.astype(BF16)]
        specs += [pl.BlockSpec((tm, o2d.shape[1]), row), _const_spec(w_o.shape)]
    args += [g.reshape(1, D).astype(F32), w_in_c, w_out_c]
    specs += [_const_spec((1, D)), _const_spec(w_in_c.shape), _const_spec(w_out_c.shape)]
    if final_g is not None:
        args.append(final_g.reshape(1, D).astype(F32))
        specs.append(_const_spec((1, D)))

    return pl.pallas_call(
        functools.partial(_ffn_kernel, n_chunks=n_chunks, has_proj=proj is not None,
                          has_final=final_g is not None),
        grid=(T // tm,),
        in_specs=specs,
        out_specs=pl.BlockSpec((tm, D), row),
        out_shape=jax.ShapeDtypeStruct((T, D), F32),
        compiler_params=pltpu.CompilerParams(dimension_semantics=("parallel",),
                                             vmem_limit_bytes=VMEM_LIMIT_BYTES),
        name="ffn",
    )(*args)


def _proj_kernel(x_ref, g_ref, wqT_ref, wk_ref, wvT_ref, qT_ref, k_ref, vT_ref, *, n_pairs, n_sub, v_dim):
    hn = _rms(x_ref[0], g_ref[...], RMS_EPS).astype(BF16)
    k_ref[0] = jnp.dot(hn, wk_ref[...], preferred_element_type=F32).astype(BF16)
    qT = lax.dot_general(wqT_ref[...], hn, _NT, preferred_element_type=F32).astype(BF16)
    for c in range(n_sub):
        qT_ref[0, c] = qT[:, c * ATT_TILE:(c + 1) * ATT_TILE]
    vT = lax.dot_general(wvT_ref[...], hn, _NT, preferred_element_type=F32).astype(BF16)
    ones_rows = jnp.where(lax.broadcasted_iota(jnp.int32, (ONES_ROWS, ATT_TILE), 0) == 0, 1.0, 0.0).astype(BF16)
    group = v_dim + ONES_ROWS
    for p in range(n_pairs):
        for c in range(n_sub):
            cols = slice(c * ATT_TILE, (c + 1) * ATT_TILE)
            for i in range(PAIR // v_dim):
                vT_ref[0, p, c, i * group:i * group + v_dim, :] = vT[p * PAIR + i * v_dim:p * PAIR + (i + 1) * v_dim, cols]
                vT_ref[0, p, c, i * group + v_dim:(i + 1) * group, :] = ones_rows


def _v_rows(v_dim):
    return (PAIR // v_dim) * (v_dim + ONES_ROWS)


def _qkv_proj(h, g, w_qkv, v_dim):
    B, S, D = h.shape
    tm = TOKEN_TILE
    n_pairs, n_sub, nk, rows = D // PAIR, tm // ATT_TILE, S // ATT_TILE, _v_rows(v_dim)
    wq, wk, wv = w_qkv[:, :D], w_qkv[:, D:2 * D], w_qkv[:, 2 * D:]
    wqT = (wq * (HEAD_DIM ** -0.5 * LOG2E)).T.astype(BF16)
    wvT = wv.T.astype(BF16)
    return pl.pallas_call(
        functools.partial(_proj_kernel, n_pairs=n_pairs, n_sub=n_sub, v_dim=v_dim),
        grid=(B, S // tm),
        in_specs=[pl.BlockSpec((1, tm, D), lambda b, s: (b, s, 0)),
                  _const_spec((1, D)), _const_spec((D, D)), _const_spec((D, D)), _const_spec((D, D))],
        out_specs=[pl.BlockSpec((1, n_sub, D, ATT_TILE), lambda b, s: (b, s, 0, 0)),
                   pl.BlockSpec((1, tm, D), lambda b, s: (b, s, 0)),
                   pl.BlockSpec((1, n_pairs, n_sub, rows, ATT_TILE), lambda b, s: (b, 0, s, 0, 0))],
        out_shape=[jax.ShapeDtypeStruct((B, nk, D, ATT_TILE), BF16),
                   jax.ShapeDtypeStruct((B, S, D), BF16),
                   jax.ShapeDtypeStruct((B, n_pairs, nk, rows, ATT_TILE), BF16)],
        compiler_params=pltpu.CompilerParams(dimension_semantics=("parallel", "parallel"),
                                             vmem_limit_bytes=VMEM_LIMIT_BYTES),
        name="qkv_proj",
    )(h, g.reshape(1, D).astype(F32), wqT, wk.astype(BF16), wvT)


def _split_pairs(qT_ref, qs, qz_ref):
    zeros = jnp.zeros((HEAD_DIM, qT_ref.shape[3]), qT_ref.dtype)
    for g in range(PAIRS_PER_STEP):
        q = qT_ref[0, qs, g * PAIR:(g + 1) * PAIR, :]
        qz_ref[2 * g, :HEAD_DIM, :] = q[:HEAD_DIM]
        qz_ref[2 * g, HEAD_DIM:, :] = zeros
        qz_ref[2 * g + 1, :HEAD_DIM, :] = zeros
        qz_ref[2 * g + 1, HEAD_DIM:, :] = q[HEAD_DIM:]


def _update(m, scores, vT_tiles, biases, m_ref, acc_ref, first):
    scores = [s if b is None else b + s for s, (b, _) in zip(scores, biases)]
    masks = [r for _, r in biases]
    if all(r is None for r in masks):
        m_cur = jnp.max(functools.reduce(jnp.maximum, scores), axis=0, keepdims=True)
    else:
        m_cur = functools.reduce(jnp.maximum, [jnp.max(s, axis=0, keepdims=True) + (0.0 if r is None else r)
                                               for s, r in zip(scores, masks)])
    if first:
        m_new = m_cur
    else:
        m_old = m_ref[m]
        m_new = jnp.maximum(m_old, m_cur)
    shifts = [m_new if r is None else m_new - r for r in masks]
    pv = functools.reduce(jnp.add, [jnp.dot(v, jnp.exp2(s - sh).astype(BF16), preferred_element_type=F32)
                                    for v, s, sh in zip(vT_tiles, scores, shifts)])
    if first:
        acc_ref[m] = pv
    else:
        acc_ref[m] = jnp.exp2(m_old - m_new) * acc_ref[m] + pv
    m_ref[m] = m_new


def _sweep(qi, k_ref, vT_ref, v_rows, near_bias, far_bias, state, pre_ref, before_past=None):
    qz_ref, m_ref, acc_ref = state
    n_maps = 2 * PAIRS_PER_STEP

    def raw_scores(j, m):
        rows = pl.ds(pl.multiple_of(j * ATT_TILE, ATT_TILE), ATT_TILE)
        k_t = k_ref[0, rows, (m // 2) * PAIR:(m // 2 + 1) * PAIR]
        return jnp.dot(k_t, qz_ref[m], preferred_element_type=F32)

    def block(tiles, biases, first, ahead, preloaded, next_tiles, next_ahead):
        scores = {}
        for m in range(ahead):
            scores[m] = ([pre_ref[m, t] for t in range(len(tiles))] if preloaded
                         else [raw_scores(j, m) for j in tiles])
        for m in range(n_maps):
            nxt = m + ahead
            if nxt < n_maps:
                scores[nxt] = [raw_scores(j, nxt) for j in tiles]
            elif nxt - n_maps < next_ahead:
                for t, j in enumerate(next_tiles):
                    pre_ref[nxt - n_maps, t] = raw_scores(j, nxt - n_maps)
            _update(m, scores.pop(m), [v_rows(vT_ref[0, m // 2, j], m) for j in tiles], biases(m),
                    m_ref, acc_ref, first)

    n_far = jnp.maximum(qi - 1, 0)
    n_pairs = n_far // 2

    @pl.when(qi == 0)
    def _():
        block([qi], lambda m: [near_bias(m, 0, qi)], True, QK_AHEAD, False, [], 0)

    @pl.when(qi >= 1)
    def _():
        if before_past is not None:
            before_past()
        block([qi - 1, qi], lambda m: [near_bias(m, 1, qi - 1), near_bias(m, 0, qi)], True, QK_AHEAD_PAIR, False,
              [0, 1], QK_AHEAD_PAIR)

    @pl.when(n_far % 2 == 1)
    def _():
        block([n_far - 1], lambda m: [far_bias(m, n_far - 1)], False, QK_AHEAD, False, [], 0)

    def far_pair(p, carry):
        j, jn = 2 * p, 2 * jnp.minimum(p + 1, n_pairs - 1)
        block([j, j + 1], lambda m: [far_bias(m, j), far_bias(m, j + 1)], False, QK_AHEAD_PAIR, True,
              [jn, jn + 1], QK_AHEAD_PAIR)
        return carry

    lax.fori_loop(0, n_pairs, far_pair, 0)


def _for_each_query_tile(body):
    first = pl.program_id(2) * Q_TILES_PER_STEP

    def one(qs, carry):
        body(qs, first + qs, pl.ds(pl.multiple_of(qs * ATT_TILE, ATT_TILE), ATT_TILE))
        return carry

    if Q_TILES_PER_STEP == 1:
        one(0, 0)
    else:
        lax.fori_loop(0, Q_TILES_PER_STEP, one, 0)


def _diff_kernel(qT_ref, k_ref, vT_ref, bias_ref, lam_ref, g_ref, o_ref,
                 qz_ref, m_ref, pre_ref, acc_ref, *, lam_init):
    def tile(qs, qi, rows):
        _split_pairs(qT_ref, qs, qz_ref)
        _sweep(qi, k_ref, vT_ref, lambda v, m: v,
               lambda m, t, j: (bias_ref[m, t], None), lambda m, j: (None, None),
               (qz_ref, m_ref, acc_ref), pre_ref)

        lp = lam_ref[...]
        lam = (jnp.exp(jnp.sum(lp[0:1] * lp[1:2], axis=-1, keepdims=True))
               - jnp.exp(jnp.sum(lp[2:3] * lp[3:4], axis=-1, keepdims=True)) + lam_init)

        def weighted(m, scale):
            return acc_ref[m, :PAIR, :] * (scale / acc_ref[m, PAIR:PAIR + 1, :])

        for g in range(PAIRS_PER_STEP):
            o = weighted(2 * g, 1.0) - weighted(2 * g + 1, lam)
            o = o * lax.rsqrt(jnp.mean(o * o, axis=0, keepdims=True) + SUBLN_EPS) * g_ref[...]
            o_ref[0, rows, g * PAIR:(g + 1) * PAIR] = o.T.astype(BF16)

    _for_each_query_tile(tile)


def _moba_kernel(qT_ref, k_ref, vT_ref, bias_ref, o_ref,
                 qz_ref, m_ref, pre_ref, acc_ref, kmean_ref, selb_ref, *, nk):
    tq = qT_ref.shape[3]
    head_rows = HEAD_DIM + ONES_ROWS

    def tile(qs, qi, rows):
        @pl.when(qi == 0)
        def _():
            for j in range(nk):
                kb = k_ref[0, j * MOBA_BLOCK:(j + 1) * MOBA_BLOCK, :].astype(F32)
                kmean_ref[j:j + 1, :] = jnp.mean(kb, axis=0, keepdims=True)

        _split_pairs(qT_ref, qs, qz_ref)

        def select_blocks():
            blk = lax.broadcasted_iota(jnp.int32, (nk, tq), 0).astype(F32)
            eligible = blk < qi.astype(F32)
            for m in range(2 * PAIRS_PER_STEP):
                rest = kmean_ref[:, (m // 2) * PAIR:(m // 2 + 1) * PAIR]
                gate = jnp.zeros((nk, tq), F32)
                for _ in range(KMEAN_TERMS):
                    term = rest.astype(BF16)
                    gate = gate + jnp.dot(term, qz_ref[m], preferred_element_type=F32)
                    rest = rest - term.astype(F32)
                gate = jnp.where(eligible, gate, -jnp.inf)
                picked = jnp.zeros((nk, tq), F32)
                for _ in range(MOBA_TOPK):
                    best = jnp.max(gate, axis=0, keepdims=True)
                    pick = blk == jnp.min(jnp.where(gate == best, blk, float(nk)), axis=0, keepdims=True)
                    picked = jnp.where(pick, 1.0, picked)
                    gate = jnp.where(pick, -jnp.inf, gate)
                selb_ref[m] = jnp.where(eligible, jnp.where(picked > 0.0, 0.0, MASK_VALUE), MASK_VALUE)

        _sweep(qi, k_ref, vT_ref, lambda v, m: v[(m % 2) * head_rows:(m % 2 + 1) * head_rows],
               lambda m, t, j: (bias_ref[m, t], None if t == 0 else selb_ref[m, pl.ds(j, 1), :]),
               lambda m, j: (None, selb_ref[m, pl.ds(j, 1), :]),
               (qz_ref, m_ref, acc_ref), pre_ref, before_past=select_blocks)

        for g in range(PAIRS_PER_STEP):
            o = jnp.concatenate([acc_ref[m, :HEAD_DIM, :] * (1.0 / acc_ref[m, HEAD_DIM:HEAD_DIM + 1, :])
                                 for m in (2 * g, 2 * g + 1)], axis=0)
            o_ref[0, rows, g * PAIR:(g + 1) * PAIR] = o.T.astype(BF16)

    _for_each_query_tile(tile)


def _attention(kind, qT, k, vT, bias, extra, *, lam_init=None):
    B, nk, D, t = qT.shape
    G, S, T = PAIRS_PER_STEP, nk * t, Q_TILES_PER_STEP
    n_maps = 2 * G
    assert D % (G * PAIR) == 0 and nk >= 2 and nk % T == 0 and t == ATT_TILE and QK_AHEAD_PAIR <= QK_AHEAD <= n_maps
    in_specs = [pl.BlockSpec((1, T, G * PAIR, t), lambda b, p, q: (b, q, p, 0)),
                pl.BlockSpec((1, S, G * PAIR), lambda b, p, q: (b, 0, p)),
                pl.BlockSpec((1, G, nk, vT.shape[3], t), lambda b, p, q: (b, p, 0, 0, 0)),
                pl.BlockSpec((n_maps, 2, t, t), lambda b, p, q: (p, 0, 0, 0), pipeline_mode=pl.Buffered(1))]
    scratch = [pltpu.VMEM((n_maps, PAIR, t), BF16),
               pltpu.VMEM((n_maps, 1, t), F32),
               pltpu.VMEM((max(QK_AHEAD, QK_AHEAD_PAIR), 2, t, t), F32)]
    if kind == "diff":
        body = functools.partial(_diff_kernel, lam_init=lam_init)
        in_specs += [pl.BlockSpec(e.shape, lambda b, p, q: (0, 0)) for e in extra]
        scratch += [pltpu.VMEM((n_maps, PAIR + ONES_ROWS, t), F32)]
    else:
        body = functools.partial(_moba_kernel, nk=nk)
        scratch += [pltpu.VMEM((n_maps, HEAD_DIM + ONES_ROWS, t), F32),
                    pltpu.VMEM((nk, G * PAIR), F32),
                    pltpu.VMEM((n_maps, nk, t), F32)]
    return pl.pallas_call(
        body,
        grid=(B, D // (G * PAIR), nk // T),
        in_specs=in_specs,
        out_specs=pl.BlockSpec((1, T * t, G * PAIR), lambda b, p, q: (b, q, p)),
        out_shape=jax.ShapeDtypeStruct((B, S, D), BF16),
        scratch_shapes=scratch,
        compiler_params=pltpu.CompilerParams(dimension_semantics=("parallel", "parallel", "arbitrary"),
                                             vmem_limit_bytes=VMEM_LIMIT_BYTES),
        name=kind + "_attention",
    )(qT, k, vT, bias, *extra)


def kernel(x, rel_bias, norm_g, final_norm_g, ffn_w_in, ffn_w_out, diff_w_qkv, diff_lambda,
           diff_subln_g, diff_w_o, moba_w_qkv, moba_w_o):
    B, S, D = x.shape
    depth = norm_g.shape[0]
    assert S % TOKEN_TILE == 0 and D % PAIR == 0 and MOBA_BLOCK == ATT_TILE
    assert rel_bias.shape == (REL_BUCKETS, D // HEAD_DIM)
    bias = _bias_tiles(rel_bias, ATT_TILE)

    h = x.reshape(B * S, D)
    for i in range(depth):
        g = norm_g[i]
        h = _ffn(h, g[0], ffn_w_in[i, 0], ffn_w_out[i, 0])
        j = i // 2
        if i % 2 == 0:
            qT, k, vT = _qkv_proj(h.reshape(B, S, D), g[1], diff_w_qkv[j], PAIR)
            lam_init = 0.8 - 0.6 * math.exp(-0.3 * i)
            g_sub = jnp.broadcast_to((diff_subln_g[j].astype(F32) * (1.0 - lam_init))[:, None], (PAIR, ATT_TILE))
            o = _attention("diff", qT, k, vT, bias, (diff_lambda[j].astype(F32), g_sub), lam_init=lam_init)
            w_o = diff_w_o[j]
        else:
            qT, k, vT = _qkv_proj(h.reshape(B, S, D), g[1], moba_w_qkv[j], HEAD_DIM)
            o = _attention("moba", qT, k, vT, bias, ())
            w_o = moba_w_o[j]
        last = i == depth - 1
        h = _ffn(h, g[2], ffn_w_in[i, 1], ffn_w_out[i, 1], proj=(o.reshape(B * S, D), w_o),
                 final_g=final_norm_g if last else None)
    return h.reshape(B, S, D)
```

```python
import functools
import math

import numpy as np
import jax
import jax.numpy as jnp
from jax import lax
from jax.experimental import pallas as pl
from jax.experimental.pallas import tpu as pltpu

HEAD_DIM = 64
PAIR = 2 * HEAD_DIM
MOBA_BLOCK = 256
MOBA_TOPK = 3
KMEAN_TERMS = 3
ONES_ROWS = 16
REL_BUCKETS = 32
REL_MAX_DIST = 128
FFN_RESIDUAL = 0.5
RMS_EPS = 1e-6
SUBLN_EPS = 1e-5

ATT_TILE = 256
PAIRS_PER_STEP = 4
Q_TILES_PER_STEP = 4
QK_AHEAD = 4
QK_AHEAD_PAIR = 2
FFN_CHUNK = 256
TOKEN_TILE = 512
FFN_TOKEN_TILE = 1024
MASK_VALUE = -1e30
LOG2E = math.log2(math.e)
VMEM_LIMIT_BYTES = 62 * 1024 * 1024

F32 = jnp.float32
BF16 = jnp.bfloat16
_NT = (((1,), (1,)), ((), ()))


def _rms(x, g, eps):
    return x * lax.rsqrt(jnp.mean(x * x, axis=-1, keepdims=True) + eps) * g


def _const_spec(shape):
    return pl.BlockSpec(shape, lambda *_: (0,) * len(shape), pipeline_mode=pl.Buffered(1))


def _rel_bucket_np(dist):
    n = np.maximum(dist, 0)
    max_exact = REL_BUCKETS // 2
    nf = np.maximum(n, 1).astype(np.float32)
    large = max_exact + (np.log(nf / np.float32(max_exact)) / np.float32(math.log(REL_MAX_DIST / max_exact))
                         * np.float32(REL_BUCKETS - max_exact)).astype(np.int32)
    large = np.minimum(large, REL_BUCKETS - 1)
    return np.where(n < max_exact, n, large).astype(np.int32)


def _bucket_tiles(t):
    j = np.arange(t)[:, None]
    i = np.arange(t)[None, :]
    diag = np.where(i - j >= 0, _rel_bucket_np(i - j), -1)
    prev = _rel_bucket_np(i - j + t)
    assert _rel_bucket_np(np.arange(t + 1, 8 * t)).min() == REL_BUCKETS - 1
    return np.stack([diag, prev]).astype(np.int32)


def _bias_kernel(rb_ref, idx_ref, out_ref):
    m = pl.program_id(0)
    far = rb_ref[REL_BUCKETS - 1, m]
    for t in range(2):
        idx = idx_ref[t]
        acc = jnp.zeros(idx.shape, F32)
        for b in range(REL_BUCKETS - 1):
            acc = jnp.where(idx == b, (rb_ref[b, m] - far) * LOG2E, acc)
        out_ref[0, t] = jnp.where(idx < 0, MASK_VALUE, acc)


def _bias_tiles(rel_bias, t):
    n_maps = rel_bias.shape[1]
    idx = jnp.asarray(_bucket_tiles(t))
    return pl.pallas_call(
        _bias_kernel,
        grid=(n_maps,),
        in_specs=[pl.BlockSpec(memory_space=pltpu.SMEM),
                  pl.BlockSpec((2, t, t), lambda m: (0, 0, 0))],
        out_specs=pl.BlockSpec((1, 2, t, t), lambda m: (m, 0, 0, 0)),
        out_shape=jax.ShapeDtypeStruct((n_maps, 2, t, t), F32),
        name="rel_bias_tiles",
    )(rel_bias.astype(F32), idx)


def _ffn_kernel(*refs, n_chunks, has_proj, has_final):
    refs = list(refs)
    x_ref = refs.pop(0)
    if has_proj:
        o_ref, wo_ref = refs.pop(0), refs.pop(0)
    g_ref, win_ref, wout_ref = refs.pop(0), refs.pop(0), refs.pop(0)
    if has_final:
        gf_ref = refs.pop(0)
    (out_ref,) = refs

    x = x_ref[...]
    if has_proj:
        x = x + jnp.dot(o_ref[...], wo_ref[...], preferred_element_type=F32)
    hn = _rms(x, g_ref[...], RMS_EPS).astype(BF16)
    acc = None
    d_ff = n_chunks * FFN_CHUNK
    for c in range(n_chunks):
        cols = slice(c * FFN_CHUNK, (c + 1) * FFN_CHUNK)
        gate = jnp.dot(hn, win_ref[:, cols], preferred_element_type=F32)
        up = jnp.dot(hn, win_ref[:, d_ff + c * FFN_CHUNK:d_ff + (c + 1) * FFN_CHUNK], preferred_element_type=F32)
        a = (gate * (1.0 / (1.0 + jnp.exp(-gate))) * up).astype(BF16)
        part = jnp.dot(a, wout_ref[cols, :], preferred_element_type=F32)
        acc = part if acc is None else acc + part
    y = x + FFN_RESIDUAL * acc
    if has_final:
        y = _rms(y, gf_ref[...], RMS_EPS)
    out_ref[...] = y


def _ffn(x2d, g, w_in, w_out, proj=None, final_g=None):
    T, D = x2d.shape
    F = w_out.shape[0]
    n_chunks = F // FFN_CHUNK
    assert n_chunks * FFN_CHUNK == F
    tm = math.gcd(T, FFN_TOKEN_TILE)
    w_in_c = w_in.astype(BF16)
    w_out_c = w_out.astype(BF16)

    row = lambda i: (i, 0)
    args, specs = [x2d], [pl.BlockSpec((tm, D), row)]
    if proj is not None:
        o2d, w_o = proj
        args += [o2d, w_o.astype(BF16)]
        specs += [pl.BlockSpec((tm, o2d.shape[1]), row), _const_spec(w_o.shape)]
    args += [g.reshape(1, D).astype(F32), w_in_c, w_out_c]
    specs += [_const_spec((1, D)), _const_spec(w_in_c.shape), _const_spec(w_out_c.shape)]
    if final_g is not None:
        args.append(final_g.reshape(1, D).astype(F32))
        specs.append(_const_spec((1, D)))

    return pl.pallas_call(
        functools.partial(_ffn_kernel, n_chunks=n_chunks, has_proj=proj is not None,
                          has_final=final_g is not None),
        grid=(T // tm,),
        in_specs=specs,
        out_specs=pl.BlockSpec((tm, D), row),
        out_shape=jax.ShapeDtypeStruct((T, D), F32),
        compiler_params=pltpu.CompilerParams(dimension_semantics=("parallel",),
                                             vmem_limit_bytes=VMEM_LIMIT_BYTES),
        name="ffn",
    )(*args)


def _proj_kernel(x_ref, g_ref, wqT_ref, wk_ref, wvT_ref, qT_ref, k_ref, vT_ref, *, n_pairs, n_sub, v_dim):
    hn = _rms(x_ref[0], g_ref[...], RMS_EPS).astype(BF16)
    k_ref[0] = jnp.dot(hn, wk_ref[...], preferred_element_type=F32).astype(BF16)
    qT = lax.dot_general(wqT_ref[...], hn, _NT, preferred_element_type=F32).astype(BF16)
    for c in range(n_sub):
        qT_ref[0, c] = qT[:, c * ATT_TILE:(c + 1) * ATT_TILE]
    vT = lax.dot_general(wvT_ref[...], hn, _NT, preferred_element_type=F32).astype(BF16)
    ones_rows = jnp.where(lax.broadcasted_iota(jnp.int32, (ONES_ROWS, ATT_TILE), 0) == 0, 1.0, 0.0).astype(BF16)
    group = v_dim + ONES_ROWS
    for p in range(n_pairs):
        for c in range(n_sub):
            cols = slice(c * ATT_TILE, (c + 1) * ATT_TILE)
            for i in range(PAIR // v_dim):
                vT_ref[0, p, c, i * group:i * group + v_dim, :] = vT[p * PAIR + i * v_dim:p * PAIR + (i + 1) * v_dim, cols]
                vT_ref[0, p, c, i * group + v_dim:(i + 1) * group, :] = ones_rows


def _v_rows(v_dim):
    return (PAIR // v_dim) * (v_dim + ONES_ROWS)


def _qkv_proj(h, g, w_qkv, v_dim):
    B, S, D = h.shape
    tm = TOKEN_TILE
    n_pairs, n_sub, nk, rows = D // PAIR, tm // ATT_TILE, S // ATT_TILE, _v_rows(v_dim)
    wq, wk, wv = w_qkv[:, :D], w_qkv[:, D:2 * D], w_qkv[:, 2 * D:]
    wqT = (wq * (HEAD_DIM ** -0.5 * LOG2E)).T.astype(BF16)
    wvT = wv.T.astype(BF16)
    return pl.pallas_call(
        functools.partial(_proj_kernel, n_pairs=n_pairs, n_sub=n_sub, v_dim=v_dim),
        grid=(B, S // tm),
        in_specs=[pl.BlockSpec((1, tm, D), lambda b, s: (b, s, 0)),
                  _const_spec((1, D)), _const_spec((D, D)), _const_spec((D, D)), _const_spec((D, D))],
        out_specs=[pl.BlockSpec((1, n_sub, D, ATT_TILE), lambda b, s: (b, s, 0, 0)),
                   pl.BlockSpec((1, tm, D), lambda b, s: (b, s, 0)),
                   pl.BlockSpec((1, n_pairs, n_sub, rows, ATT_TILE), lambda b, s: (b, 0, s, 0, 0))],
        out_shape=[jax.ShapeDtypeStruct((B, nk, D, ATT_TILE), BF16),
                   jax.ShapeDtypeStruct((B, S, D), BF16),
                   jax.ShapeDtypeStruct((B, n_pairs, nk, rows, ATT_TILE), BF16)],
        compiler_params=pltpu.CompilerParams(dimension_semantics=("parallel", "parallel"),
                                             vmem_limit_bytes=VMEM_LIMIT_BYTES),
        name="qkv_proj",
    )(h, g.reshape(1, D).astype(F32), wqT, wk.astype(BF16), wvT)


def _split_pairs(qT_ref, qs, qz_ref):
    zeros = jnp.zeros((HEAD_DIM, qT_ref.shape[3]), qT_ref.dtype)
    for g in range(PAIRS_PER_STEP):
        q = qT_ref[0, qs, g * PAIR:(g + 1) * PAIR, :]
        qz_ref[2 * g, :HEAD_DIM, :] = q[:HEAD_DIM]
        qz_ref[2 * g, HEAD_DIM:, :] = zeros
        qz_ref[2 * g + 1, :HEAD_DIM, :] = zeros
        qz_ref[2 * g + 1, HEAD_DIM:, :] = q[HEAD_DIM:]


def _update(m, scores, vT_tiles, biases, m_ref, acc_ref, first):
    scores = [s if b is None else b + s for s, (b, _) in zip(scores, biases)]
    masks = [r for _, r in biases]
    if all(r is None for r in masks):
        m_cur = jnp.max(functools.reduce(jnp.maximum, scores), axis=0, keepdims=True)
    else:
        m_cur = functools.reduce(jnp.maximum, [jnp.max(s, axis=0, keepdims=True) + (0.0 if r is None else r)
                                               for s, r in zip(scores, masks)])
    if first:
        m_new = m_cur
    else:
        m_old = m_ref[m]
        m_new = jnp.maximum(m_old, m_cur)
    shifts = [m_new if r is None else m_new - r for r in masks]
    pv = functools.reduce(jnp.add, [jnp.dot(v, jnp.exp2(s - sh).astype(BF16), preferred_element_type=F32)
                                    for v, s, sh in zip(vT_tiles, scores, shifts)])
    if first:
        acc_ref[m] = pv
    else:
        acc_ref[m] = jnp.exp2(m_old - m_new) * acc_ref[m] + pv
    m_ref[m] = m_new


def _sweep(qi, k_ref, vT_ref, v_rows, near_bias, far_bias, state, pre_ref, stage, before_past=None):
    qz_ref, m_ref, acc_ref = state
    n_maps = 2 * PAIRS_PER_STEP

    def raw_scores(j, m):
        rows = pl.ds(pl.multiple_of(j * ATT_TILE, ATT_TILE), ATT_TILE)
        k_t = k_ref[0, rows, (m // 2) * PAIR:(m // 2 + 1) * PAIR]
        return jnp.dot(k_t, qz_ref[m], preferred_element_type=F32)

    def block(tiles, biases, first, ahead, preloaded, next_tiles, next_ahead):
        scores = {}
        for m in range(ahead):
            scores[m] = ([pre_ref[m, t] for t in range(len(tiles))] if preloaded
                         else [raw_scores(j, m) for j in tiles])
        for m in range(n_maps):
            nxt = m + ahead
            if nxt < n_maps:
                scores[nxt] = [raw_scores(j, nxt) for j in tiles]
            elif nxt - n_maps < next_ahead:
                for t, j in enumerate(next_tiles):
                    pre_ref[nxt - n_maps, t] = raw_scores(j, nxt - n_maps)
            _update(m, scores.pop(m), [v_rows(vT_ref[0, m // 2, j], m) for j in tiles], biases(m),
                    m_ref, acc_ref, first)

    n_far = jnp.maximum(qi - 1, 0)
    n_pairs = n_far // 2

    @pl.when(qi == 0)
    def _():
        block([qi], lambda m: [near_bias(m, 0, qi)], True, QK_AHEAD, False, [], 0)

    @pl.when(qi >= 1)
    def _():
        if before_past is not None:
            before_past()
        block([qi - 1, qi], lambda m: [near_bias(m, 1, qi - 1), near_bias(m, 0, qi)], True, QK_AHEAD_PAIR, False,
              [], 0)

    @pl.when(n_far % 2 == 1)
    def _():
        block([n_far - 1], lambda m: [far_bias(m, n_far - 1)], False, QK_AHEAD, False, [], 0)

    s_bufs, p_bufs, a_bufs = stage

    def scores_stage(p, slot, m):
        for t in range(2):
            s_bufs[slot][m, t] = raw_scores(2 * p + t, m)

    def values_stage(p, slot, m):
        pv = functools.reduce(jnp.add, [jnp.dot(v_rows(vT_ref[0, m // 2, 2 * p + t], m), p_bufs[slot][m, t],
                                                preferred_element_type=F32) for t in range(2)])
        acc_ref[m] = a_bufs[slot][m] * acc_ref[m] + pv

    def probs_stage(p, slot, m, valid):
        scores = [s_bufs[slot][m, t] for t in range(2)]
        masks = [far_bias(m, 2 * p + t)[1] for t in range(2)]
        masks = [valid if r is None else r + valid for r in masks]
        m_cur = functools.reduce(jnp.maximum, [jnp.max(sc, axis=0, keepdims=True) + r for sc, r in zip(scores, masks)])
        m_old = m_ref[m]
        m_new = jnp.maximum(m_old, m_cur)
        for t in range(2):
            p_bufs[slot][m, t] = jnp.exp2(scores[t] - (m_new - masks[t])).astype(BF16)
        a_bufs[slot][m] = jnp.exp2(m_old - m_new)
        m_ref[m] = m_new

    last = jnp.maximum(n_pairs - 1, 0)

    @pl.when(n_pairs >= 1)
    def _():
        for m in range(n_maps):
            scores_stage(0, 0, m)
        p_bufs[1][...] = jnp.zeros(p_bufs[1].shape, BF16)
        a_bufs[1][...] = jnp.ones(a_bufs[1].shape, F32)

    def half(p, cur, other):
        valid = jnp.where(p < n_pairs, 0.0, MASK_VALUE).astype(F32) * jnp.ones((1, ATT_TILE), F32)
        p_here, p_next, p_prev = jnp.minimum(p, last), jnp.minimum(p + 1, last), jnp.clip(p - 1, 0, last)
        for m in range(n_maps):
            scores_stage(p_next, other, m)
            values_stage(p_prev, other, m)
            probs_stage(p_here, cur, m, valid)

    def far_pairs(i, carry):
        half(2 * i, 0, 1)
        half(2 * i + 1, 1, 0)
        return carry

    lax.fori_loop(0, (n_pairs + 1) // 2, far_pairs, 0)

    @pl.when(n_pairs >= 1)
    def _():
        for m in range(n_maps):
            values_stage(last, 1, m)


def _for_each_query_tile(body):
    first = pl.program_id(2) * Q_TILES_PER_STEP

    def one(qs, carry):
        body(qs, first + qs, pl.ds(pl.multiple_of(qs * ATT_TILE, ATT_TILE), ATT_TILE))
        return carry

    if Q_TILES_PER_STEP == 1:
        one(0, 0)
    else:
        lax.fori_loop(0, Q_TILES_PER_STEP, one, 0)


def _diff_kernel(qT_ref, k_ref, vT_ref, bias_ref, lam_ref, g_ref, o_ref,
                 qz_ref, m_ref, pre_ref, s0_buf, s1_buf, p0_buf, p1_buf, a0_buf, a1_buf, acc_ref, *, lam_init):
    def tile(qs, qi, rows):
        _split_pairs(qT_ref, qs, qz_ref)
        _sweep(qi, k_ref, vT_ref, lambda v, m: v,
               lambda m, t, j: (bias_ref[m, t], None), lambda m, j: (None, None),
               (qz_ref, m_ref, acc_ref), pre_ref, ([s0_buf, s1_buf], [p0_buf, p1_buf], [a0_buf, a1_buf]))

        lp = lam_ref[...]
        lam = (jnp.exp(jnp.sum(lp[0:1] * lp[1:2], axis=-1, keepdims=True))
               - jnp.exp(jnp.sum(lp[2:3] * lp[3:4], axis=-1, keepdims=True)) + lam_init)

        def weighted(m, scale):
            return acc_ref[m, :PAIR, :] * (scale / acc_ref[m, PAIR:PAIR + 1, :])

        for g in range(PAIRS_PER_STEP):
            o = weighted(2 * g, 1.0) - weighted(2 * g + 1, lam)
            o = o * lax.rsqrt(jnp.mean(o * o, axis=0, keepdims=True) + SUBLN_EPS) * g_ref[...]
            o_ref[0, rows, g * PAIR:(g + 1) * PAIR] = o.T.astype(BF16)

    _for_each_query_tile(tile)


def _moba_kernel(qT_ref, k_ref, vT_ref, bias_ref, o_ref,
                 qz_ref, m_ref, pre_ref, s0_buf, s1_buf, p0_buf, p1_buf, a0_buf, a1_buf, acc_ref, kmean_ref, selb_ref, *, nk):
    tq = qT_ref.shape[3]
    head_rows = HEAD_DIM + ONES_ROWS

    def tile(qs, qi, rows):
        @pl.when(qi == 0)
        def _():
            for j in range(nk):
                kb = k_ref[0, j * MOBA_BLOCK:(j + 1) * MOBA_BLOCK, :].astype(F32)
                kmean_ref[j:j + 1, :] = jnp.mean(kb, axis=0, keepdims=True)

        _split_pairs(qT_ref, qs, qz_ref)

        def select_blocks():
            blk = lax.broadcasted_iota(jnp.int32, (nk, tq), 0).astype(F32)
            eligible = blk < qi.astype(F32)
            for m in range(2 * PAIRS_PER_STEP):
                rest = kmean_ref[:, (m // 2) * PAIR:(m // 2 + 1) * PAIR]
                gate = jnp.zeros((nk, tq), F32)
                for _ in range(KMEAN_TERMS):
                    term = rest.astype(BF16)
                    gate = gate + jnp.dot(term, qz_ref[m], preferred_element_type=F32)
                    rest = rest - term.astype(F32)
                gate = jnp.where(eligible, gate, -jnp.inf)
                picked = jnp.zeros((nk, tq), F32)
                for _ in range(MOBA_TOPK):
                    best = jnp.max(gate, axis=0, keepdims=True)
                    pick = blk == jnp.min(jnp.where(gate == best, blk, float(nk)), axis=0, keepdims=True)
                    picked = jnp.where(pick, 1.0, picked)
                    gate = jnp.where(pick, -jnp.inf, gate)
                selb_ref[m] = jnp.where(eligible, jnp.where(picked > 0.0, 0.0, MASK_VALUE), MASK_VALUE)

        _sweep(qi, k_ref, vT_ref, lambda v, m: v[(m % 2) * head_rows:(m % 2 + 1) * head_rows],
               lambda m, t, j: (bias_ref[m, t], None if t == 0 else selb_ref[m, pl.ds(j, 1), :]),
               lambda m, j: (None, selb_ref[m, pl.ds(j, 1), :]),
               (qz_ref, m_ref, acc_ref), pre_ref, ([s0_buf, s1_buf], [p0_buf, p1_buf], [a0_buf, a1_buf]), before_past=select_blocks)

        for g in range(PAIRS_PER_STEP):
            o = jnp.concatenate([acc_ref[m, :HEAD_DIM, :] * (1.0 / acc_ref[m, HEAD_DIM:HEAD_DIM + 1, :])
                                 for m in (2 * g, 2 * g + 1)], axis=0)
            o_ref[0, rows, g * PAIR:(g + 1) * PAIR] = o.T.astype(BF16)

    _for_each_query_tile(tile)


def _attention(kind, qT, k, vT, bias, extra, *, lam_init=None):
    B, nk, D, t = qT.shape
    G, S, T = PAIRS_PER_STEP, nk * t, Q_TILES_PER_STEP
    n_maps = 2 * G
    assert D % (G * PAIR) == 0 and nk >= 2 and nk % T == 0 and t == ATT_TILE and QK_AHEAD_PAIR <= QK_AHEAD <= n_maps
    in_specs = [pl.BlockSpec((1, T, G * PAIR, t), lambda b, p, q: (b, q, p, 0)),
                pl.BlockSpec((1, S, G * PAIR), lambda b, p, q: (b, 0, p)),
                pl.BlockSpec((1, G, nk, vT.shape[3], t), lambda b, p, q: (b, p, 0, 0, 0)),
                pl.BlockSpec((n_maps, 2, t, t), lambda b, p, q: (p, 0, 0, 0), pipeline_mode=pl.Buffered(1))]
    scratch = [pltpu.VMEM((n_maps, PAIR, t), BF16),
               pltpu.VMEM((n_maps, 1, t), F32),
               pltpu.VMEM((max(QK_AHEAD, QK_AHEAD_PAIR), 2, t, t), F32),
               pltpu.VMEM((n_maps, 2, t, t), F32), pltpu.VMEM((n_maps, 2, t, t), F32),
               pltpu.VMEM((n_maps, 2, t, t), BF16), pltpu.VMEM((n_maps, 2, t, t), BF16),
               pltpu.VMEM((n_maps, 1, t), F32), pltpu.VMEM((n_maps, 1, t), F32)]
    if kind == "diff":
        body = functools.partial(_diff_kernel, lam_init=lam_init)
        in_specs += [pl.BlockSpec(e.shape, lambda b, p, q: (0, 0)) for e in extra]
        scratch += [pltpu.VMEM((n_maps, PAIR + ONES_ROWS, t), F32)]
    else:
        body = functools.partial(_moba_kernel, nk=nk)
        scratch += [pltpu.VMEM((n_maps, HEAD_DIM + ONES_ROWS, t), F32),
                    pltpu.VMEM((nk, G * PAIR), F32),
                    pltpu.VMEM((n_maps, nk, t), F32)]
    return pl.pallas_call(
        body,
        grid=(B, D // (G * PAIR), nk // T),
        in_specs=in_specs,
        out_specs=pl.BlockSpec((1, T * t, G * PAIR), lambda b, p, q: (b, q, p)),
        out_shape=jax.ShapeDtypeStruct((B, S, D), BF16),
        scratch_shapes=scratch,
        compiler_params=pltpu.CompilerParams(dimension_semantics=("parallel", "parallel", "arbitrary"),
                                             vmem_limit_bytes=VMEM_LIMIT_BYTES),
        name=kind + "_attention",
    )(qT, k, vT, bias, *extra)


def kernel(x, rel_bias, norm_g, final_norm_g, ffn_w_in, ffn_w_out, diff_w_qkv, diff_lambda,
           diff_subln_g, diff_w_o, moba_w_qkv, moba_w_o):
    B, S, D = x.shape
    depth = norm_g.shape[0]
    assert S % TOKEN_TILE == 0 and D % PAIR == 0 and MOBA_BLOCK == ATT_TILE
    assert rel_bias.shape == (REL_BUCKETS, D // HEAD_DIM)
    bias = _bias_tiles(rel_bias, ATT_TILE)

    h = x.reshape(B * S, D)
    for i in range(depth):
        g = norm_g[i]
        h = _ffn(h, g[0], ffn_w_in[i, 0], ffn_w_out[i, 0])
        j = i // 2
        if i % 2 == 0:
            qT, k, vT = _qkv_proj(h.reshape(B, S, D), g[1], diff_w_qkv[j], PAIR)
            lam_init = 0.8 - 0.6 * math.exp(-0.3 * i)
            g_sub = jnp.broadcast_to((diff_subln_g[j].astype(F32) * (1.0 - lam_init))[:, None], (PAIR, ATT_TILE))
            o = _attention("diff", qT, k, vT, bias, (diff_lambda[j].astype(F32), g_sub), lam_init=lam_init)
            w_o = diff_w_o[j]
        else:
            qT, k, vT = _qkv_proj(h.reshape(B, S, D), g[1], moba_w_qkv[j], HEAD_DIM)
            o = _attention("moba", qT, k, vT, bias, ())
            w_o = moba_w_o[j]
        last = i == depth - 1
        h = _ffn(h, g[2], ffn_w_in[i, 1], ffn_w_out[i, 1], proj=(o.reshape(B * S, D), w_o),
                 final_g=final_norm_g if last else None)
    return h.reshape(B, S, D)
```

```python
import functools
import math

import numpy as np
import jax
import jax.numpy as jnp
from jax import lax
from jax.experimental import pallas as pl
from jax.experimental.pallas import tpu as pltpu

HEAD_DIM = 64
PAIR = 2 * HEAD_DIM
MOBA_BLOCK = 256
MOBA_TOPK = 3
KMEAN_TERMS = 3
ONES_ROWS = 16
REL_BUCKETS = 32
REL_MAX_DIST = 128
FFN_RESIDUAL = 0.5
RMS_EPS = 1e-6
SUBLN_EPS = 1e-5

ATT_TILE = 256
PAIRS_PER_STEP = 8
Q_TILES_PER_STEP = 2
QK_AHEAD = 4
QK_AHEAD_PAIR = 2
FFN_CHUNK = 256
TOKEN_TILE = 512
FFN_TOKEN_TILE = 1024
MASK_VALUE = -1e30
LOG2E = math.log2(math.e)
VMEM_LIMIT_BYTES = 62 * 1024 * 1024

F32 = jnp.float32
BF16 = jnp.bfloat16
_NT = (((1,), (1,)), ((), ()))


def _rms(x, g, eps):
    return x * lax.rsqrt(jnp.mean(x * x, axis=-1, keepdims=True) + eps) * g


def _const_spec(shape):
    return pl.BlockSpec(shape, lambda *_: (0,) * len(shape), pipeline_mode=pl.Buffered(1))


def _rel_bucket_np(dist):
    n = np.maximum(dist, 0)
    max_exact = REL_BUCKETS // 2
    nf = np.maximum(n, 1).astype(np.float32)
    large = max_exact + (np.log(nf / np.float32(max_exact)) / np.float32(math.log(REL_MAX_DIST / max_exact))
                         * np.float32(REL_BUCKETS - max_exact)).astype(np.int32)
    large = np.minimum(large, REL_BUCKETS - 1)
    return np.where(n < max_exact, n, large).astype(np.int32)


def _bucket_tiles(t):
    j = np.arange(t)[:, None]
    i = np.arange(t)[None, :]
    diag = np.where(i - j >= 0, _rel_bucket_np(i - j), -1)
    prev = _rel_bucket_np(i - j + t)
    assert _rel_bucket_np(np.arange(t + 1, 8 * t)).min() == REL_BUCKETS - 1
    return np.stack([diag, prev]).astype(np.int32)


def _bias_kernel(rb_ref, idx_ref, out_ref):
    m = pl.program_id(0)
    far = rb_ref[REL_BUCKETS - 1, m]
    for t in range(2):
        idx = idx_ref[t]
        acc = jnp.zeros(idx.shape, F32)
        for b in range(REL_BUCKETS - 1):
            acc = jnp.where(idx == b, (rb_ref[b, m] - far) * LOG2E, acc)
        out_ref[0, t] = jnp.where(idx < 0, MASK_VALUE, acc)


def _bias_tiles(rel_bias, t):
    n_maps = rel_bias.shape[1]
    idx = jnp.asarray(_bucket_tiles(t))
    return pl.pallas_call(
        _bias_kernel,
        grid=(n_maps,),
        in_specs=[pl.BlockSpec(memory_space=pltpu.SMEM),
                  pl.BlockSpec((2, t, t), lambda m: (0, 0, 0))],
        out_specs=pl.BlockSpec((1, 2, t, t), lambda m: (m, 0, 0, 0)),
        out_shape=jax.ShapeDtypeStruct((n_maps, 2, t, t), F32),
        name="rel_bias_tiles",
    )(rel_bias.astype(F32), idx)


def _ffn_kernel(*refs, n_chunks, has_proj, has_final):
    refs = list(refs)
    x_ref = refs.pop(0)
    if has_proj:
        o_ref, wo_ref = refs.pop(0), refs.pop(0)
    g_ref, win_ref, wout_ref = refs.pop(0), refs.pop(0), refs.pop(0)
    if has_final:
        gf_ref = refs.pop(0)
    (out_ref,) = refs

    x = x_ref[...]
    if has_proj:
        x = x + jnp.dot(o_ref[...], wo_ref[...], preferred_element_type=F32)
    hn = _rms(x, g_ref[...], RMS_EPS).astype(BF16)
    acc = None
    d_ff = n_chunks * FFN_CHUNK
    for c in range(n_chunks):
        cols = slice(c * FFN_CHUNK, (c + 1) * FFN_CHUNK)
        gate = jnp.dot(hn, win_ref[:, cols], preferred_element_type=F32)
        up = jnp.dot(hn, win_ref[:, d_ff + c * FFN_CHUNK:d_ff + (c + 1) * FFN_CHUNK], preferred_element_type=F32)
        a = (gate * (1.0 / (1.0 + jnp.exp(-gate))) * up).astype(BF16)
        part = jnp.dot(a, wout_ref[cols, :], preferred_element_type=F32)
        acc = part if acc is None else acc + part
    y = x + FFN_RESIDUAL * acc
    if has_final:
        y = _rms(y, gf_ref[...], RMS_EPS)
    out_ref[...] = y


def _ffn(x2d, g, w_in, w_out, proj=None, final_g=None):
    T, D = x2d.shape
    F = w_out.shape[0]
    n_chunks = F // FFN_CHUNK
    assert n_chunks * FFN_CHUNK == F
    tm = math.gcd(T, FFN_TOKEN_TILE)
    w_in_c = w_in.astype(BF16)
    w_out_c = w_out.astype(BF16)

    row = lambda i: (i, 0)
    args, specs = [x2d], [pl.BlockSpec((tm, D), row)]
    if proj is not None:
        o2d, w_o = proj
        args += [o2d, w_o.astype(BF16)]
        specs += [pl.BlockSpec((tm, o2d.shape[1]), row), _const_spec(w_o.shape)]
    args += [g.reshape(1, D).astype(F32), w_in_c, w_out_c]
    specs += [_const_spec((1, D)), _const_spec(w_in_c.shape), _const_spec(w_out_c.shape)]
    if final_g is not None:
        args.append(final_g.reshape(1, D).astype(F32))
        specs.append(_const_spec((1, D)))

    return pl.pallas_call(
        functools.partial(_ffn_kernel, n_chunks=n_chunks, has_proj=proj is not None,
                          has_final=final_g is not None),
        grid=(T // tm,),
        in_specs=specs,
        out_specs=pl.BlockSpec((tm, D), row),
        out_shape=jax.ShapeDtypeStruct((T, D), F32),
        compiler_params=pltpu.CompilerParams(dimension_semantics=("parallel",),
                                             vmem_limit_bytes=VMEM_LIMIT_BYTES),
        name="ffn",
    )(*args)


def _proj_kernel(x_ref, g_ref, wqT_ref, wk_ref, wvT_ref, qT_ref, k_ref, vT_ref, *, n_pairs, n_sub, v_dim):
    hn = _rms(x_ref[0], g_ref[...], RMS_EPS).astype(BF16)
    k_ref[0] = jnp.dot(hn, wk_ref[...], preferred_element_type=F32).astype(BF16)
    qT = lax.dot_general(wqT_ref[...], hn, _NT, preferred_element_type=F32).astype(BF16)
    for c in range(n_sub):
        qT_ref[0, c] = qT[:, c * ATT_TILE:(c + 1) * ATT_TILE]
    vT = lax.dot_general(wvT_ref[...], hn, _NT, preferred_element_type=F32).astype(BF16)
    ones_rows = jnp.where(lax.broadcasted_iota(jnp.int32, (ONES_ROWS, ATT_TILE), 0) == 0, 1.0, 0.0).astype(BF16)
    group = v_dim + ONES_ROWS
    for p in range(n_pairs):
        for c in range(n_sub):
            cols = slice(c * ATT_TILE, (c + 1) * ATT_TILE)
            for i in range(PAIR // v_dim):
                vT_ref[0, p, c, i * group:i * group + v_dim, :] = vT[p * PAIR + i * v_dim:p * PAIR + (i + 1) * v_dim, cols]
                vT_ref[0, p, c, i * group + v_dim:(i + 1) * group, :] = ones_rows


def _v_rows(v_dim):
    return (PAIR // v_dim) * (v_dim + ONES_ROWS)


def _qkv_proj(h, g, w_qkv, v_dim):
    B, S, D = h.shape
    tm = TOKEN_TILE
    n_pairs, n_sub, nk, rows = D // PAIR, tm // ATT_TILE, S // ATT_TILE, _v_rows(v_dim)
    wq, wk, wv = w_qkv[:, :D], w_qkv[:, D:2 * D], w_qkv[:, 2 * D:]
    wqT = (wq * (HEAD_DIM ** -0.5 * LOG2E)).T.astype(BF16)
    wvT = wv.T.astype(BF16)
    return pl.pallas_call(
        functools.partial(_proj_kernel, n_pairs=n_pairs, n_sub=n_sub, v_dim=v_dim),
        grid=(B, S // tm),
        in_specs=[pl.BlockSpec((1, tm, D), lambda b, s: (b, s, 0)),
                  _const_spec((1, D)), _const_spec((D, D)), _const_spec((D, D)), _const_spec((D, D))],
        out_specs=[pl.BlockSpec((1, n_sub, D, ATT_TILE), lambda b, s: (b, s, 0, 0)),
                   pl.BlockSpec((1, tm, D), lambda b, s: (b, s, 0)),
                   pl.BlockSpec((1, n_pairs, n_sub, rows, ATT_TILE), lambda b, s: (b, 0, s, 0, 0))],
        out_shape=[jax.ShapeDtypeStruct((B, nk, D, ATT_TILE), BF16),
                   jax.ShapeDtypeStruct((B, S, D), BF16),
                   jax.ShapeDtypeStruct((B, n_pairs, nk, rows, ATT_TILE), BF16)],
        compiler_params=pltpu.CompilerParams(dimension_semantics=("parallel", "parallel"),
                                             vmem_limit_bytes=VMEM_LIMIT_BYTES),
        name="qkv_proj",
    )(h, g.reshape(1, D).astype(F32), wqT, wk.astype(BF16), wvT)


def _split_pairs(qT_ref, qs, qz_ref):
    zeros = jnp.zeros((HEAD_DIM, qT_ref.shape[3]), qT_ref.dtype)
    for g in range(PAIRS_PER_STEP):
        q = qT_ref[0, qs, g * PAIR:(g + 1) * PAIR, :]
        qz_ref[2 * g, :HEAD_DIM, :] = q[:HEAD_DIM]
        qz_ref[2 * g, HEAD_DIM:, :] = zeros
        qz_ref[2 * g + 1, :HEAD_DIM, :] = zeros
        qz_ref[2 * g + 1, HEAD_DIM:, :] = q[HEAD_DIM:]


def _update(m, scores, vT_tiles, biases, m_ref, acc_ref, first):
    scores = [s if b is None else b + s for s, (b, _) in zip(scores, biases)]
    masks = [r for _, r in biases]
    if all(r is None for r in masks):
        m_cur = jnp.max(functools.reduce(jnp.maximum, scores), axis=0, keepdims=True)
    else:
        m_cur = functools.reduce(jnp.maximum, [jnp.max(s, axis=0, keepdims=True) + (0.0 if r is None else r)
                                               for s, r in zip(scores, masks)])
    if first:
        m_new = m_cur
    else:
        m_old = m_ref[m]
        m_new = jnp.maximum(m_old, m_cur)
    shifts = [m_new if r is None else m_new - r for r in masks]
    pv = functools.reduce(jnp.add, [jnp.dot(v, jnp.exp2(s - sh).astype(BF16), preferred_element_type=F32)
                                    for v, s, sh in zip(vT_tiles, scores, shifts)])
    if first:
        acc_ref[m] = pv
    else:
        acc_ref[m] = jnp.exp2(m_old - m_new) * acc_ref[m] + pv
    m_ref[m] = m_new


def _sweep(qi, k_ref, vT_ref, v_rows, near_bias, far_bias, state, pre_ref, with_first_block=(), may_be_first_tile=True):
    qz_ref, m_ref, acc_ref = state
    n_maps = 2 * PAIRS_PER_STEP

    def raw_scores(j, m):
        rows = pl.ds(pl.multiple_of(j * ATT_TILE, ATT_TILE), ATT_TILE)
        k_t = k_ref[0, rows, (m // 2) * PAIR:(m // 2 + 1) * PAIR]
        return jnp.dot(k_t, qz_ref[m], preferred_element_type=F32)

    def block(tiles, biases, first, ahead, preloaded, next_tiles, next_ahead):
        scores = {}
        for m in range(ahead):
            scores[m] = ([pre_ref[m, t] for t in range(len(tiles))] if preloaded
                         else [raw_scores(j, m) for j in tiles])
        for m in range(n_maps):
            nxt = m + ahead
            if nxt < n_maps:
                scores[nxt] = [raw_scores(j, nxt) for j in tiles]
            elif nxt - n_maps < next_ahead:
                for t, j in enumerate(next_tiles):
                    pre_ref[nxt - n_maps, t] = raw_scores(j, nxt - n_maps)
            _update(m, scores.pop(m), [v_rows(vT_ref[0, m // 2, j], m) for j in tiles], biases(m),
                    m_ref, acc_ref, first)

    n_far = jnp.maximum(qi - 1, 0)
    n_pairs = n_far // 2

    def first_pair():
        for traced_alongside in with_first_block:
            traced_alongside()
        block([qi - 1, qi], lambda m: [near_bias(m, 1, qi - 1), near_bias(m, 0, qi)], True, QK_AHEAD_PAIR, False,
              [0, 1], QK_AHEAD_PAIR)

    if may_be_first_tile:
        @pl.when(qi == 0)
        def _():
            block([qi], lambda m: [near_bias(m, 0, qi)], True, QK_AHEAD, False, [], 0)

        pl.when(qi >= 1)(first_pair)
    else:
        first_pair()

    @pl.when(n_far % 2 == 1)
    def _():
        block([n_far - 1], lambda m: [far_bias(m, n_far - 1)], False, QK_AHEAD, False, [], 0)

    def far_pair(p, carry):
        j, jn = 2 * p, 2 * jnp.minimum(p + 1, n_pairs - 1)
        block([j, j + 1], lambda m: [far_bias(m, j), far_bias(m, j + 1)], False, QK_AHEAD_PAIR, True,
              [jn, jn + 1], QK_AHEAD_PAIR)
        return carry

    lax.fori_loop(0, n_pairs, far_pair, 0)


def _query_tiles(step_body):
    first = pl.program_id(2) * Q_TILES_PER_STEP
    pending = []
    for qs in range(Q_TILES_PER_STEP):
        sweep, finish = step_body(qs, first + qs)
        sweep(pending)
        pending = [finish]
    pending[0]()


def _diff_kernel(qT_ref, k_ref, vT_ref, bias_ref, lam_ref, g_ref, o_ref,
                 qz_ref, m_ref, pre_ref, acc_ref, *, lam_init):
    def tile(qs, qi):
        qz, m_run, acc = qz_ref.at[qs], m_ref.at[qs], acc_ref.at[qs]

        def sweep(extra):
            if qs == 0:
                _split_pairs(qT_ref, qs, qz)
                first = []
            else:
                first = [functools.partial(_split_pairs, qT_ref, qs, qz)]
            _sweep(qi, k_ref, vT_ref, lambda v, m: v,
                   lambda m, t, j: (bias_ref[m, t], None), lambda m, j: (None, None),
                   (qz, m_run, acc), pre_ref, with_first_block=first + extra, may_be_first_tile=qs == 0)

        def finish():
            lp = lam_ref[...]
            lam = (jnp.exp(jnp.sum(lp[0:1] * lp[1:2], axis=-1, keepdims=True))
                   - jnp.exp(jnp.sum(lp[2:3] * lp[3:4], axis=-1, keepdims=True)) + lam_init)

            def weighted(m, scale):
                return acc[m, :PAIR, :] * (scale / acc[m, PAIR:PAIR + 1, :])

            for g in range(PAIRS_PER_STEP):
                o = weighted(2 * g, 1.0) - weighted(2 * g + 1, lam)
                o = o * lax.rsqrt(jnp.mean(o * o, axis=0, keepdims=True) + SUBLN_EPS) * g_ref[...]
                o_ref[0, qs * ATT_TILE:(qs + 1) * ATT_TILE, g * PAIR:(g + 1) * PAIR] = o.T.astype(BF16)

        return sweep, finish

    _query_tiles(tile)


def _moba_kernel(qT_ref, k_ref, vT_ref, bias_ref, o_ref,
                 qz_ref, m_ref, pre_ref, acc_ref, kmean_ref, selb_ref, *, nk):
    tq = qT_ref.shape[3]
    head_rows = HEAD_DIM + ONES_ROWS

    def tile(qs, qi):
        qz, m_run, acc = qz_ref.at[qs], m_ref.at[qs], acc_ref.at[qs]

        def select_blocks():
            blk = lax.broadcasted_iota(jnp.int32, (nk, tq), 0).astype(F32)
            eligible = blk < qi.astype(F32)
            for m in range(2 * PAIRS_PER_STEP):
                rest = kmean_ref[:, (m // 2) * PAIR:(m // 2 + 1) * PAIR]
                gate = jnp.zeros((nk, tq), F32)
                for _ in range(KMEAN_TERMS):
                    term = rest.astype(BF16)
                    gate = gate + jnp.dot(term, qz[m], preferred_element_type=F32)
                    rest = rest - term.astype(F32)
                gate = jnp.where(eligible, gate, -jnp.inf)
                picked = jnp.zeros((nk, tq), F32)
                for _ in range(MOBA_TOPK):
                    best = jnp.max(gate, axis=0, keepdims=True)
                    pick = blk == jnp.min(jnp.where(gate == best, blk, float(nk)), axis=0, keepdims=True)
                    picked = jnp.where(pick, 1.0, picked)
                    gate = jnp.where(pick, -jnp.inf, gate)
                selb_ref[m] = jnp.where(eligible, jnp.where(picked > 0.0, 0.0, MASK_VALUE), MASK_VALUE)

        def sweep(extra):
            if qs == 0:
                @pl.when(qi == 0)
                def _():
                    for j in range(nk):
                        kb = k_ref[0, j * MOBA_BLOCK:(j + 1) * MOBA_BLOCK, :].astype(F32)
                        kmean_ref[j:j + 1, :] = jnp.mean(kb, axis=0, keepdims=True)

                _split_pairs(qT_ref, qs, qz)
                first = [select_blocks]
            else:
                first = [functools.partial(_split_pairs, qT_ref, qs, qz), select_blocks]
            _sweep(qi, k_ref, vT_ref, lambda v, m: v[(m % 2) * head_rows:(m % 2 + 1) * head_rows],
                   lambda m, t, j: (bias_ref[m, t], None if t == 0 else selb_ref[m, pl.ds(j, 1), :]),
                   lambda m, j: (None, selb_ref[m, pl.ds(j, 1), :]),
                   (qz, m_run, acc), pre_ref, with_first_block=first + extra, may_be_first_tile=qs == 0)

        def finish():
            for g in range(PAIRS_PER_STEP):
                o = jnp.concatenate([acc[m, :HEAD_DIM, :] * (1.0 / acc[m, HEAD_DIM:HEAD_DIM + 1, :])
                                     for m in (2 * g, 2 * g + 1)], axis=0)
                o_ref[0, qs * ATT_TILE:(qs + 1) * ATT_TILE, g * PAIR:(g + 1) * PAIR] = o.T.astype(BF16)

        return sweep, finish

    _query_tiles(tile)


def _attention(kind, qT, k, vT, bias, extra, *, lam_init=None):
    B, nk, D, t = qT.shape
    G, S, T = PAIRS_PER_STEP, nk * t, Q_TILES_PER_STEP
    n_maps = 2 * G
    assert D % (G * PAIR) == 0 and nk >= 2 and nk % T == 0 and t == ATT_TILE and QK_AHEAD_PAIR <= QK_AHEAD <= n_maps
    in_specs = [pl.BlockSpec((1, T, G * PAIR, t), lambda b, p, q: (b, q, p, 0)),
                pl.BlockSpec((1, S, G * PAIR), lambda b, p, q: (b, 0, p)),
                pl.BlockSpec((1, G, nk, vT.shape[3], t), lambda b, p, q: (b, p, 0, 0, 0)),
                pl.BlockSpec((n_maps, 2, t, t), lambda b, p, q: (p, 0, 0, 0), pipeline_mode=pl.Buffered(1))]
    scratch = [pltpu.VMEM((T, n_maps, PAIR, t), BF16),
               pltpu.VMEM((T, n_maps, 1, t), F32),
               pltpu.VMEM((max(QK_AHEAD, QK_AHEAD_PAIR), 2, t, t), F32)]
    if kind == "diff":
        body = functools.partial(_diff_kernel, lam_init=lam_init)
        in_specs += [pl.BlockSpec(e.shape, lambda b, p, q: (0, 0)) for e in extra]
        scratch += [pltpu.VMEM((T, n_maps, PAIR + ONES_ROWS, t), F32)]
    else:
        body = functools.partial(_moba_kernel, nk=nk)
        scratch += [pltpu.VMEM((T, n_maps, HEAD_DIM + ONES_ROWS, t), F32),
                    pltpu.VMEM((nk, G * PAIR), F32),
                    pltpu.VMEM((n_maps, nk, t), F32)]
    return pl.pallas_call(
        body,
        grid=(B, D // (G * PAIR), nk // T),
        in_specs=in_specs,
        out_specs=pl.BlockSpec((1, T * t, G * PAIR), lambda b, p, q: (b, q, p)),
        out_shape=jax.ShapeDtypeStruct((B, S, D), BF16),
        scratch_shapes=scratch,
        compiler_params=pltpu.CompilerParams(dimension_semantics=("parallel", "parallel", "arbitrary"),
                                             vmem_limit_bytes=VMEM_LIMIT_BYTES),
        name=kind + "_attention",
    )(qT, k, vT, bias, *extra)


def kernel(x, rel_bias, norm_g, final_norm_g, ffn_w_in, ffn_w_out, diff_w_qkv, diff_lambda,
           diff_subln_g, diff_w_o, moba_w_qkv, moba_w_o):
    B, S, D = x.shape
    depth = norm_g.shape[0]
    assert S % TOKEN_TILE == 0 and D % PAIR == 0 and MOBA_BLOCK == ATT_TILE
    assert rel_bias.shape == (REL_BUCKETS, D // HEAD_DIM)
    bias = _bias_tiles(rel_bias, ATT_TILE)

    h = x.reshape(B * S, D)
    for i in range(depth):
        g = norm_g[i]
        h = _ffn(h, g[0], ffn_w_in[i, 0], ffn_w_out[i, 0])
        j = i // 2
        if i % 2 == 0:
            qT, k, vT = _qkv_proj(h.reshape(B, S, D), g[1], diff_w_qkv[j], PAIR)
            lam_init = 0.8 - 0.6 * math.exp(-0.3 * i)
            g_sub = jnp.broadcast_to((diff_subln_g[j].astype(F32) * (1.0 - lam_init))[:, None], (PAIR, ATT_TILE))
            o = _attention("diff", qT, k, vT, bias, (diff_lambda[j].astype(F32), g_sub), lam_init=lam_init)
            w_o = diff_w_o[j]
        else:
            qT, k, vT = _qkv_proj(h.reshape(B, S, D), g[1], moba_w_qkv[j], HEAD_DIM)
            o = _attention("moba", qT, k, vT, bias, ())
            w_o = moba_w_o[j]
        last = i == depth - 1
        h = _ffn(h, g[2], ffn_w_in[i, 1], ffn_w_out[i, 1], proj=(o.reshape(B * S, D), w_o),
                 final_g=final_norm_g if last else None)
    return h.reshape(B, S, D)
```

```python
import functools
import math

import numpy as np
import jax
import jax.numpy as jnp
from jax import lax
from jax.experimental import pallas as pl
from jax.experimental.pallas import tpu as pltpu

HEAD_DIM = 64
PAIR = 2 * HEAD_DIM
MOBA_BLOCK = 256
MOBA_TOPK = 3
KMEAN_TERMS = 3
ONES_ROWS = 16
REL_BUCKETS = 32
REL_MAX_DIST = 128
FFN_RESIDUAL = 0.5
RMS_EPS = 1e-6
SUBLN_EPS = 1e-5

ATT_TILE = 256
PAIRS_PER_STEP = 8
Q_TILES_PER_STEP = 2
QK_AHEAD = 4
QK_AHEAD_PAIR = 2
FFN_CHUNK = 256
TOKEN_TILE = 512
FFN_TOKEN_TILE = 1024
MASK_VALUE = -1e30
LOG2E = math.log2(math.e)
VMEM_LIMIT_BYTES = 62 * 1024 * 1024

F32 = jnp.float32
BF16 = jnp.bfloat16
_NT = (((1,), (1,)), ((), ()))


def _rms(x, g, eps):
    return x * lax.rsqrt(jnp.mean(x * x, axis=-1, keepdims=True) + eps) * g


def _const_spec(shape):
    return pl.BlockSpec(shape, lambda *_: (0,) * len(shape), pipeline_mode=pl.Buffered(1))


def _rel_bucket_np(dist):
    n = np.maximum(dist, 0)
    max_exact = REL_BUCKETS // 2
    nf = np.maximum(n, 1).astype(np.float32)
    large = max_exact + (np.log(nf / np.float32(max_exact)) / np.float32(math.log(REL_MAX_DIST / max_exact))
                         * np.float32(REL_BUCKETS - max_exact)).astype(np.int32)
    large = np.minimum(large, REL_BUCKETS - 1)
    return np.where(n < max_exact, n, large).astype(np.int32)


def _bucket_tiles(t):
    j = np.arange(t)[:, None]
    i = np.arange(t)[None, :]
    diag = np.where(i - j >= 0, _rel_bucket_np(i - j), -1)
    prev = _rel_bucket_np(i - j + t)
    assert _rel_bucket_np(np.arange(t + 1, 8 * t)).min() == REL_BUCKETS - 1
    return np.stack([diag, prev]).astype(np.int32)


def _bias_kernel(rb_ref, idx_ref, out_ref):
    m = pl.program_id(0)
    far = rb_ref[REL_BUCKETS - 1, m]
    for t in range(2):
        idx = idx_ref[t]
        acc = jnp.zeros(idx.shape, F32)
        for b in range(REL_BUCKETS - 1):
            acc = jnp.where(idx == b, (rb_ref[b, m] - far) * LOG2E, acc)
        out_ref[0, t] = jnp.where(idx < 0, MASK_VALUE, acc)


def _bias_tiles(rel_bias, t):
    n_maps = rel_bias.shape[1]
    idx = jnp.asarray(_bucket_tiles(t))
    return pl.pallas_call(
        _bias_kernel,
        grid=(n_maps,),
        in_specs=[pl.BlockSpec(memory_space=pltpu.SMEM),
                  pl.BlockSpec((2, t, t), lambda m: (0, 0, 0))],
        out_specs=pl.BlockSpec((1, 2, t, t), lambda m: (m, 0, 0, 0)),
        out_shape=jax.ShapeDtypeStruct((n_maps, 2, t, t), F32),
        name="rel_bias_tiles",
    )(rel_bias.astype(F32), idx)


def _ffn_kernel(*refs, n_chunks, has_proj, has_final):
    refs = list(refs)
    x_ref = refs.pop(0)
    if has_proj:
        o_ref, wo_ref = refs.pop(0), refs.pop(0)
    g_ref, win_ref, wout_ref = refs.pop(0), refs.pop(0), refs.pop(0)
    if has_final:
        gf_ref = refs.pop(0)
    (out_ref,) = refs

    x = x_ref[...]
    if has_proj:
        x = x + jnp.dot(o_ref[...], wo_ref[...], preferred_element_type=F32)
    hn = _rms(x, g_ref[...], RMS_EPS).astype(BF16)
    acc = None
    d_ff = n_chunks * FFN_CHUNK
    for c in range(n_chunks):
        cols = slice(c * FFN_CHUNK, (c + 1) * FFN_CHUNK)
        gate = jnp.dot(hn, win_ref[:, cols], preferred_element_type=F32)
        up = jnp.dot(hn, win_ref[:, d_ff + c * FFN_CHUNK:d_ff + (c + 1) * FFN_CHUNK], preferred_element_type=F32)
        a = (gate * (1.0 / (1.0 + jnp.exp(-gate))) * up).astype(BF16)
        part = jnp.dot(a, wout_ref[cols, :], preferred_element_type=F32)
        acc = part if acc is None else acc + part
    y = x + FFN_RESIDUAL * acc
    if has_final:
        y = _rms(y, gf_ref[...], RMS_EPS)
    out_ref[...] = y


def _ffn(x2d, g, w_in, w_out, proj=None, final_g=None):
    T, D = x2d.shape
    F = w_out.shape[0]
    n_chunks = F // FFN_CHUNK
    assert n_chunks * FFN_CHUNK == F
    tm = math.gcd(T, FFN_TOKEN_TILE)
    w_in_c = w_in.astype(BF16)
    w_out_c = w_out.astype(BF16)

    row = lambda i: (i, 0)
    args, specs = [x2d], [pl.BlockSpec((tm, D), row)]
    if proj is not None:
        o2d, w_o = proj
        args += [o2d, w_o.astype(BF16)]
        specs += [pl.BlockSpec((tm, o2d.shape[1]), row), _const_spec(w_o.shape)]
    args += [g.reshape(1, D).astype(F32), w_in_c, w_out_c]
    specs += [_const_spec((1, D)), _const_spec(w_in_c.shape), _const_spec(w_out_c.shape)]
    if final_g is not None:
        args.append(final_g.reshape(1, D).astype(F32))
        specs.append(_const_spec((1, D)))

    return pl.pallas_call(
        functools.partial(_ffn_kernel, n_chunks=n_chunks, has_proj=proj is not None,
                          has_final=final_g is not None),
        grid=(T // tm,),
        in_specs=specs,
        out_specs=pl.BlockSpec((tm, D), row),
        out_shape=jax.ShapeDtypeStruct((T, D), F32),
        compiler_params=pltpu.CompilerParams(dimension_semantics=("parallel",),
                                             vmem_limit_bytes=VMEM_LIMIT_BYTES),
        name="ffn",
    )(*args)


def _proj_kernel(x_ref, g_ref, wqT_ref, wk_ref, wvT_ref, qT_ref, k_ref, vT_ref, *, n_pairs, n_sub, v_dim):
    hn = _rms(x_ref[0], g_ref[...], RMS_EPS).astype(BF16)
    k_ref[0] = jnp.dot(hn, wk_ref[...], preferred_element_type=F32).astype(BF16)
    qT = lax.dot_general(wqT_ref[...], hn, _NT, preferred_element_type=F32).astype(BF16)
    for c in range(n_sub):
        qT_ref[0, c] = qT[:, c * ATT_TILE:(c + 1) * ATT_TILE]
    vT = lax.dot_general(wvT_ref[...], hn, _NT, preferred_element_type=F32).astype(BF16)
    ones_rows = jnp.where(lax.broadcasted_iota(jnp.int32, (ONES_ROWS, ATT_TILE), 0) == 0, 1.0, 0.0).astype(BF16)
    group = v_dim + ONES_ROWS
    for p in range(n_pairs):
        for c in range(n_sub):
            cols = slice(c * ATT_TILE, (c + 1) * ATT_TILE)
            for i in range(PAIR // v_dim):
                vT_ref[0, p, c, i * group:i * group + v_dim, :] = vT[p * PAIR + i * v_dim:p * PAIR + (i + 1) * v_dim, cols]
                vT_ref[0, p, c, i * group + v_dim:(i + 1) * group, :] = ones_rows


def _v_rows(v_dim):
    return (PAIR // v_dim) * (v_dim + ONES_ROWS)


def _qkv_proj(h, g, w_qkv, v_dim):
    B, S, D = h.shape
    tm = TOKEN_TILE
    n_pairs, n_sub, nk, rows = D // PAIR, tm // ATT_TILE, S // ATT_TILE, _v_rows(v_dim)
    wq, wk, wv = w_qkv[:, :D], w_qkv[:, D:2 * D], w_qkv[:, 2 * D:]
    wqT = (wq * (HEAD_DIM ** -0.5 * LOG2E)).T.astype(BF16)
    wvT = wv.T.astype(BF16)
    return pl.pallas_call(
        functools.partial(_proj_kernel, n_pairs=n_pairs, n_sub=n_sub, v_dim=v_dim),
        grid=(B, S // tm),
        in_specs=[pl.BlockSpec((1, tm, D), lambda b, s: (b, s, 0)),
                  _const_spec((1, D)), _const_spec((D, D)), _const_spec((D, D)), _const_spec((D, D))],
        out_specs=[pl.BlockSpec((1, n_sub, D, ATT_TILE), lambda b, s: (b, s, 0, 0)),
                   pl.BlockSpec((1, tm, D), lambda b, s: (b, s, 0)),
                   pl.BlockSpec((1, n_pairs, n_sub, rows, ATT_TILE), lambda b, s: (b, 0, s, 0, 0))],
        out_shape=[jax.ShapeDtypeStruct((B, nk, D, ATT_TILE), BF16),
                   jax.ShapeDtypeStruct((B, S, D), BF16),
                   jax.ShapeDtypeStruct((B, n_pairs, nk, rows, ATT_TILE), BF16)],
        compiler_params=pltpu.CompilerParams(dimension_semantics=("parallel", "parallel"),
                                             vmem_limit_bytes=VMEM_LIMIT_BYTES),
        name="qkv_proj",
    )(h, g.reshape(1, D).astype(F32), wqT, wk.astype(BF16), wvT)


def _split_pairs(qT_ref, qs, qz_ref):
    zeros = jnp.zeros((HEAD_DIM, qT_ref.shape[3]), qT_ref.dtype)
    for g in range(PAIRS_PER_STEP):
        q = qT_ref[0, qs, g * PAIR:(g + 1) * PAIR, :]
        qz_ref[2 * g, :HEAD_DIM, :] = q[:HEAD_DIM]
        qz_ref[2 * g, HEAD_DIM:, :] = zeros
        qz_ref[2 * g + 1, :HEAD_DIM, :] = zeros
        qz_ref[2 * g + 1, HEAD_DIM:, :] = q[HEAD_DIM:]


def _update(m, scores, vT_tiles, biases, m_ref, acc_ref, first):
    scores = [s if b is None else b + s for s, (b, _) in zip(scores, biases)]
    masks = [r for _, r in biases]
    if all(r is None for r in masks):
        m_cur = jnp.max(functools.reduce(jnp.maximum, scores), axis=0, keepdims=True)
    else:
        m_cur = functools.reduce(jnp.maximum, [jnp.max(s, axis=0, keepdims=True) + (0.0 if r is None else r)
                                               for s, r in zip(scores, masks)])
    if first:
        m_new = m_cur
    else:
        m_old = m_ref[m]
        m_new = jnp.maximum(m_old, m_cur)
    shifts = [m_new if r is None else m_new - r for r in masks]
    pv = functools.reduce(jnp.add, [jnp.dot(v, jnp.exp2(s - sh).astype(BF16), preferred_element_type=F32)
                                    for v, s, sh in zip(vT_tiles, scores, shifts)])
    if first:
        acc_ref[m] = pv
    else:
        acc_ref[m] = jnp.exp2(m_old - m_new) * acc_ref[m] + pv
    m_ref[m] = m_new


def _sweep(qi, qi_is_even, k_ref, vT_ref, v_rows, near_bias, far_bias, state, pre_ref, with_first_block=()):
    qz_ref, m_ref, acc_ref = state
    n_maps = 2 * PAIRS_PER_STEP

    def raw_scores(j, m):
        rows = pl.ds(pl.multiple_of(j * ATT_TILE, ATT_TILE), ATT_TILE)
        k_t = k_ref[0, rows, (m // 2) * PAIR:(m // 2 + 1) * PAIR]
        return jnp.dot(k_t, qz_ref[m], preferred_element_type=F32)

    def block(tiles, biases, first, ahead, preloaded, next_tiles, next_ahead):
        scores = {}
        for m in range(ahead):
            scores[m] = ([pre_ref[m, t] for t in range(len(tiles))] if preloaded
                         else [raw_scores(j, m) for j in tiles])
        for m in range(n_maps):
            nxt = m + ahead
            if nxt < n_maps:
                scores[nxt] = [raw_scores(j, nxt) for j in tiles]
            elif nxt - n_maps < next_ahead:
                for t, j in enumerate(next_tiles):
                    pre_ref[nxt - n_maps, t] = raw_scores(j, nxt - n_maps)
            _update(m, scores.pop(m), [v_rows(vT_ref[0, m // 2, j], m) for j in tiles], biases(m),
                    m_ref, acc_ref, first)

    n_first = 3 if qi_is_even else 2
    n_pairs = jnp.maximum(qi + 1 - n_first, 0) // 2

    def first_block():
        for traced_alongside in with_first_block:
            traced_alongside()
        tiles = [qi - 2, qi - 1, qi][-n_first:]
        block(tiles, lambda m: ([far_bias(m, qi - 2)] if qi_is_even else [])
              + [near_bias(m, 1, qi - 1), near_bias(m, 0, qi)], True, QK_AHEAD_PAIR, False, [0, 1], QK_AHEAD_PAIR)

    if qi_is_even:
        @pl.when(qi == 0)
        def _():
            block([qi], lambda m: [near_bias(m, 0, qi)], True, QK_AHEAD, False, [], 0)

        pl.when(qi >= 2)(first_block)
    else:
        first_block()

    def far_pair(p, carry):
        j, jn = 2 * p, 2 * jnp.minimum(p + 1, n_pairs - 1)
        block([j, j + 1], lambda m: [far_bias(m, j), far_bias(m, j + 1)], False, QK_AHEAD_PAIR, True,
              [jn, jn + 1], QK_AHEAD_PAIR)
        return carry

    lax.fori_loop(0, n_pairs, far_pair, 0)


def _query_tiles(step_body):
    first = pl.program_id(2) * Q_TILES_PER_STEP
    pending = []
    for qs in range(Q_TILES_PER_STEP):
        sweep, finish = step_body(qs, first + qs)
        sweep(pending)
        pending = [finish]
    pending[0]()


def _diff_kernel(qT_ref, k_ref, vT_ref, bias_ref, lam_ref, g_ref, o_ref,
                 qz_ref, m_ref, pre_ref, acc_ref, *, lam_init):
    def tile(qs, qi):
        qz, m_run, acc = qz_ref.at[qs], m_ref.at[qs], acc_ref.at[qs]

        def sweep(extra):
            if qs % 2 == 0:
                _split_pairs(qT_ref, qs, qz)
                first = []
            else:
                first = [functools.partial(_split_pairs, qT_ref, qs, qz)]
            _sweep(qi, qs % 2 == 0, k_ref, vT_ref, lambda v, m: v,
                   lambda m, t, j: (bias_ref[m, t], None), lambda m, j: (None, None),
                   (qz, m_run, acc), pre_ref, with_first_block=first + extra)

        def finish():
            lp = lam_ref[...]
            lam = (jnp.exp(jnp.sum(lp[0:1] * lp[1:2], axis=-1, keepdims=True))
                   - jnp.exp(jnp.sum(lp[2:3] * lp[3:4], axis=-1, keepdims=True)) + lam_init)

            def weighted(m, scale):
                return acc[m, :PAIR, :] * (scale / acc[m, PAIR:PAIR + 1, :])

            for g in range(PAIRS_PER_STEP):
                o = weighted(2 * g, 1.0) - weighted(2 * g + 1, lam)
                o = o * lax.rsqrt(jnp.mean(o * o, axis=0, keepdims=True) + SUBLN_EPS) * g_ref[...]
                o_ref[0, qs * ATT_TILE:(qs + 1) * ATT_TILE, g * PAIR:(g + 1) * PAIR] = o.T.astype(BF16)

        return sweep, finish

    _query_tiles(tile)


def _moba_kernel(qT_ref, k_ref, vT_ref, bias_ref, o_ref,
                 qz_ref, m_ref, pre_ref, acc_ref, kmean_ref, selb_ref, *, nk):
    tq = qT_ref.shape[3]
    head_rows = HEAD_DIM + ONES_ROWS

    def tile(qs, qi):
        qz, m_run, acc = qz_ref.at[qs], m_ref.at[qs], acc_ref.at[qs]

        def select_blocks():
            blk = lax.broadcasted_iota(jnp.int32, (nk, tq), 0).astype(F32)
            eligible = blk < qi.astype(F32)
            for m in range(2 * PAIRS_PER_STEP):
                rest = kmean_ref[:, (m // 2) * PAIR:(m // 2 + 1) * PAIR]
                gate = jnp.zeros((nk, tq), F32)
                for _ in range(KMEAN_TERMS):
                    term = rest.astype(BF16)
                    gate = gate + jnp.dot(term, qz[m], preferred_element_type=F32)
                    rest = rest - term.astype(F32)
                gate = jnp.where(eligible, gate, -jnp.inf)
                picked = jnp.zeros((nk, tq), F32)
                for _ in range(MOBA_TOPK):
                    best = jnp.max(gate, axis=0, keepdims=True)
                    pick = blk == jnp.min(jnp.where(gate == best, blk, float(nk)), axis=0, keepdims=True)
                    picked = jnp.where(pick, 1.0, picked)
                    gate = jnp.where(pick, -jnp.inf, gate)
                selb_ref[m] = jnp.where(eligible, jnp.where(picked > 0.0, 0.0, MASK_VALUE), MASK_VALUE)

        def sweep(extra):
            if qs % 2 == 0:
                @pl.when(qi == 0)
                def _():
                    for j in range(nk):
                        kb = k_ref[0, j * MOBA_BLOCK:(j + 1) * MOBA_BLOCK, :].astype(F32)
                        kmean_ref[j:j + 1, :] = jnp.mean(kb, axis=0, keepdims=True)

                _split_pairs(qT_ref, qs, qz)
                first = [select_blocks]
            else:
                first = [functools.partial(_split_pairs, qT_ref, qs, qz), select_blocks]
            _sweep(qi, qs % 2 == 0, k_ref, vT_ref, lambda v, m: v[(m % 2) * head_rows:(m % 2 + 1) * head_rows],
                   lambda m, t, j: (bias_ref[m, t], None if t == 0 else selb_ref[m, pl.ds(j, 1), :]),
                   lambda m, j: (None, selb_ref[m, pl.ds(j, 1), :]),
                   (qz, m_run, acc), pre_ref, with_first_block=first + extra)

        def finish():
            for g in range(PAIRS_PER_STEP):
                o = jnp.concatenate([acc[m, :HEAD_DIM, :] * (1.0 / acc[m, HEAD_DIM:HEAD_DIM + 1, :])
                                     for m in (2 * g, 2 * g + 1)], axis=0)
                o_ref[0, qs * ATT_TILE:(qs + 1) * ATT_TILE, g * PAIR:(g + 1) * PAIR] = o.T.astype(BF16)

        return sweep, finish

    _query_tiles(tile)


def _attention(kind, qT, k, vT, bias, extra, *, lam_init=None):
    B, nk, D, t = qT.shape
    G, S, T = PAIRS_PER_STEP, nk * t, Q_TILES_PER_STEP
    n_maps = 2 * G
    assert D % (G * PAIR) == 0 and nk >= 2 and nk % T == 0 and T % 2 == 0 and t == ATT_TILE and QK_AHEAD_PAIR <= QK_AHEAD <= n_maps
    in_specs = [pl.BlockSpec((1, T, G * PAIR, t), lambda b, p, q: (b, q, p, 0)),
                pl.BlockSpec((1, S, G * PAIR), lambda b, p, q: (b, 0, p)),
                pl.BlockSpec((1, G, nk, vT.shape[3], t), lambda b, p, q: (b, p, 0, 0, 0)),
                pl.BlockSpec((n_maps, 2, t, t), lambda b, p, q: (p, 0, 0, 0), pipeline_mode=pl.Buffered(1))]
    scratch = [pltpu.VMEM((T, n_maps, PAIR, t), BF16),
               pltpu.VMEM((T, n_maps, 1, t), F32),
               pltpu.VMEM((max(QK_AHEAD, QK_AHEAD_PAIR), 2, t, t), F32)]
    if kind == "diff":
        body = functools.partial(_diff_kernel, lam_init=lam_init)
        in_specs += [pl.BlockSpec(e.shape, lambda b, p, q: (0, 0)) for e in extra]
        scratch += [pltpu.VMEM((T, n_maps, PAIR + ONES_ROWS, t), F32)]
    else:
        body = functools.partial(_moba_kernel, nk=nk)
        scratch += [pltpu.VMEM((T, n_maps, HEAD_DIM + ONES_ROWS, t), F32),
                    pltpu.VMEM((nk, G * PAIR), F32),
                    pltpu.VMEM((n_maps, nk, t), F32)]
    return pl.pallas_call(
        body,
        grid=(B, D // (G * PAIR), nk // T),
        in_specs=in_specs,
        out_specs=pl.BlockSpec((1, T * t, G * PAIR), lambda b, p, q: (b, q, p)),
        out_shape=jax.ShapeDtypeStruct((B, S, D), BF16),
        scratch_shapes=scratch,
        compiler_params=pltpu.CompilerParams(dimension_semantics=("parallel", "parallel", "arbitrary"),
                                             vmem_limit_bytes=VMEM_LIMIT_BYTES),
        name=kind + "_attention",
    )(qT, k, vT, bias, *extra)


def kernel(x, rel_bias, norm_g, final_norm_g, ffn_w_in, ffn_w_out, diff_w_qkv, diff_lambda,
           diff_subln_g, diff_w_o, moba_w_qkv, moba_w_o):
    B, S, D = x.shape
    depth = norm_g.shape[0]
    assert S % TOKEN_TILE == 0 and D % PAIR == 0 and MOBA_BLOCK == ATT_TILE
    assert rel_bias.shape == (REL_BUCKETS, D // HEAD_DIM)
    bias = _bias_tiles(rel_bias, ATT_TILE)

    h = x.reshape(B * S, D)
    for i in range(depth):
        g = norm_g[i]
        h = _ffn(h, g[0], ffn_w_in[i, 0], ffn_w_out[i, 0])
        j = i // 2
        if i % 2 == 0:
            qT, k, vT = _qkv_proj(h.reshape(B, S, D), g[1], diff_w_qkv[j], PAIR)
            lam_init = 0.8 - 0.6 * math.exp(-0.3 * i)
            g_sub = jnp.broadcast_to((diff_subln_g[j].astype(F32) * (1.0 - lam_init))[:, None], (PAIR, ATT_TILE))
            o = _attention("diff", qT, k, vT, bias, (diff_lambda[j].astype(F32), g_sub), lam_init=lam_init)
            w_o = diff_w_o[j]
        else:
            qT, k, vT = _qkv_proj(h.reshape(B, S, D), g[1], moba_w_qkv[j], HEAD_DIM)
            o = _attention("moba", qT, k, vT, bias, ())
            w_o = moba_w_o[j]
        last = i == depth - 1
        h = _ffn(h, g[2], ffn_w_in[i, 1], ffn_w_out[i, 1], proj=(o.reshape(B * S, D), w_o),
                 final_g=final_norm_g if last else None)
    return h.reshape(B, S, D)
```

```python
import functools
import math

import numpy as np
import jax
import jax.numpy as jnp
from jax import lax
from jax.experimental import pallas as pl
from jax.experimental.pallas import tpu as pltpu

HEAD_DIM = 64
PAIR = 2 * HEAD_DIM
MOBA_BLOCK = 256
MOBA_TOPK = 3
KMEAN_TERMS = 3
ONES_ROWS = 16
REL_BUCKETS = 32
REL_MAX_DIST = 128
FFN_RESIDUAL = 0.5
RMS_EPS = 1e-6
SUBLN_EPS = 1e-5

ATT_TILE = 256
PAIRS_PER_STEP = 8
Q_TILES_PER_STEP = 2
QK_AHEAD = 4
QK_AHEAD_PAIR = 2
FFN_CHUNK = 256
TOKEN_TILE = 512
FFN_TOKEN_TILE = 1024
MASK_VALUE = -1e30
LOG2E = math.log2(math.e)
VMEM_LIMIT_BYTES = 62 * 1024 * 1024

F32 = jnp.float32
BF16 = jnp.bfloat16
_NT = (((1,), (1,)), ((), ()))


def _rms(x, g, eps):
    return x * lax.rsqrt(jnp.mean(x * x, axis=-1, keepdims=True) + eps) * g


def _const_spec(shape):
    return pl.BlockSpec(shape, lambda *_: (0,) * len(shape), pipeline_mode=pl.Buffered(1))


def _rel_bucket_np(dist):
    n = np.maximum(dist, 0)
    max_exact = REL_BUCKETS // 2
    nf = np.maximum(n, 1).astype(np.float32)
    large = max_exact + (np.log(nf / np.float32(max_exact)) / np.float32(math.log(REL_MAX_DIST / max_exact))
                         * np.float32(REL_BUCKETS - max_exact)).astype(np.int32)
    large = np.minimum(large, REL_BUCKETS - 1)
    return np.where(n < max_exact, n, large).astype(np.int32)


def _bucket_tiles(t):
    j = np.arange(t)[:, None]
    i = np.arange(t)[None, :]
    diag = np.where(i - j >= 0, _rel_bucket_np(i - j), -1)
    prev = _rel_bucket_np(i - j + t)
    assert _rel_bucket_np(np.arange(t + 1, 8 * t)).min() == REL_BUCKETS - 1
    return np.stack([diag, prev]).astype(np.int32)


def _bias_kernel(rb_ref, idx_ref, out_ref):
    m = pl.program_id(0)
    far = rb_ref[REL_BUCKETS - 1, m]
    for t in range(2):
        idx = idx_ref[t]
        acc = jnp.zeros(idx.shape, F32)
        for b in range(REL_BUCKETS - 1):
            acc = jnp.where(idx == b, (rb_ref[b, m] - far) * LOG2E, acc)
        out_ref[0, t] = jnp.where(idx < 0, MASK_VALUE, acc)


def _bias_tiles(rel_bias, t):
    n_maps = rel_bias.shape[1]
    idx = jnp.asarray(_bucket_tiles(t))
    return pl.pallas_call(
        _bias_kernel,
        grid=(n_maps,),
        in_specs=[pl.BlockSpec(memory_space=pltpu.SMEM),
                  pl.BlockSpec((2, t, t), lambda m: (0, 0, 0))],
        out_specs=pl.BlockSpec((1, 2, t, t), lambda m: (m, 0, 0, 0)),
        out_shape=jax.ShapeDtypeStruct((n_maps, 2, t, t), F32),
        name="rel_bias_tiles",
    )(rel_bias.astype(F32), idx)


def _ffn_kernel(*refs, n_chunks, has_proj, has_final):
    refs = list(refs)
    x_ref = refs.pop(0)
    if has_proj:
        o_ref, wo_ref = refs.pop(0), refs.pop(0)
    g_ref, win_ref, wout_ref = refs.pop(0), refs.pop(0), refs.pop(0)
    if has_final:
        gf_ref = refs.pop(0)
    (out_ref,) = refs

    x = x_ref[...]
    if has_proj:
        x = x + jnp.dot(o_ref[...], wo_ref[...], preferred_element_type=F32)
    hn = _rms(x, g_ref[...], RMS_EPS).astype(BF16)
    acc = None
    d_ff = n_chunks * FFN_CHUNK
    for c in range(n_chunks):
        cols = slice(c * FFN_CHUNK, (c + 1) * FFN_CHUNK)
        gate = jnp.dot(hn, win_ref[:, cols], preferred_element_type=F32)
        up = jnp.dot(hn, win_ref[:, d_ff + c * FFN_CHUNK:d_ff + (c + 1) * FFN_CHUNK], preferred_element_type=F32)
        a = (gate * (1.0 / (1.0 + jnp.exp(-gate))) * up).astype(BF16)
        part = jnp.dot(a, wout_ref[cols, :], preferred_element_type=F32)
        acc = part if acc is None else acc + part
    y = x + FFN_RESIDUAL * acc
    if has_final:
        y = _rms(y, gf_ref[...], RMS_EPS)
    out_ref[...] = y


def _ffn(x2d, g, w_in, w_out, proj=None, final_g=None):
    T, D = x2d.shape
    F = w_out.shape[0]
    n_chunks = F // FFN_CHUNK
    assert n_chunks * FFN_CHUNK == F
    tm = math.gcd(T, FFN_TOKEN_TILE)
    w_in_c = w_in.astype(BF16)
    w_out_c = w_out.astype(BF16)

    row = lambda i: (i, 0)
    args, specs = [x2d], [pl.BlockSpec((tm, D), row)]
    if proj is not None:
        o2d, w_o = proj
        args += [o2d, w_o.astype(BF16)]
        specs += [pl.BlockSpec((tm, o2d.shape[1]), row), _const_spec(w_o.shape)]
    args += [g.reshape(1, D).astype(F32), w_in_c, w_out_c]
    specs += [_const_spec((1, D)), _const_spec(w_in_c.shape), _const_spec(w_out_c.shape)]
    if final_g is not None:
        args.append(final_g.reshape(1, D).astype(F32))
        specs.append(_const_spec((1, D)))

    return pl.pallas_call(
        functools.partial(_ffn_kernel, n_chunks=n_chunks, has_proj=proj is not None,
                          has_final=final_g is not None),
        grid=(T // tm,),
        in_specs=specs,
        out_specs=pl.BlockSpec((tm, D), row),
        out_shape=jax.ShapeDtypeStruct((T, D), F32),
        compiler_params=pltpu.CompilerParams(dimension_semantics=("parallel",),
                                             vmem_limit_bytes=VMEM_LIMIT_BYTES),
        name="ffn",
    )(*args)


def _proj_kernel(x_ref, g_ref, wqT_ref, wk_ref, wvT_ref, qT_ref, k_ref, vT_ref, *, n_pairs, n_sub, v_dim):
    hn = _rms(x_ref[0], g_ref[...], RMS_EPS).astype(BF16)
    k_ref[0] = jnp.dot(hn, wk_ref[...], preferred_element_type=F32).astype(BF16)
    qT = lax.dot_general(wqT_ref[...], hn, _NT, preferred_element_type=F32).astype(BF16)
    for c in range(n_sub):
        qT_ref[0, c] = qT[:, c * ATT_TILE:(c + 1) * ATT_TILE]
    vT = lax.dot_general(wvT_ref[...], hn, _NT, preferred_element_type=F32).astype(BF16)
    ones_rows = jnp.where(lax.broadcasted_iota(jnp.int32, (ONES_ROWS, ATT_TILE), 0) == 0, 1.0, 0.0).astype(BF16)
    group = v_dim + ONES_ROWS
    for p in range(n_pairs):
        for c in range(n_sub):
            cols = slice(c * ATT_TILE, (c + 1) * ATT_TILE)
            for i in range(PAIR // v_dim):
                vT_ref[0, p, c, i * group:i * group + v_dim, :] = vT[p * PAIR + i * v_dim:p * PAIR + (i + 1) * v_dim, cols]
                vT_ref[0, p, c, i * group + v_dim:(i + 1) * group, :] = ones_rows


def _v_rows(v_dim):
    return (PAIR // v_dim) * (v_dim + ONES_ROWS)


def _qkv_proj(h, g, w_qkv, v_dim):
    B, S, D = h.shape
    tm = TOKEN_TILE
    n_pairs, n_sub, nk, rows = D // PAIR, tm // ATT_TILE, S // ATT_TILE, _v_rows(v_dim)
    wq, wk, wv = w_qkv[:, :D], w_qkv[:, D:2 * D], w_qkv[:, 2 * D:]
    wqT = (wq * (HEAD_DIM ** -0.5 * LOG2E)).T.astype(BF16)
    wvT = wv.T.astype(BF16)
    return pl.pallas_call(
        functools.partial(_proj_kernel, n_pairs=n_pairs, n_sub=n_sub, v_dim=v_dim),
        grid=(B, S // tm),
        in_specs=[pl.BlockSpec((1, tm, D), lambda b, s: (b, s, 0)),
                  _const_spec((1, D)), _const_spec((D, D)), _const_spec((D, D)), _const_spec((D, D))],
        out_specs=[pl.BlockSpec((1, n_sub, D, ATT_TILE), lambda b, s: (b, s, 0, 0)),
                   pl.BlockSpec((1, tm, D), lambda b, s: (b, s, 0)),
                   pl.BlockSpec((1, n_pairs, n_sub, rows, ATT_TILE), lambda b, s: (b, 0, s, 0, 0))],
        out_shape=[jax.ShapeDtypeStruct((B, nk, D, ATT_TILE), BF16),
                   jax.ShapeDtypeStruct((B, S, D), BF16),
                   jax.ShapeDtypeStruct((B, n_pairs, nk, rows, ATT_TILE), BF16)],
        compiler_params=pltpu.CompilerParams(dimension_semantics=("parallel", "parallel"),
                                             vmem_limit_bytes=VMEM_LIMIT_BYTES),
        name="qkv_proj",
    )(h, g.reshape(1, D).astype(F32), wqT, wk.astype(BF16), wvT)


def _split_pairs(qT_ref, qs, qz_ref):
    zeros = jnp.zeros((HEAD_DIM, qT_ref.shape[3]), qT_ref.dtype)
    for g in range(PAIRS_PER_STEP):
        q = qT_ref[0, qs, g * PAIR:(g + 1) * PAIR, :]
        qz_ref[2 * g, :HEAD_DIM, :] = q[:HEAD_DIM]
        qz_ref[2 * g, HEAD_DIM:, :] = zeros
        qz_ref[2 * g + 1, :HEAD_DIM, :] = zeros
        qz_ref[2 * g + 1, HEAD_DIM:, :] = q[HEAD_DIM:]


def _update(m, scores, vT_tiles, biases, m_ref, acc_ref, first):
    scores = [s if b is None else b + s for s, (b, _) in zip(scores, biases)]
    masks = [r for _, r in biases]
    if all(r is None for r in masks):
        m_cur = jnp.max(functools.reduce(jnp.maximum, scores), axis=0, keepdims=True)
    else:
        m_cur = functools.reduce(jnp.maximum, [jnp.max(s, axis=0, keepdims=True) + (0.0 if r is None else r)
                                               for s, r in zip(scores, masks)])
    if first:
        m_new = m_cur
    else:
        m_old = m_ref[m]
        m_new = jnp.maximum(m_old, m_cur)
    shifts = [m_new if r is None else m_new - r for r in masks]
    pv = functools.reduce(jnp.add, [jnp.dot(v, jnp.exp2(s - sh).astype(BF16), preferred_element_type=F32)
                                    for v, s, sh in zip(vT_tiles, scores, shifts)])
    if first:
        acc_ref[m] = pv
    else:
        acc_ref[m] = jnp.exp2(m_old - m_new) * acc_ref[m] + pv
    m_ref[m] = m_new


def _sweep(qi, k_ref, vT_ref, v_rows, near_bias, far_bias, state, pre_ref, with_first_block=(), may_be_first_tile=True):
    qz_ref, m_ref, acc_ref = state
    n_maps = 2 * PAIRS_PER_STEP

    def raw_scores(j, m):
        rows = pl.ds(pl.multiple_of(j * ATT_TILE, ATT_TILE), ATT_TILE)
        k_t = k_ref[0, rows, (m // 2) * PAIR:(m // 2 + 1) * PAIR]
        return jnp.dot(k_t, qz_ref[m], preferred_element_type=F32)

    def block(tiles, biases, first, ahead, preloaded, next_tiles, next_ahead):
        scores = {}
        for m in range(ahead):
            scores[m] = ([pre_ref[m, t] for t in range(len(tiles))] if preloaded
                         else [raw_scores(j, m) for j in tiles])
        for m in range(n_maps):
            nxt = m + ahead
            if nxt < n_maps:
                scores[nxt] = [raw_scores(j, nxt) for j in tiles]
            elif nxt - n_maps < next_ahead:
                for t, j in enumerate(next_tiles):
                    pre_ref[nxt - n_maps, t] = raw_scores(j, nxt - n_maps)
            _update(m, scores.pop(m), [v_rows(vT_ref[0, m // 2, j], m) for j in tiles], biases(m),
                    m_ref, acc_ref, first)

    n_far = jnp.maximum(qi - 1, 0)
    n_pairs = n_far // 2

    def first_pair():
        for traced_alongside in with_first_block:
            traced_alongside()
        block([qi - 1, qi], lambda m: [near_bias(m, 1, qi - 1), near_bias(m, 0, qi)], True, QK_AHEAD_PAIR, False,
              [0, 1], QK_AHEAD_PAIR)

    if may_be_first_tile:
        @pl.when(qi == 0)
        def _():
            block([qi], lambda m: [near_bias(m, 0, qi)], True, QK_AHEAD, False, [], 0)

        pl.when(qi >= 1)(first_pair)
    else:
        first_pair()

    @pl.when(n_far % 2 == 1)
    def _():
        block([n_far - 1], lambda m: [far_bias(m, n_far - 1)], False, QK_AHEAD, False, [], 0)

    def far_pair(p, carry):
        j, jn = 2 * p, 2 * jnp.minimum(p + 1, n_pairs - 1)
        block([j, j + 1], lambda m: [far_bias(m, j), far_bias(m, j + 1)], False, QK_AHEAD_PAIR, True,
              [jn, jn + 1], QK_AHEAD_PAIR)
        return carry

    lax.fori_loop(0, n_pairs, far_pair, 0)


def _query_tiles(step_body):
    first = pl.program_id(2) * Q_TILES_PER_STEP
    pending = []
    for qs in range(Q_TILES_PER_STEP):
        sweep, finish = step_body(qs, first + qs)
        sweep(pending)
        pending = [finish]
    pending[0]()


def _diff_kernel(qT_ref, k_ref, vT_ref, bias_ref, lam_ref, g_ref, o_ref,
                 qz_ref, m_ref, pre_ref, acc_ref, *, lam_init):
    def tile(qs, qi):
        qz, m_run, acc = qz_ref.at[qs], m_ref.at[qs], acc_ref.at[qs]

        def sweep(extra):
            if qs == 0:
                _split_pairs(qT_ref, qs, qz)
                first = []
            else:
                first = [functools.partial(_split_pairs, qT_ref, qs, qz)]
            _sweep(qi, k_ref, vT_ref, lambda v, m: v,
                   lambda m, t, j: (bias_ref[m, t], None), lambda m, j: (None, None),
                   (qz, m_run, acc), pre_ref, with_first_block=first + extra, may_be_first_tile=qs == 0)

        def finish():
            lp = lam_ref[...]
            lam = (jnp.exp(jnp.sum(lp[0:1] * lp[1:2], axis=-1, keepdims=True))
                   - jnp.exp(jnp.sum(lp[2:3] * lp[3:4], axis=-1, keepdims=True)) + lam_init)

            def weighted(m, scale):
                return acc[m, :PAIR, :] * (scale / acc[m, PAIR:PAIR + 1, :])

            for g in range(PAIRS_PER_STEP):
                o = weighted(2 * g, 1.0) - weighted(2 * g + 1, lam)
                o = o * lax.rsqrt(jnp.mean(o * o, axis=0, keepdims=True) + SUBLN_EPS) * g_ref[...]
                o_ref[0, qs * ATT_TILE:(qs + 1) * ATT_TILE, g * PAIR:(g + 1) * PAIR] = o.T.astype(BF16)

        return sweep, finish

    _query_tiles(tile)


def _moba_kernel(qT_ref, k_ref, vT_ref, bias_ref, o_ref,
                 qz_ref, m_ref, pre_ref, acc_ref, kmean_ref, selb_ref, *, nk):
    tq = qT_ref.shape[3]
    head_rows = HEAD_DIM + ONES_ROWS

    def tile(qs, qi):
        qz, m_run, acc = qz_ref.at[qs], m_ref.at[qs], acc_ref.at[qs]

        def select_blocks():
            blk = lax.broadcasted_iota(jnp.int32, (nk, tq), 0).astype(F32)
            eligible = blk < qi.astype(F32)
            for m in range(2 * PAIRS_PER_STEP):
                rest = kmean_ref[:, (m // 2) * PAIR:(m // 2 + 1) * PAIR]
                gate = jnp.zeros((nk, tq), F32)
                for _ in range(KMEAN_TERMS):
                    term = rest.astype(BF16)
                    gate = gate + jnp.dot(term, qz[m], preferred_element_type=F32)
                    rest = rest - term.astype(F32)
                gate = jnp.where(eligible, gate, -jnp.inf)
                picked = jnp.zeros((nk, tq), F32)
                for _ in range(MOBA_TOPK):
                    best = jnp.max(gate, axis=0, keepdims=True)
                    pick = blk == jnp.min(jnp.where(gate == best, blk, float(nk)), axis=0, keepdims=True)
                    picked = jnp.where(pick, 1.0, picked)
                    gate = jnp.where(pick, -jnp.inf, gate)
                selb_ref[m] = jnp.where(eligible, jnp.where(picked > 0.0, 0.0, MASK_VALUE), MASK_VALUE)

        def sweep(extra):
            if qs == 0:
                @pl.when(qi == 0)
                def _():
                    for j in range(nk):
                        kb = k_ref[0, j * MOBA_BLOCK:(j + 1) * MOBA_BLOCK, :].astype(F32)
                        kmean_ref[j:j + 1, :] = jnp.mean(kb, axis=0, keepdims=True)

                _split_pairs(qT_ref, qs, qz)
                first = [select_blocks]
            else:
                first = [functools.partial(_split_pairs, qT_ref, qs, qz), select_blocks]
            _sweep(qi, k_ref, vT_ref, lambda v, m: v[(m % 2) * head_rows:(m % 2 + 1) * head_rows],
                   lambda m, t, j: (bias_ref[m, t], None if t == 0 else selb_ref[m, pl.ds(j, 1), :]),
                   lambda m, j: (None, selb_ref[m, pl.ds(j, 1), :]),
                   (qz, m_run, acc), pre_ref, with_first_block=first + extra, may_be_first_tile=qs == 0)

        def finish():
            for g in range(PAIRS_PER_STEP):
                o = jnp.concatenate([acc[m, :HEAD_DIM, :] * (1.0 / acc[m, HEAD_DIM:HEAD_DIM + 1, :])
                                     for m in (2 * g, 2 * g + 1)], axis=0)
                o_ref[0, qs * ATT_TILE:(qs + 1) * ATT_TILE, g * PAIR:(g + 1) * PAIR] = o.T.astype(BF16)

        return sweep, finish

    _query_tiles(tile)


def _attention(kind, qT, k, vT, bias, extra, *, lam_init=None):
    B, nk, D, t = qT.shape
    G, S, T = PAIRS_PER_STEP, nk * t, Q_TILES_PER_STEP
    n_maps = 2 * G
    assert D % (G * PAIR) == 0 and nk >= 2 and nk % T == 0 and t == ATT_TILE and QK_AHEAD_PAIR <= QK_AHEAD <= n_maps
    in_specs = [pl.BlockSpec((1, T, G * PAIR, t), lambda b, p, q: (b, q, p, 0)),
                pl.BlockSpec((1, S, G * PAIR), lambda b, p, q: (b, 0, p)),
                pl.BlockSpec((1, G, nk, vT.shape[3], t), lambda b, p, q: (b, p, 0, 0, 0)),
                pl.BlockSpec((n_maps, 2, t, t), lambda b, p, q: (p, 0, 0, 0), pipeline_mode=pl.Buffered(1))]
    scratch = [pltpu.VMEM((T, n_maps, PAIR, t), BF16),
               pltpu.VMEM((T, n_maps, 1, t), F32),
               pltpu.VMEM((max(QK_AHEAD, QK_AHEAD_PAIR), 2, t, t), F32)]
    if kind == "diff":
        body = functools.partial(_diff_kernel, lam_init=lam_init)
        in_specs += [pl.BlockSpec(e.shape, lambda b, p, q: (0, 0)) for e in extra]
        scratch += [pltpu.VMEM((T, n_maps, PAIR + ONES_ROWS, t), F32)]
    else:
        body = functools.partial(_moba_kernel, nk=nk)
        scratch += [pltpu.VMEM((T, n_maps, HEAD_DIM + ONES_ROWS, t), F32),
                    pltpu.VMEM((nk, G * PAIR), F32),
                    pltpu.VMEM((n_maps, nk, t), F32)]
    return pl.pallas_call(
        body,
        grid=(B, D // (G * PAIR), nk // T),
        in_specs=in_specs,
        out_specs=pl.BlockSpec((1, T * t, G * PAIR), lambda b, p, q: (b, q, p)),
        out_shape=jax.ShapeDtypeStruct((B, S, D), BF16),
        scratch_shapes=scratch,
        compiler_params=pltpu.CompilerParams(dimension_semantics=("parallel", "parallel", "arbitrary"),
                                             vmem_limit_bytes=VMEM_LIMIT_BYTES),
        name=kind + "_attention",
    )(qT, k, vT, bias, *extra)


def kernel(x, rel_bias, norm_g, final_norm_g, ffn_w_in, ffn_w_out, diff_w_qkv, diff_lambda,
           diff_subln_g, diff_w_o, moba_w_qkv, moba_w_o):
    B, S, D = x.shape
    depth = norm_g.shape[0]
    assert S % TOKEN_TILE == 0 and D % PAIR == 0 and MOBA_BLOCK == ATT_TILE
    assert rel_bias.shape == (REL_BUCKETS, D // HEAD_DIM)
    bias = _bias_tiles(rel_bias, ATT_TILE)

    h = x.reshape(B * S, D)
    for i in range(depth):
        g = norm_g[i]
        h = _ffn(h, g[0], ffn_w_in[i, 0], ffn_w_out[i, 0])
        j = i // 2
        if i % 2 == 0:
            qT, k, vT = _qkv_proj(h.reshape(B, S, D), g[1], diff_w_qkv[j], PAIR)
            lam_init = 0.8 - 0.6 * math.exp(-0.3 * i)
            g_sub = jnp.broadcast_to((diff_subln_g[j].astype(F32) * (1.0 - lam_init))[:, None], (PAIR, ATT_TILE))
            o = _attention("diff", qT, k, vT, bias, (diff_lambda[j].astype(F32), g_sub), lam_init=lam_init)
            w_o = diff_w_o[j]
        else:
            qT, k, vT = _qkv_proj(h.reshape(B, S, D), g[1], moba_w_qkv[j], HEAD_DIM)
            o = _attention("moba", qT, k, vT, bias, ())
            w_o = moba_w_o[j]
        last = i == depth - 1
        h = _ffn(h, g[2], ffn_w_in[i, 1], ffn_w_out[i, 1], proj=(o.reshape(B * S, D), w_o),
                 final_g=final_norm_g if last else None)
    return h.reshape(B, S, D)
```

```python
import functools
import math

import numpy as np
import jax
import jax.numpy as jnp
from jax import lax
from jax.experimental import pallas as pl
from jax.experimental.pallas import tpu as pltpu

HEAD_DIM = 64
PAIR = 2 * HEAD_DIM
MOBA_BLOCK = 256
MOBA_TOPK = 3
KMEAN_TERMS = 3
ONES_ROWS = 16
REL_BUCKETS = 32
REL_MAX_DIST = 128
FFN_RESIDUAL = 0.5
RMS_EPS = 1e-6
SUBLN_EPS = 1e-5

ATT_TILE = 256
PAIRS_PER_STEP = 8
Q_TILES_PER_STEP = 2
QK_AHEAD = 4
QK_AHEAD_PAIR = 2
FFN_CHUNK = 256
TOKEN_TILE = 1024
FFN_TOKEN_TILE = 1024
MASK_VALUE = -1e30
LOG2E = math.log2(math.e)
VMEM_LIMIT_BYTES = 62 * 1024 * 1024

F32 = jnp.float32
BF16 = jnp.bfloat16
_NT = (((1,), (1,)), ((), ()))


def _rms(x, g, eps):
    return x * lax.rsqrt(jnp.mean(x * x, axis=-1, keepdims=True) + eps) * g


def _const_spec(shape):
    return pl.BlockSpec(shape, lambda *_: (0,) * len(shape), pipeline_mode=pl.Buffered(1))


def _rel_bucket_np(dist):
    n = np.maximum(dist, 0)
    max_exact = REL_BUCKETS // 2
    nf = np.maximum(n, 1).astype(np.float32)
    large = max_exact + (np.log(nf / np.float32(max_exact)) / np.float32(math.log(REL_MAX_DIST / max_exact))
                         * np.float32(REL_BUCKETS - max_exact)).astype(np.int32)
    large = np.minimum(large, REL_BUCKETS - 1)
    return np.where(n < max_exact, n, large).astype(np.int32)


def _bucket_tiles(t):
    j = np.arange(t)[:, None]
    i = np.arange(t)[None, :]
    diag = np.where(i - j >= 0, _rel_bucket_np(i - j), -1)
    prev = _rel_bucket_np(i - j + t)
    assert _rel_bucket_np(np.arange(t + 1, 8 * t)).min() == REL_BUCKETS - 1
    return np.stack([diag, prev]).astype(np.int32)


def _bias_kernel(rb_ref, idx_ref, out_ref):
    m = pl.program_id(0)
    far = rb_ref[REL_BUCKETS - 1, m]
    for t in range(2):
        idx = idx_ref[t]
        acc = jnp.zeros(idx.shape, F32)
        for b in range(REL_BUCKETS - 1):
            acc = jnp.where(idx == b, (rb_ref[b, m] - far) * LOG2E, acc)
        out_ref[0, t] = jnp.where(idx < 0, MASK_VALUE, acc)


def _bias_tiles(rel_bias, t):
    n_maps = rel_bias.shape[1]
    idx = jnp.asarray(_bucket_tiles(t))
    return pl.pallas_call(
        _bias_kernel,
        grid=(n_maps,),
        in_specs=[pl.BlockSpec(memory_space=pltpu.SMEM),
                  pl.BlockSpec((2, t, t), lambda m: (0, 0, 0))],
        out_specs=pl.BlockSpec((1, 2, t, t), lambda m: (m, 0, 0, 0)),
        out_shape=jax.ShapeDtypeStruct((n_maps, 2, t, t), F32),
        name="rel_bias_tiles",
    )(rel_bias.astype(F32), idx)


def _ffn_kernel(*refs, n_chunks, has_proj, has_final):
    refs = list(refs)
    x_ref = refs.pop(0)
    if has_proj:
        o_ref, wo_ref = refs.pop(0), refs.pop(0)
    g_ref, win_ref, wout_ref = refs.pop(0), refs.pop(0), refs.pop(0)
    if has_final:
        gf_ref = refs.pop(0)
    (out_ref,) = refs

    x = x_ref[...]
    if has_proj:
        x = x + jnp.dot(o_ref[...], wo_ref[...], preferred_element_type=F32)
    hn = _rms(x, g_ref[...], RMS_EPS).astype(BF16)
    acc = None
    d_ff = n_chunks * FFN_CHUNK
    for c in range(n_chunks):
        cols = slice(c * FFN_CHUNK, (c + 1) * FFN_CHUNK)
        gate = jnp.dot(hn, win_ref[:, cols], preferred_element_type=F32)
        up = jnp.dot(hn, win_ref[:, d_ff + c * FFN_CHUNK:d_ff + (c + 1) * FFN_CHUNK], preferred_element_type=F32)
        a = (gate * (1.0 / (1.0 + jnp.exp(-gate))) * up).astype(BF16)
        part = jnp.dot(a, wout_ref[cols, :], preferred_element_type=F32)
        acc = part if acc is None else acc + part
    y = x + FFN_RESIDUAL * acc
    if has_final:
        y = _rms(y, gf_ref[...], RMS_EPS)
    out_ref[...] = y


def _ffn(x2d, g, w_in, w_out, proj=None, final_g=None):
    T, D = x2d.shape
    F = w_out.shape[0]
    n_chunks = F // FFN_CHUNK
    assert n_chunks * FFN_CHUNK == F
    tm = math.gcd(T, FFN_TOKEN_TILE)
    w_in_c = w_in.astype(BF16)
    w_out_c = w_out.astype(BF16)

    row = lambda i: (i, 0)
    args, specs = [x2d], [pl.BlockSpec((tm, D), row)]
    if proj is not None:
        o2d, w_o = proj
        args += [o2d, w_o.astype(BF16)]
        specs += [pl.BlockSpec((tm, o2d.shape[1]), row), _const_spec(w_o.shape)]
    args += [g.reshape(1, D).astype(F32), w_in_c, w_out_c]
    specs += [_const_spec((1, D)), _const_spec(w_in_c.shape), _const_spec(w_out_c.shape)]
    if final_g is not None:
        args.append(final_g.reshape(1, D).astype(F32))
        specs.append(_const_spec((1, D)))

    return pl.pallas_call(
        functools.partial(_ffn_kernel, n_chunks=n_chunks, has_proj=proj is not None,
                          has_final=final_g is not None),
        grid=(T // tm,),
        in_specs=specs,
        out_specs=pl.BlockSpec((tm, D), row),
        out_shape=jax.ShapeDtypeStruct((T, D), F32),
        compiler_params=pltpu.CompilerParams(dimension_semantics=("parallel",),
                                             vmem_limit_bytes=VMEM_LIMIT_BYTES),
        name="ffn",
    )(*args)


def _proj_kernel(x_ref, g_ref, wqT_ref, wk_ref, wvT_ref, qT_ref, k_ref, vT_ref, *, n_pairs, n_sub, v_dim):
    hn = _rms(x_ref[0], g_ref[...], RMS_EPS).astype(BF16)
    k_ref[0] = jnp.dot(hn, wk_ref[...], preferred_element_type=F32).astype(BF16)
    qT = lax.dot_general(wqT_ref[...], hn, _NT, preferred_element_type=F32).astype(BF16)
    for c in range(n_sub):
        qT_ref[0, c] = qT[:, c * ATT_TILE:(c + 1) * ATT_TILE]
    vT = lax.dot_general(wvT_ref[...], hn, _NT, preferred_element_type=F32).astype(BF16)
    ones_rows = jnp.where(lax.broadcasted_iota(jnp.int32, (ONES_ROWS, ATT_TILE), 0) == 0, 1.0, 0.0).astype(BF16)
    group = v_dim + ONES_ROWS
    for p in range(n_pairs):
        for c in range(n_sub):
            cols = slice(c * ATT_TILE, (c + 1) * ATT_TILE)
            for i in range(PAIR // v_dim):
                vT_ref[0, p, c, i * group:i * group + v_dim, :] = vT[p * PAIR + i * v_dim:p * PAIR + (i + 1) * v_dim, cols]
                vT_ref[0, p, c, i * group + v_dim:(i + 1) * group, :] = ones_rows


def _v_rows(v_dim):
    return (PAIR // v_dim) * (v_dim + ONES_ROWS)


def _qkv_proj(h, g, w_qkv, v_dim):
    B, S, D = h.shape
    tm = TOKEN_TILE
    n_pairs, n_sub, nk, rows = D // PAIR, tm // ATT_TILE, S // ATT_TILE, _v_rows(v_dim)
    wq, wk, wv = w_qkv[:, :D], w_qkv[:, D:2 * D], w_qkv[:, 2 * D:]
    wqT = (wq * (HEAD_DIM ** -0.5 * LOG2E)).T.astype(BF16)
    wvT = wv.T.astype(BF16)
    return pl.pallas_call(
        functools.partial(_proj_kernel, n_pairs=n_pairs, n_sub=n_sub, v_dim=v_dim),
        grid=(B, S // tm),
        in_specs=[pl.BlockSpec((1, tm, D), lambda b, s: (b, s, 0)),
                  _const_spec((1, D)), _const_spec((D, D)), _const_spec((D, D)), _const_spec((D, D))],
        out_specs=[pl.BlockSpec((1, n_sub, D, ATT_TILE), lambda b, s: (b, s, 0, 0)),
                   pl.BlockSpec((1, tm, D), lambda b, s: (b, s, 0)),
                   pl.BlockSpec((1, n_pairs, n_sub, rows, ATT_TILE), lambda b, s: (b, 0, s, 0, 0))],
        out_shape=[jax.ShapeDtypeStruct((B, nk, D, ATT_TILE), BF16),
                   jax.ShapeDtypeStruct((B, S, D), BF16),
                   jax.ShapeDtypeStruct((B, n_pairs, nk, rows, ATT_TILE), BF16)],
        compiler_params=pltpu.CompilerParams(dimension_semantics=("parallel", "parallel"),
                                             vmem_limit_bytes=VMEM_LIMIT_BYTES),
        name="qkv_proj",
    )(h, g.reshape(1, D).astype(F32), wqT, wk.astype(BF16), wvT)


def _split_pairs(qT_ref, qs, qz_ref):
    zeros = jnp.zeros((HEAD_DIM, qT_ref.shape[3]), qT_ref.dtype)
    for g in range(PAIRS_PER_STEP):
        q = qT_ref[0, qs, g * PAIR:(g + 1) * PAIR, :]
        qz_ref[2 * g, :HEAD_DIM, :] = q[:HEAD_DIM]
        qz_ref[2 * g, HEAD_DIM:, :] = zeros
        qz_ref[2 * g + 1, :HEAD_DIM, :] = zeros
        qz_ref[2 * g + 1, HEAD_DIM:, :] = q[HEAD_DIM:]


def _update(m, scores, vT_tiles, biases, m_ref, acc_ref, first):
    scores = [s if b is None else b + s for s, (b, _) in zip(scores, biases)]
    masks = [r for _, r in biases]
    if all(r is None for r in masks):
        m_cur = jnp.max(functools.reduce(jnp.maximum, scores), axis=0, keepdims=True)
    else:
        m_cur = functools.reduce(jnp.maximum, [jnp.max(s, axis=0, keepdims=True) + (0.0 if r is None else r)
                                               for s, r in zip(scores, masks)])
    if first:
        m_new = m_cur
    else:
        m_old = m_ref[m]
        m_new = jnp.maximum(m_old, m_cur)
    shifts = [m_new if r is None else m_new - r for r in masks]
    pv = functools.reduce(jnp.add, [jnp.dot(v, jnp.exp2(s - sh).astype(BF16), preferred_element_type=F32)
                                    for v, s, sh in zip(vT_tiles, scores, shifts)])
    if first:
        acc_ref[m] = pv
    else:
        acc_ref[m] = jnp.exp2(m_old - m_new) * acc_ref[m] + pv
    m_ref[m] = m_new


def _sweep(qi, k_ref, vT_ref, v_rows, near_bias, far_bias, state, pre_ref, with_first_block=(), may_be_first_tile=True):
    qz_ref, m_ref, acc_ref = state
    n_maps = 2 * PAIRS_PER_STEP

    def raw_scores(j, m):
        rows = pl.ds(pl.multiple_of(j * ATT_TILE, ATT_TILE), ATT_TILE)
        k_t = k_ref[0, rows, (m // 2) * PAIR:(m // 2 + 1) * PAIR]
        return jnp.dot(k_t, qz_ref[m], preferred_element_type=F32)

    def block(tiles, biases, first, ahead, preloaded, next_tiles, next_ahead):
        scores = {}
        for m in range(ahead):
            scores[m] = ([pre_ref[m, t] for t in range(len(tiles))] if preloaded
                         else [raw_scores(j, m) for j in tiles])
        for m in range(n_maps):
            nxt = m + ahead
            if nxt < n_maps:
                scores[nxt] = [raw_scores(j, nxt) for j in tiles]
            elif nxt - n_maps < next_ahead:
                for t, j in enumerate(next_tiles):
                    pre_ref[nxt - n_maps, t] = raw_scores(j, nxt - n_maps)
            _update(m, scores.pop(m), [v_rows(vT_ref[0, m // 2, j], m) for j in tiles], biases(m),
                    m_ref, acc_ref, first)

    n_far = jnp.maximum(qi - 1, 0)
    n_pairs = n_far // 2

    def first_pair():
        for traced_alongside in with_first_block:
            traced_alongside()
        block([qi - 1, qi], lambda m: [near_bias(m, 1, qi - 1), near_bias(m, 0, qi)], True, QK_AHEAD_PAIR, False,
              [0, 1], QK_AHEAD_PAIR)

    if may_be_first_tile:
        @pl.when(qi == 0)
        def _():
            block([qi], lambda m: [near_bias(m, 0, qi)], True, QK_AHEAD, False, [], 0)

        pl.when(qi >= 1)(first_pair)
    else:
        first_pair()

    @pl.when(n_far % 2 == 1)
    def _():
        block([n_far - 1], lambda m: [far_bias(m, n_far - 1)], False, QK_AHEAD, False, [], 0)

    def far_pair(p, carry):
        j, jn = 2 * p, 2 * jnp.minimum(p + 1, n_pairs - 1)
        block([j, j + 1], lambda m: [far_bias(m, j), far_bias(m, j + 1)], False, QK_AHEAD_PAIR, True,
              [jn, jn + 1], QK_AHEAD_PAIR)
        return carry

    lax.fori_loop(0, n_pairs, far_pair, 0)


def _query_tiles(step_body):
    first = pl.program_id(2) * Q_TILES_PER_STEP
    pending = []
    for qs in range(Q_TILES_PER_STEP):
        sweep, finish = step_body(qs, first + qs)
        sweep(pending)
        pending = [finish]
    pending[0]()


def _diff_kernel(qT_ref, k_ref, vT_ref, bias_ref, lam_ref, g_ref, o_ref,
                 qz_ref, m_ref, pre_ref, acc_ref, *, lam_init):
    def tile(qs, qi):
        qz, m_run, acc = qz_ref.at[qs], m_ref.at[qs], acc_ref.at[qs]

        def sweep(extra):
            if qs == 0:
                _split_pairs(qT_ref, qs, qz)
                first = []
            else:
                first = [functools.partial(_split_pairs, qT_ref, qs, qz)]
            _sweep(qi, k_ref, vT_ref, lambda v, m: v,
                   lambda m, t, j: (bias_ref[m, t], None), lambda m, j: (None, None),
                   (qz, m_run, acc), pre_ref, with_first_block=first + extra, may_be_first_tile=qs == 0)

        def finish():
            lp = lam_ref[...]
            lam = (jnp.exp(jnp.sum(lp[0:1] * lp[1:2], axis=-1, keepdims=True))
                   - jnp.exp(jnp.sum(lp[2:3] * lp[3:4], axis=-1, keepdims=True)) + lam_init)

            def weighted(m, scale):
                return acc[m, :PAIR, :] * (scale / acc[m, PAIR:PAIR + 1, :])

            for g in range(PAIRS_PER_STEP):
                o = weighted(2 * g, 1.0) - weighted(2 * g + 1, lam)
                o = o * lax.rsqrt(jnp.mean(o * o, axis=0, keepdims=True) + SUBLN_EPS) * g_ref[...]
                o_ref[0, qs * ATT_TILE:(qs + 1) * ATT_TILE, g * PAIR:(g + 1) * PAIR] = o.T.astype(BF16)

        return sweep, finish

    _query_tiles(tile)


def _moba_kernel(qT_ref, k_ref, vT_ref, bias_ref, o_ref,
                 qz_ref, m_ref, pre_ref, acc_ref, kmean_ref, selb_ref, *, nk):
    tq = qT_ref.shape[3]
    head_rows = HEAD_DIM + ONES_ROWS

    def tile(qs, qi):
        qz, m_run, acc = qz_ref.at[qs], m_ref.at[qs], acc_ref.at[qs]

        def select_blocks():
            blk = lax.broadcasted_iota(jnp.int32, (nk, tq), 0).astype(F32)
            eligible = blk < qi.astype(F32)
            for m in range(2 * PAIRS_PER_STEP):
                rest = kmean_ref[:, (m // 2) * PAIR:(m // 2 + 1) * PAIR]
                gate = jnp.zeros((nk, tq), F32)
                for _ in range(KMEAN_TERMS):
                    term = rest.astype(BF16)
                    gate = gate + jnp.dot(term, qz[m], preferred_element_type=F32)
                    rest = rest - term.astype(F32)
                gate = jnp.where(eligible, gate, -jnp.inf)
                picked = jnp.zeros((nk, tq), F32)
                for _ in range(MOBA_TOPK):
                    best = jnp.max(gate, axis=0, keepdims=True)
                    pick = blk == jnp.min(jnp.where(gate == best, blk, float(nk)), axis=0, keepdims=True)
                    picked = jnp.where(pick, 1.0, picked)
                    gate = jnp.where(pick, -jnp.inf, gate)
                selb_ref[m] = jnp.where(eligible, jnp.where(picked > 0.0, 0.0, MASK_VALUE), MASK_VALUE)

        def sweep(extra):
            if qs == 0:
                @pl.when(qi == 0)
                def _():
                    for j in range(nk):
                        kb = k_ref[0, j * MOBA_BLOCK:(j + 1) * MOBA_BLOCK, :].astype(F32)
                        kmean_ref[j:j + 1, :] = jnp.mean(kb, axis=0, keepdims=True)

                _split_pairs(qT_ref, qs, qz)
                first = [select_blocks]
            else:
                first = [functools.partial(_split_pairs, qT_ref, qs, qz), select_blocks]
            _sweep(qi, k_ref, vT_ref, lambda v, m: v[(m % 2) * head_rows:(m % 2 + 1) * head_rows],
                   lambda m, t, j: (bias_ref[m, t], None if t == 0 else selb_ref[m, pl.ds(j, 1), :]),
                   lambda m, j: (None, selb_ref[m, pl.ds(j, 1), :]),
                   (qz, m_run, acc), pre_ref, with_first_block=first + extra, may_be_first_tile=qs == 0)

        def finish():
            for g in range(PAIRS_PER_STEP):
                o = jnp.concatenate([acc[m, :HEAD_DIM, :] * (1.0 / acc[m, HEAD_DIM:HEAD_DIM + 1, :])
                                     for m in (2 * g, 2 * g + 1)], axis=0)
                o_ref[0, qs * ATT_TILE:(qs + 1) * ATT_TILE, g * PAIR:(g + 1) * PAIR] = o.T.astype(BF16)

        return sweep, finish

    _query_tiles(tile)


def _attention(kind, qT, k, vT, bias, extra, *, lam_init=None):
    B, nk, D, t = qT.shape
    G, S, T = PAIRS_PER_STEP, nk * t, Q_TILES_PER_STEP
    n_maps = 2 * G
    assert D % (G * PAIR) == 0 and nk >= 2 and nk % T == 0 and t == ATT_TILE and QK_AHEAD_PAIR <= QK_AHEAD <= n_maps
    in_specs = [pl.BlockSpec((1, T, G * PAIR, t), lambda b, p, q: (b, q, p, 0)),
                pl.BlockSpec((1, S, G * PAIR), lambda b, p, q: (b, 0, p)),
                pl.BlockSpec((1, G, nk, vT.shape[3], t), lambda b, p, q: (b, p, 0, 0, 0)),
                pl.BlockSpec((n_maps, 2, t, t), lambda b, p, q: (p, 0, 0, 0), pipeline_mode=pl.Buffered(1))]
    scratch = [pltpu.VMEM((T, n_maps, PAIR, t), BF16),
               pltpu.VMEM((T, n_maps, 1, t), F32),
               pltpu.VMEM((max(QK_AHEAD, QK_AHEAD_PAIR), 2, t, t), F32)]
    if kind == "diff":
        body = functools.partial(_diff_kernel, lam_init=lam_init)
        in_specs += [pl.BlockSpec(e.shape, lambda b, p, q: (0, 0)) for e in extra]
        scratch += [pltpu.VMEM((T, n_maps, PAIR + ONES_ROWS, t), F32)]
    else:
        body = functools.partial(_moba_kernel, nk=nk)
        scratch += [pltpu.VMEM((T, n_maps, HEAD_DIM + ONES_ROWS, t), F32),
                    pltpu.VMEM((nk, G * PAIR), F32),
                    pltpu.VMEM((n_maps, nk, t), F32)]
    return pl.pallas_call(
        body,
        grid=(B, D // (G * PAIR), nk // T),
        in_specs=in_specs,
        out_specs=pl.BlockSpec((1, T * t, G * PAIR), lambda b, p, q: (b, q, p)),
        out_shape=jax.ShapeDtypeStruct((B, S, D), BF16),
        scratch_shapes=scratch,
        compiler_params=pltpu.CompilerParams(dimension_semantics=("parallel", "parallel", "arbitrary"),
                                             vmem_limit_bytes=VMEM_LIMIT_BYTES),
        name=kind + "_attention",
    )(qT, k, vT, bias, *extra)


def kernel(x, rel_bias, norm_g, final_norm_g, ffn_w_in, ffn_w_out, diff_w_qkv, diff_lambda,
           diff_subln_g, diff_w_o, moba_w_qkv, moba_w_o):
    B, S, D = x.shape
    depth = norm_g.shape[0]
    assert S % TOKEN_TILE == 0 and D % PAIR == 0 and MOBA_BLOCK == ATT_TILE
    assert rel_bias.shape == (REL_BUCKETS, D // HEAD_DIM)
    bias = _bias_tiles(rel_bias, ATT_TILE)

    h = x.reshape(B * S, D)
    for i in range(depth):
        g = norm_g[i]
        h = _ffn(h, g[0], ffn_w_in[i, 0], ffn_w_out[i, 0])
        j = i // 2
        if i % 2 == 0:
            qT, k, vT = _qkv_proj(h.reshape(B, S, D), g[1], diff_w_qkv[j], PAIR)
            lam_init = 0.8 - 0.6 * math.exp(-0.3 * i)
            g_sub = jnp.broadcast_to((diff_subln_g[j].astype(F32) * (1.0 - lam_init))[:, None], (PAIR, ATT_TILE))
            o = _attention("diff", qT, k, vT, bias, (diff_lambda[j].astype(F32), g_sub), lam_init=lam_init)
            w_o = diff_w_o[j]
        else:
            qT, k, vT = _qkv_proj(h.reshape(B, S, D), g[1], moba_w_qkv[j], HEAD_DIM)
            o = _attention("moba", qT, k, vT, bias, ())
            w_o = moba_w_o[j]
        last = i == depth - 1
        h = _ffn(h, g[2], ffn_w_in[i, 1], ffn_w_out[i, 1], proj=(o.reshape(B * S, D), w_o),
                 final_g=final_norm_g if last else None)
    return h.reshape(B, S, D)
```

```python
import functools
import math

import numpy as np
import jax
import jax.numpy as jnp
from jax import lax
from jax.experimental import pallas as pl
from jax.experimental.pallas import tpu as pltpu

HEAD_DIM = 64
PAIR = 2 * HEAD_DIM
MOBA_BLOCK = 256
MOBA_TOPK = 3
KMEAN_TERMS = 3
ONES_ROWS = 16
REL_BUCKETS = 32
REL_MAX_DIST = 128
FFN_RESIDUAL = 0.5
RMS_EPS = 1e-6
SUBLN_EPS = 1e-5

ATT_TILE = 256
PAIRS_PER_STEP = 8
Q_TILES_PER_STEP = 2
QK_AHEAD = 4
QK_AHEAD_PAIR = 2
FFN_CHUNK = 256
TOKEN_TILE = 1024
FFN_TOKEN_TILE = 1024
MASK_VALUE = -1e30
LOG2E = math.log2(math.e)
VMEM_LIMIT_BYTES = 62 * 1024 * 1024

F32 = jnp.float32
BF16 = jnp.bfloat16
_TT = (((0,), (1,)), ((), ()))


def _rms(x, g, eps):
    return x * lax.rsqrt(jnp.mean(x * x, axis=-1, keepdims=True) + eps) * g


def _const_spec(shape):
    return pl.BlockSpec(shape, lambda *_: (0,) * len(shape), pipeline_mode=pl.Buffered(1))


def _rel_bucket_np(dist):
    n = np.maximum(dist, 0)
    max_exact = REL_BUCKETS // 2
    nf = np.maximum(n, 1).astype(np.float32)
    large = max_exact + (np.log(nf / np.float32(max_exact)) / np.float32(math.log(REL_MAX_DIST / max_exact))
                         * np.float32(REL_BUCKETS - max_exact)).astype(np.int32)
    large = np.minimum(large, REL_BUCKETS - 1)
    return np.where(n < max_exact, n, large).astype(np.int32)


def _bucket_tiles(t):
    j = np.arange(t)[:, None]
    i = np.arange(t)[None, :]
    diag = np.where(i - j >= 0, _rel_bucket_np(i - j), -1)
    prev = _rel_bucket_np(i - j + t)
    assert _rel_bucket_np(np.arange(t + 1, 8 * t)).min() == REL_BUCKETS - 1
    return np.stack([diag, prev]).astype(np.int32)


def _bias_kernel(rb_ref, idx_ref, out_ref):
    m = pl.program_id(0)
    far = rb_ref[REL_BUCKETS - 1, m]
    for t in range(2):
        idx = idx_ref[t]
        acc = jnp.zeros(idx.shape, F32)
        for b in range(REL_BUCKETS - 1):
            acc = jnp.where(idx == b, (rb_ref[b, m] - far) * LOG2E, acc)
        out_ref[0, t] = jnp.where(idx < 0, MASK_VALUE, acc)


def _bias_tiles(rel_bias, t):
    n_maps = rel_bias.shape[1]
    idx = jnp.asarray(_bucket_tiles(t))
    return pl.pallas_call(
        _bias_kernel,
        grid=(n_maps,),
        in_specs=[pl.BlockSpec(memory_space=pltpu.SMEM),
                  pl.BlockSpec((2, t, t), lambda m: (0, 0, 0))],
        out_specs=pl.BlockSpec((1, 2, t, t), lambda m: (m, 0, 0, 0)),
        out_shape=jax.ShapeDtypeStruct((n_maps, 2, t, t), F32),
        name="rel_bias_tiles",
    )(rel_bias.astype(F32), idx)


def _ffn_kernel(*refs, n_chunks, has_proj, has_final):
    refs = list(refs)
    x_ref = refs.pop(0)
    if has_proj:
        o_ref, wo_ref = refs.pop(0), refs.pop(0)
    g_ref, win_ref, wout_ref = refs.pop(0), refs.pop(0), refs.pop(0)
    if has_final:
        gf_ref = refs.pop(0)
    (out_ref,) = refs

    x = x_ref[...]
    if has_proj:
        x = x + jnp.dot(o_ref[...], wo_ref[...], preferred_element_type=F32)
    hn = _rms(x, g_ref[...], RMS_EPS).astype(BF16)
    acc = None
    d_ff = n_chunks * FFN_CHUNK
    for c in range(n_chunks):
        cols = slice(c * FFN_CHUNK, (c + 1) * FFN_CHUNK)
        gate = jnp.dot(hn, win_ref[:, cols], preferred_element_type=F32)
        up = jnp.dot(hn, win_ref[:, d_ff + c * FFN_CHUNK:d_ff + (c + 1) * FFN_CHUNK], preferred_element_type=F32)
        a = (gate * (1.0 / (1.0 + jnp.exp(-gate))) * up).astype(BF16)
        part = jnp.dot(a, wout_ref[cols, :], preferred_element_type=F32)
        acc = part if acc is None else acc + part
    y = x + FFN_RESIDUAL * acc
    if has_final:
        y = _rms(y, gf_ref[...], RMS_EPS)
    out_ref[...] = y


def _ffn(x2d, g, w_in, w_out, proj=None, final_g=None):
    T, D = x2d.shape
    F = w_out.shape[0]
    n_chunks = F // FFN_CHUNK
    assert n_chunks * FFN_CHUNK == F
    tm = math.gcd(T, FFN_TOKEN_TILE)
    w_in_c = w_in.astype(BF16)
    w_out_c = w_out.astype(BF16)

    row = lambda i: (i, 0)
    args, specs = [x2d], [pl.BlockSpec((tm, D), row)]
    if proj is not None:
        o2d, w_o = proj
        args += [o2d, w_o.astype(BF16)]
        specs += [pl.BlockSpec((tm, o2d.shape[1]), row), _const_spec(w_o.shape)]
    args += [g.reshape(1, D).astype(F32), w_in_c, w_out_c]
    specs += [_const_spec((1, D)), _const_spec(w_in_c.shape), _const_spec(w_out_c.shape)]
    if final_g is not None:
        args.append(final_g.reshape(1, D).astype(F32))
        specs.append(_const_spec((1, D)))

    return pl.pallas_call(
        functools.partial(_ffn_kernel, n_chunks=n_chunks, has_proj=proj is not None,
                          has_final=final_g is not None),
        grid=(T // tm,),
        in_specs=specs,
        out_specs=pl.BlockSpec((tm, D), row),
        out_shape=jax.ShapeDtypeStruct((T, D), F32),
        compiler_params=pltpu.CompilerParams(dimension_semantics=("parallel",),
                                             vmem_limit_bytes=VMEM_LIMIT_BYTES),
        name="ffn",
    )(*args)


def _proj_kernel(x_ref, g_ref, w_ref, qT_ref, k_ref, vT_ref, *, n_pairs, n_sub, v_dim):
    D = x_ref.shape[2]
    hn = _rms(x_ref[0], g_ref[...], RMS_EPS).astype(BF16)
    k_ref[0] = jnp.dot(hn, w_ref[:, D:2 * D], preferred_element_type=F32).astype(BF16)
    qT = lax.dot_general(w_ref[:, :D], hn, _TT, preferred_element_type=F32) * (HEAD_DIM ** -0.5 * LOG2E)
    qT = qT.astype(BF16)
    for c in range(n_sub):
        qT_ref[0, c] = qT[:, c * ATT_TILE:(c + 1) * ATT_TILE]
    vT = lax.dot_general(w_ref[:, 2 * D:], hn, _TT, preferred_element_type=F32).astype(BF16)
    ones_rows = jnp.where(lax.broadcasted_iota(jnp.int32, (ONES_ROWS, ATT_TILE), 0) == 0, 1.0, 0.0).astype(BF16)
    group = v_dim + ONES_ROWS
    for p in range(n_pairs):
        for c in range(n_sub):
            cols = slice(c * ATT_TILE, (c + 1) * ATT_TILE)
            for i in range(PAIR // v_dim):
                vT_ref[0, p, c, i * group:i * group + v_dim, :] = vT[p * PAIR + i * v_dim:p * PAIR + (i + 1) * v_dim, cols]
                vT_ref[0, p, c, i * group + v_dim:(i + 1) * group, :] = ones_rows


def _v_rows(v_dim):
    return (PAIR // v_dim) * (v_dim + ONES_ROWS)


def _qkv_proj(h, g, w_qkv, v_dim):
    B, S, D = h.shape
    tm = TOKEN_TILE
    n_pairs, n_sub, nk, rows = D // PAIR, tm // ATT_TILE, S // ATT_TILE, _v_rows(v_dim)
    return pl.pallas_call(
        functools.partial(_proj_kernel, n_pairs=n_pairs, n_sub=n_sub, v_dim=v_dim),
        grid=(B, S // tm),
        in_specs=[pl.BlockSpec((1, tm, D), lambda b, s: (b, s, 0)),
                  _const_spec((1, D)), _const_spec((D, 3 * D))],
        out_specs=[pl.BlockSpec((1, n_sub, D, ATT_TILE), lambda b, s: (b, s, 0, 0)),
                   pl.BlockSpec((1, tm, D), lambda b, s: (b, s, 0)),
                   pl.BlockSpec((1, n_pairs, n_sub, rows, ATT_TILE), lambda b, s: (b, 0, s, 0, 0))],
        out_shape=[jax.ShapeDtypeStruct((B, nk, D, ATT_TILE), BF16),
                   jax.ShapeDtypeStruct((B, S, D), BF16),
                   jax.ShapeDtypeStruct((B, n_pairs, nk, rows, ATT_TILE), BF16)],
        compiler_params=pltpu.CompilerParams(dimension_semantics=("parallel", "parallel"),
                                             vmem_limit_bytes=VMEM_LIMIT_BYTES),
        name="qkv_proj",
    )(h, g.reshape(1, D).astype(F32), w_qkv.astype(BF16))


def _split_pairs(qT_ref, qs, qz_ref):
    zeros = jnp.zeros((HEAD_DIM, qT_ref.shape[3]), qT_ref.dtype)
    for g in range(PAIRS_PER_STEP):
        q = qT_ref[0, qs, g * PAIR:(g + 1) * PAIR, :]
        qz_ref[2 * g, :HEAD_DIM, :] = q[:HEAD_DIM]
        qz_ref[2 * g, HEAD_DIM:, :] = zeros
        qz_ref[2 * g + 1, :HEAD_DIM, :] = zeros
        qz_ref[2 * g + 1, HEAD_DIM:, :] = q[HEAD_DIM:]


def _update(m, scores, vT_tiles, biases, m_ref, acc_ref, first):
    scores = [s if b is None else b + s for s, (b, _) in zip(scores, biases)]
    masks = [r for _, r in biases]
    if all(r is None for r in masks):
        m_cur = jnp.max(functools.reduce(jnp.maximum, scores), axis=0, keepdims=True)
    else:
        m_cur = functools.reduce(jnp.maximum, [jnp.max(s, axis=0, keepdims=True) + (0.0 if r is None else r)
                                               for s, r in zip(scores, masks)])
    if first:
        m_new = m_cur
    else:
        m_old = m_ref[m]
        m_new = jnp.maximum(m_old, m_cur)
    shifts = [m_new if r is None else m_new - r for r in masks]
    pv = functools.reduce(jnp.add, [jnp.dot(v, jnp.exp2(s - sh).astype(BF16), preferred_element_type=F32)
                                    for v, s, sh in zip(vT_tiles, scores, shifts)])
    if first:
        acc_ref[m] = pv
    else:
        acc_ref[m] = jnp.exp2(m_old - m_new) * acc_ref[m] + pv
    m_ref[m] = m_new


def _sweep(qi, k_ref, vT_ref, v_rows, near_bias, far_bias, state, pre_ref, with_first_block=(), may_be_first_tile=True):
    qz_ref, m_ref, acc_ref = state
    n_maps = 2 * PAIRS_PER_STEP

    def raw_scores(j, m):
        rows = pl.ds(pl.multiple_of(j * ATT_TILE, ATT_TILE), ATT_TILE)
        k_t = k_ref[0, rows, (m // 2) * PAIR:(m // 2 + 1) * PAIR]
        return jnp.dot(k_t, qz_ref[m], preferred_element_type=F32)

    def block(tiles, biases, first, ahead, preloaded, next_tiles, next_ahead):
        scores = {}
        for m in range(ahead):
            scores[m] = ([pre_ref[m, t] for t in range(len(tiles))] if preloaded
                         else [raw_scores(j, m) for j in tiles])
        for m in range(n_maps):
            nxt = m + ahead
            if nxt < n_maps:
                scores[nxt] = [raw_scores(j, nxt) for j in tiles]
            elif nxt - n_maps < next_ahead:
                for t, j in enumerate(next_tiles):
                    pre_ref[nxt - n_maps, t] = raw_scores(j, nxt - n_maps)
            _update(m, scores.pop(m), [v_rows(vT_ref[0, m // 2, j], m) for j in tiles], biases(m),
                    m_ref, acc_ref, first)

    n_far = jnp.maximum(qi - 1, 0)
    n_pairs = n_far // 2

    def first_pair():
        for traced_alongside in with_first_block:
            traced_alongside()
        block([qi - 1, qi], lambda m: [near_bias(m, 1, qi - 1), near_bias(m, 0, qi)], True, QK_AHEAD_PAIR, False,
              [0, 1], QK_AHEAD_PAIR)

    if may_be_first_tile:
        @pl.when(qi == 0)
        def _():
            block([qi], lambda m: [near_bias(m, 0, qi)], True, QK_AHEAD, False, [], 0)

        pl.when(qi >= 1)(first_pair)
    else:
        first_pair()

    @pl.when(n_far % 2 == 1)
    def _():
        block([n_far - 1], lambda m: [far_bias(m, n_far - 1)], False, QK_AHEAD, False, [], 0)

    def far_pair(p, carry):
        j, jn = 2 * p, 2 * jnp.minimum(p + 1, n_pairs - 1)
        block([j, j + 1], lambda m: [far_bias(m, j), far_bias(m, j + 1)], False, QK_AHEAD_PAIR, True,
              [jn, jn + 1], QK_AHEAD_PAIR)
        return carry

    lax.fori_loop(0, n_pairs, far_pair, 0)


def _query_tiles(step_body):
    first = pl.program_id(2) * Q_TILES_PER_STEP
    pending = []
    for qs in range(Q_TILES_PER_STEP):
        sweep, finish = step_body(qs, first + qs)
        sweep(pending)
        pending = [finish]
    pending[0]()


def _diff_kernel(qT_ref, k_ref, vT_ref, bias_ref, lam_ref, g_ref, o_ref,
                 qz_ref, m_ref, pre_ref, acc_ref, *, lam_init):
    def tile(qs, qi):
        qz, m_run, acc = qz_ref.at[qs], m_ref.at[qs], acc_ref.at[qs]

        def sweep(extra):
            if qs == 0:
                _split_pairs(qT_ref, qs, qz)
                first = []
            else:
                first = [functools.partial(_split_pairs, qT_ref, qs, qz)]
            _sweep(qi, k_ref, vT_ref, lambda v, m: v,
                   lambda m, t, j: (bias_ref[m, t], None), lambda m, j: (None, None),
                   (qz, m_run, acc), pre_ref, with_first_block=first + extra, may_be_first_tile=qs == 0)

        def finish():
            lp = lam_ref[...]
            lam = (jnp.exp(jnp.sum(lp[0:1] * lp[1:2], axis=-1, keepdims=True))
                   - jnp.exp(jnp.sum(lp[2:3] * lp[3:4], axis=-1, keepdims=True)) + lam_init)

            def weighted(m, scale):
                return acc[m, :PAIR, :] * (scale / acc[m, PAIR:PAIR + 1, :])

            for g in range(PAIRS_PER_STEP):
                o = weighted(2 * g, 1.0) - weighted(2 * g + 1, lam)
                o = o * lax.rsqrt(jnp.mean(o * o, axis=0, keepdims=True) + SUBLN_EPS) * g_ref[...]
                o_ref[0, qs * ATT_TILE:(qs + 1) * ATT_TILE, g * PAIR:(g + 1) * PAIR] = o.T.astype(BF16)

        return sweep, finish

    _query_tiles(tile)


def _moba_kernel(qT_ref, k_ref, vT_ref, bias_ref, o_ref,
                 qz_ref, m_ref, pre_ref, acc_ref, kmean_ref, selb_ref, *, nk):
    tq = qT_ref.shape[3]
    head_rows = HEAD_DIM + ONES_ROWS

    def tile(qs, qi):
        qz, m_run, acc = qz_ref.at[qs], m_ref.at[qs], acc_ref.at[qs]

        def select_blocks():
            blk = lax.broadcasted_iota(jnp.int32, (nk, tq), 0).astype(F32)
            eligible = blk < qi.astype(F32)
            for m in range(2 * PAIRS_PER_STEP):
                rest = kmean_ref[:, (m // 2) * PAIR:(m // 2 + 1) * PAIR]
                gate = jnp.zeros((nk, tq), F32)
                for _ in range(KMEAN_TERMS):
                    term = rest.astype(BF16)
                    gate = gate + jnp.dot(term, qz[m], preferred_element_type=F32)
                    rest = rest - term.astype(F32)
                gate = jnp.where(eligible, gate, -jnp.inf)
                picked = jnp.zeros((nk, tq), F32)
                for _ in range(MOBA_TOPK):
                    best = jnp.max(gate, axis=0, keepdims=True)
                    pick = blk == jnp.min(jnp.where(gate == best, blk, float(nk)), axis=0, keepdims=True)
                    picked = jnp.where(pick, 1.0, picked)
                    gate = jnp.where(pick, -jnp.inf, gate)
                selb_ref[m] = jnp.where(eligible, jnp.where(picked > 0.0, 0.0, MASK_VALUE), MASK_VALUE)

        def sweep(extra):
            if qs == 0:
                @pl.when(qi == 0)
                def _():
                    for j in range(nk):
                        kb = k_ref[0, j * MOBA_BLOCK:(j + 1) * MOBA_BLOCK, :].astype(F32)
                        kmean_ref[j:j + 1, :] = jnp.mean(kb, axis=0, keepdims=True)

                _split_pairs(qT_ref, qs, qz)
                first = [select_blocks]
            else:
                first = [functools.partial(_split_pairs, qT_ref, qs, qz), select_blocks]
            _sweep(qi, k_ref, vT_ref, lambda v, m: v[(m % 2) * head_rows:(m % 2 + 1) * head_rows],
                   lambda m, t, j: (bias_ref[m, t], None if t == 0 else selb_ref[m, pl.ds(j, 1), :]),
                   lambda m, j: (None, selb_ref[m, pl.ds(j, 1), :]),
                   (qz, m_run, acc), pre_ref, with_first_block=first + extra, may_be_first_tile=qs == 0)

        def finish():
            for g in range(PAIRS_PER_STEP):
                o = jnp.concatenate([acc[m, :HEAD_DIM, :] * (1.0 / acc[m, HEAD_DIM:HEAD_DIM + 1, :])
                                     for m in (2 * g, 2 * g + 1)], axis=0)
                o_ref[0, qs * ATT_TILE:(qs + 1) * ATT_TILE, g * PAIR:(g + 1) * PAIR] = o.T.astype(BF16)

        return sweep, finish

    _query_tiles(tile)


def _attention(kind, qT, k, vT, bias, extra, *, lam_init=None):
    B, nk, D, t = qT.shape
    G, S, T = PAIRS_PER_STEP, nk * t, Q_TILES_PER_STEP
    n_maps = 2 * G
    assert D % (G * PAIR) == 0 and nk >= 2 and nk % T == 0 and t == ATT_TILE and QK_AHEAD_PAIR <= QK_AHEAD <= n_maps
    in_specs = [pl.BlockSpec((1, T, G * PAIR, t), lambda b, p, q: (b, q, p, 0)),
                pl.BlockSpec((1, S, G * PAIR), lambda b, p, q: (b, 0, p)),
                pl.BlockSpec((1, G, nk, vT.shape[3], t), lambda b, p, q: (b, p, 0, 0, 0)),
                pl.BlockSpec((n_maps, 2, t, t), lambda b, p, q: (p, 0, 0, 0), pipeline_mode=pl.Buffered(1))]
    scratch = [pltpu.VMEM((T, n_maps, PAIR, t), BF16),
               pltpu.VMEM((T, n_maps, 1, t), F32),
               pltpu.VMEM((max(QK_AHEAD, QK_AHEAD_PAIR), 2, t, t), F32)]
    if kind == "diff":
        body = functools.partial(_diff_kernel, lam_init=lam_init)
        in_specs += [pl.BlockSpec(e.shape, lambda b, p, q: (0, 0)) for e in extra]
        scratch += [pltpu.VMEM((T, n_maps, PAIR + ONES_ROWS, t), F32)]
    else:
        body = functools.partial(_moba_kernel, nk=nk)
        scratch += [pltpu.VMEM((T, n_maps, HEAD_DIM + ONES_ROWS, t), F32),
                    pltpu.VMEM((nk, G * PAIR), F32),
                    pltpu.VMEM((n_maps, nk, t), F32)]
    return pl.pallas_call(
        body,
        grid=(B, D // (G * PAIR), nk // T),
        in_specs=in_specs,
        out_specs=pl.BlockSpec((1, T * t, G * PAIR), lambda b, p, q: (b, q, p)),
        out_shape=jax.ShapeDtypeStruct((B, S, D), BF16),
        scratch_shapes=scratch,
        compiler_params=pltpu.CompilerParams(dimension_semantics=("parallel", "parallel", "arbitrary"),
                                             vmem_limit_bytes=VMEM_LIMIT_BYTES),
        name=kind + "_attention",
    )(qT, k, vT, bias, *extra)


def kernel(x, rel_bias, norm_g, final_norm_g, ffn_w_in, ffn_w_out, diff_w_qkv, diff_lambda,
           diff_subln_g, diff_w_o, moba_w_qkv, moba_w_o):
    B, S, D = x.shape
    depth = norm_g.shape[0]
    assert S % TOKEN_TILE == 0 and D % PAIR == 0 and MOBA_BLOCK == ATT_TILE
    assert rel_bias.shape == (REL_BUCKETS, D // HEAD_DIM)
    bias = _bias_tiles(rel_bias, ATT_TILE)

    h = x.reshape(B * S, D)
    for i in range(depth):
        g = norm_g[i]
        h = _ffn(h, g[0], ffn_w_in[i, 0], ffn_w_out[i, 0])
        j = i // 2
        if i % 2 == 0:
            qT, k, vT = _qkv_proj(h.reshape(B, S, D), g[1], diff_w_qkv[j], PAIR)
            lam_init = 0.8 - 0.6 * math.exp(-0.3 * i)
            g_sub = jnp.broadcast_to((diff_subln_g[j].astype(F32) * (1.0 - lam_init))[:, None], (PAIR, ATT_TILE))
            o = _attention("diff", qT, k, vT, bias, (diff_lambda[j].astype(F32), g_sub), lam_init=lam_init)
            w_o = diff_w_o[j]
        else:
            qT, k, vT = _qkv_proj(h.reshape(B, S, D), g[1], moba_w_qkv[j], HEAD_DIM)
            o = _attention("moba", qT, k, vT, bias, ())
            w_o = moba_w_o[j]
        last = i == depth - 1
        h = _ffn(h, g[2], ffn_w_in[i, 1], ffn_w_out[i, 1], proj=(o.reshape(B * S, D), w_o),
                 final_g=final_norm_g if last else None)
    return h.reshape(B, S, D)
```

```python
import functools
import math

import numpy as np
import jax
import jax.numpy as jnp
from jax import lax
from jax.experimental import pallas as pl
from jax.experimental.pallas import tpu as pltpu

HEAD_DIM = 64
PAIR = 2 * HEAD_DIM
MOBA_BLOCK = 256
MOBA_TOPK = 3
KMEAN_TERMS = 3
ONES_ROWS = 16
REL_BUCKETS = 32
REL_MAX_DIST = 128
FFN_RESIDUAL = 0.5
RMS_EPS = 1e-6
SUBLN_EPS = 1e-5

ATT_TILE = 256
PAIRS_PER_STEP = 8
Q_TILES_PER_STEP = 2
QK_AHEAD = 4
QK_AHEAD_PAIR = 2
FFN_CHUNK = 256
TOKEN_TILE = 1024
FFN_TOKEN_TILE = 1024
MASK_VALUE = -1e30
LOG2E = math.log2(math.e)
VMEM_LIMIT_BYTES = 62 * 1024 * 1024

F32 = jnp.float32
BF16 = jnp.bfloat16
_TT = (((0,), (1,)), ((), ()))


def _rms(x, g, eps):
    return x * lax.rsqrt(jnp.mean(x * x, axis=-1, keepdims=True) + eps) * g


def _const_spec(shape):
    return pl.BlockSpec(shape, lambda *_: (0,) * len(shape), pipeline_mode=pl.Buffered(1))


def _rel_bucket_np(dist):
    n = np.maximum(dist, 0)
    max_exact = REL_BUCKETS // 2
    nf = np.maximum(n, 1).astype(np.float32)
    large = max_exact + (np.log(nf / np.float32(max_exact)) / np.float32(math.log(REL_MAX_DIST / max_exact))
                         * np.float32(REL_BUCKETS - max_exact)).astype(np.int32)
    large = np.minimum(large, REL_BUCKETS - 1)
    return np.where(n < max_exact, n, large).astype(np.int32)


def _bucket_tiles(t):
    j = np.arange(t)[:, None]
    i = np.arange(t)[None, :]
    diag = np.where(i - j >= 0, _rel_bucket_np(i - j), -1)
    prev = _rel_bucket_np(i - j + t)
    assert _rel_bucket_np(np.arange(t + 1, 8 * t)).min() == REL_BUCKETS - 1
    return np.stack([diag, prev]).astype(np.int32)


def _bias_kernel(rb_ref, idx_ref, out_ref):
    m = pl.program_id(0)
    far = rb_ref[REL_BUCKETS - 1, m]
    for t in range(2):
        idx = idx_ref[t]
        acc = jnp.zeros(idx.shape, F32)
        for b in range(REL_BUCKETS - 1):
            acc = jnp.where(idx == b, (rb_ref[b, m] - far) * LOG2E, acc)
        out_ref[0, t] = jnp.where(idx < 0, MASK_VALUE, acc)


def _bias_tiles(rel_bias, t):
    n_maps = rel_bias.shape[1]
    idx = jnp.asarray(_bucket_tiles(t))
    return pl.pallas_call(
        _bias_kernel,
        grid=(n_maps,),
        in_specs=[pl.BlockSpec(memory_space=pltpu.SMEM),
                  pl.BlockSpec((2, t, t), lambda m: (0, 0, 0))],
        out_specs=pl.BlockSpec((1, 2, t, t), lambda m: (m, 0, 0, 0)),
        out_shape=jax.ShapeDtypeStruct((n_maps, 2, t, t), F32),
        name="rel_bias_tiles",
    )(rel_bias.astype(F32), idx)


def _ffn_kernel(*refs, n_chunks, has_proj, has_final):
    refs = list(refs)
    x_ref = refs.pop(0)
    if has_proj:
        o_ref, wo_ref = refs.pop(0), refs.pop(0)
    g_ref, win_ref, wout_ref = refs.pop(0), refs.pop(0), refs.pop(0)
    if has_final:
        gf_ref = refs.pop(0)
    (out_ref,) = refs

    x = x_ref[...]
    if has_proj:
        x = x + jnp.dot(o_ref[...], wo_ref[...], preferred_element_type=F32)
    hn = _rms(x, g_ref[...], RMS_EPS).astype(BF16)
    acc = None
    d_ff = n_chunks * FFN_CHUNK
    for c in range(n_chunks):
        cols = slice(c * FFN_CHUNK, (c + 1) * FFN_CHUNK)
        gate = jnp.dot(hn, win_ref[:, cols], preferred_element_type=F32)
        up = jnp.dot(hn, win_ref[:, d_ff + c * FFN_CHUNK:d_ff + (c + 1) * FFN_CHUNK], preferred_element_type=F32)
        a = (gate * (1.0 / (1.0 + jnp.exp(-gate))) * up).astype(BF16)
        part = jnp.dot(a, wout_ref[cols, :], preferred_element_type=F32)
        acc = part if acc is None else acc + part
    y = x + FFN_RESIDUAL * acc
    if has_final:
        y = _rms(y, gf_ref[...], RMS_EPS)
    out_ref[...] = y


def _stacked_spec(stack, index):
    lead = len(index)
    return pl.BlockSpec((None,) * lead + stack.shape[lead:], lambda *_: tuple(index) + (0, 0),
                        pipeline_mode=pl.Buffered(1))


def _ffn(x2d, g, w_in_all, w_out_all, which, proj=None, final_g=None):
    T, D = x2d.shape
    F = w_out_all.shape[-2]
    n_chunks = F // FFN_CHUNK
    assert n_chunks * FFN_CHUNK == F and w_in_all.shape[-2:] == (D, 2 * F)
    tm = math.gcd(T, FFN_TOKEN_TILE)

    row = lambda i: (i, 0)
    args, specs = [x2d], [pl.BlockSpec((tm, D), row)]
    if proj is not None:
        o2d, w_o = proj
        args += [o2d, w_o.astype(BF16)]
        specs += [pl.BlockSpec((tm, o2d.shape[1]), row), _const_spec(w_o.shape)]
    args += [g.reshape(1, D).astype(F32), w_in_all, w_out_all]
    specs += [_const_spec((1, D)), _stacked_spec(w_in_all, which), _stacked_spec(w_out_all, which)]
    if final_g is not None:
        args.append(final_g.reshape(1, D).astype(F32))
        specs.append(_const_spec((1, D)))

    return pl.pallas_call(
        functools.partial(_ffn_kernel, n_chunks=n_chunks, has_proj=proj is not None,
                          has_final=final_g is not None),
        grid=(T // tm,),
        in_specs=specs,
        out_specs=pl.BlockSpec((tm, D), row),
        out_shape=jax.ShapeDtypeStruct((T, D), F32),
        compiler_params=pltpu.CompilerParams(dimension_semantics=("parallel",),
                                             vmem_limit_bytes=VMEM_LIMIT_BYTES),
        name="ffn",
    )(*args)


def _proj_kernel(x_ref, g_ref, w_ref, qT_ref, k_ref, vT_ref, *, n_pairs, n_sub, v_dim):
    D = x_ref.shape[2]
    hn = _rms(x_ref[0], g_ref[...], RMS_EPS).astype(BF16)
    k_ref[0] = jnp.dot(hn, w_ref[:, D:2 * D], preferred_element_type=F32).astype(BF16)
    qT = lax.dot_general(w_ref[:, :D], hn, _TT, preferred_element_type=F32) * (HEAD_DIM ** -0.5 * LOG2E)
    qT = qT.astype(BF16)
    for c in range(n_sub):
        qT_ref[0, c] = qT[:, c * ATT_TILE:(c + 1) * ATT_TILE]
    vT = lax.dot_general(w_ref[:, 2 * D:], hn, _TT, preferred_element_type=F32).astype(BF16)
    ones_rows = jnp.where(lax.broadcasted_iota(jnp.int32, (ONES_ROWS, ATT_TILE), 0) == 0, 1.0, 0.0).astype(BF16)
    group = v_dim + ONES_ROWS
    for p in range(n_pairs):
        for c in range(n_sub):
            cols = slice(c * ATT_TILE, (c + 1) * ATT_TILE)
            for i in range(PAIR // v_dim):
                vT_ref[0, p, c, i * group:i * group + v_dim, :] = vT[p * PAIR + i * v_dim:p * PAIR + (i + 1) * v_dim, cols]
                vT_ref[0, p, c, i * group + v_dim:(i + 1) * group, :] = ones_rows


def _v_rows(v_dim):
    return (PAIR // v_dim) * (v_dim + ONES_ROWS)


def _qkv_proj(h, g, w_qkv, v_dim):
    B, S, D = h.shape
    tm = TOKEN_TILE
    n_pairs, n_sub, nk, rows = D // PAIR, tm // ATT_TILE, S // ATT_TILE, _v_rows(v_dim)
    return pl.pallas_call(
        functools.partial(_proj_kernel, n_pairs=n_pairs, n_sub=n_sub, v_dim=v_dim),
        grid=(B, S // tm),
        in_specs=[pl.BlockSpec((1, tm, D), lambda b, s: (b, s, 0)),
                  _const_spec((1, D)), _const_spec((D, 3 * D))],
        out_specs=[pl.BlockSpec((1, n_sub, D, ATT_TILE), lambda b, s: (b, s, 0, 0)),
                   pl.BlockSpec((1, tm, D), lambda b, s: (b, s, 0)),
                   pl.BlockSpec((1, n_pairs, n_sub, rows, ATT_TILE), lambda b, s: (b, 0, s, 0, 0))],
        out_shape=[jax.ShapeDtypeStruct((B, nk, D, ATT_TILE), BF16),
                   jax.ShapeDtypeStruct((B, S, D), BF16),
                   jax.ShapeDtypeStruct((B, n_pairs, nk, rows, ATT_TILE), BF16)],
        compiler_params=pltpu.CompilerParams(dimension_semantics=("parallel", "parallel"),
                                             vmem_limit_bytes=VMEM_LIMIT_BYTES),
        name="qkv_proj",
    )(h, g.reshape(1, D).astype(F32), w_qkv.astype(BF16))


def _split_pairs(qT_ref, qs, qz_ref):
    zeros = jnp.zeros((HEAD_DIM, qT_ref.shape[3]), qT_ref.dtype)
    for g in range(PAIRS_PER_STEP):
        q = qT_ref[0, qs, g * PAIR:(g + 1) * PAIR, :]
        qz_ref[2 * g, :HEAD_DIM, :] = q[:HEAD_DIM]
        qz_ref[2 * g, HEAD_DIM:, :] = zeros
        qz_ref[2 * g + 1, :HEAD_DIM, :] = zeros
        qz_ref[2 * g + 1, HEAD_DIM:, :] = q[HEAD_DIM:]


def _update(m, scores, vT_tiles, biases, m_ref, acc_ref, first):
    scores = [s if b is None else b + s for s, (b, _) in zip(scores, biases)]
    masks = [r for _, r in biases]
    if all(r is None for r in masks):
        m_cur = jnp.max(functools.reduce(jnp.maximum, scores), axis=0, keepdims=True)
    else:
        m_cur = functools.reduce(jnp.maximum, [jnp.max(s, axis=0, keepdims=True) + (0.0 if r is None else r)
                                               for s, r in zip(scores, masks)])
    if first:
        m_new = m_cur
    else:
        m_old = m_ref[m]
        m_new = jnp.maximum(m_old, m_cur)
    shifts = [m_new if r is None else m_new - r for r in masks]
    pv = functools.reduce(jnp.add, [jnp.dot(v, jnp.exp2(s - sh).astype(BF16), preferred_element_type=F32)
                                    for v, s, sh in zip(vT_tiles, scores, shifts)])
    if first:
        acc_ref[m] = pv
    else:
        acc_ref[m] = jnp.exp2(m_old - m_new) * acc_ref[m] + pv
    m_ref[m] = m_new


def _sweep(qi, k_ref, vT_ref, v_rows, near_bias, far_bias, state, pre_ref, with_first_block=(), may_be_first_tile=True):
    qz_ref, m_ref, acc_ref = state
    n_maps = 2 * PAIRS_PER_STEP

    def raw_scores(j, m):
        rows = pl.ds(pl.multiple_of(j * ATT_TILE, ATT_TILE), ATT_TILE)
        k_t = k_ref[0, rows, (m // 2) * PAIR:(m // 2 + 1) * PAIR]
        return jnp.dot(k_t, qz_ref[m], preferred_element_type=F32)

    def block(tiles, biases, first, ahead, preloaded, next_tiles, next_ahead):
        scores = {}
        for m in range(ahead):
            scores[m] = ([pre_ref[m, t] for t in range(len(tiles))] if preloaded
                         else [raw_scores(j, m) for j in tiles])
        for m in range(n_maps):
            nxt = m + ahead
            if nxt < n_maps:
                scores[nxt] = [raw_scores(j, nxt) for j in tiles]
            elif nxt - n_maps < next_ahead:
                for t, j in enumerate(next_tiles):
                    pre_ref[nxt - n_maps, t] = raw_scores(j, nxt - n_maps)
            _update(m, scores.pop(m), [v_rows(vT_ref[0, m // 2, j], m) for j in tiles], biases(m),
                    m_ref, acc_ref, first)

    n_far = jnp.maximum(qi - 1, 0)
    n_pairs = n_far // 2

    def first_pair():
        for traced_alongside in with_first_block:
            traced_alongside()
        block([qi - 1, qi], lambda m: [near_bias(m, 1, qi - 1), near_bias(m, 0, qi)], True, QK_AHEAD_PAIR, False,
              [0, 1], QK_AHEAD_PAIR)

    if may_be_first_tile:
        @pl.when(qi == 0)
        def _():
            block([qi], lambda m: [near_bias(m, 0, qi)], True, QK_AHEAD, False, [], 0)

        pl.when(qi >= 1)(first_pair)
    else:
        first_pair()

    @pl.when(n_far % 2 == 1)
    def _():
        block([n_far - 1], lambda m: [far_bias(m, n_far - 1)], False, QK_AHEAD, False, [], 0)

    def far_pair(p, carry):
        j, jn = 2 * p, 2 * jnp.minimum(p + 1, n_pairs - 1)
        block([j, j + 1], lambda m: [far_bias(m, j), far_bias(m, j + 1)], False, QK_AHEAD_PAIR, True,
              [jn, jn + 1], QK_AHEAD_PAIR)
        return carry

    lax.fori_loop(0, n_pairs, far_pair, 0)


def _query_tiles(step_body):
    first = pl.program_id(2) * Q_TILES_PER_STEP
    pending = []
    for qs in range(Q_TILES_PER_STEP):
        sweep, finish = step_body(qs, first + qs)
        sweep(pending)
        pending = [finish]
    pending[0]()


def _diff_kernel(qT_ref, k_ref, vT_ref, bias_ref, lam_ref, g_ref, o_ref,
                 qz_ref, m_ref, pre_ref, acc_ref, *, lam_init):
    def tile(qs, qi):
        qz, m_run, acc = qz_ref.at[qs], m_ref.at[qs], acc_ref.at[qs]

        def sweep(extra):
            if qs == 0:
                _split_pairs(qT_ref, qs, qz)
                first = []
            else:
                first = [functools.partial(_split_pairs, qT_ref, qs, qz)]
            _sweep(qi, k_ref, vT_ref, lambda v, m: v,
                   lambda m, t, j: (bias_ref[m, t], None), lambda m, j: (None, None),
                   (qz, m_run, acc), pre_ref, with_first_block=first + extra, may_be_first_tile=qs == 0)

        def finish():
            lp = lam_ref[...]
            lam = (jnp.exp(jnp.sum(lp[0:1] * lp[1:2], axis=-1, keepdims=True))
                   - jnp.exp(jnp.sum(lp[2:3] * lp[3:4], axis=-1, keepdims=True)) + lam_init)

            def weighted(m, scale):
                return acc[m, :PAIR, :] * (scale / acc[m, PAIR:PAIR + 1, :])

            for g in range(PAIRS_PER_STEP):
                o = weighted(2 * g, 1.0) - weighted(2 * g + 1, lam)
                o = o * lax.rsqrt(jnp.mean(o * o, axis=0, keepdims=True) + SUBLN_EPS) * g_ref[...]
                o_ref[0, qs * ATT_TILE:(qs + 1) * ATT_TILE, g * PAIR:(g + 1) * PAIR] = o.T.astype(BF16)

        return sweep, finish

    _query_tiles(tile)


def _moba_kernel(qT_ref, k_ref, vT_ref, bias_ref, o_ref,
                 qz_ref, m_ref, pre_ref, acc_ref, kmean_ref, selb_ref, *, nk):
    tq = qT_ref.shape[3]
    head_rows = HEAD_DIM + ONES_ROWS

    def tile(qs, qi):
        qz, m_run, acc = qz_ref.at[qs], m_ref.at[qs], acc_ref.at[qs]

        def select_blocks():
            blk = lax.broadcasted_iota(jnp.int32, (nk, tq), 0).astype(F32)
            eligible = blk < qi.astype(F32)
            for m in range(2 * PAIRS_PER_STEP):
                rest = kmean_ref[:, (m // 2) * PAIR:(m // 2 + 1) * PAIR]
                gate = jnp.zeros((nk, tq), F32)
                for _ in range(KMEAN_TERMS):
                    term = rest.astype(BF16)
                    gate = gate + jnp.dot(term, qz[m], preferred_element_type=F32)
                    rest = rest - term.astype(F32)
                gate = jnp.where(eligible, gate, -jnp.inf)
                picked = jnp.zeros((nk, tq), F32)
                for _ in range(MOBA_TOPK):
                    best = jnp.max(gate, axis=0, keepdims=True)
                    pick = blk == jnp.min(jnp.where(gate == best, blk, float(nk)), axis=0, keepdims=True)
                    picked = jnp.where(pick, 1.0, picked)
                    gate = jnp.where(pick, -jnp.inf, gate)
                selb_ref[m] = jnp.where(eligible, jnp.where(picked > 0.0, 0.0, MASK_VALUE), MASK_VALUE)

        def sweep(extra):
            if qs == 0:
                @pl.when(qi == 0)
                def _():
                    for j in range(nk):
                        kb = k_ref[0, j * MOBA_BLOCK:(j + 1) * MOBA_BLOCK, :].astype(F32)
                        kmean_ref[j:j + 1, :] = jnp.mean(kb, axis=0, keepdims=True)

                _split_pairs(qT_ref, qs, qz)
                first = [select_blocks]
            else:
                first = [functools.partial(_split_pairs, qT_ref, qs, qz), select_blocks]
            _sweep(qi, k_ref, vT_ref, lambda v, m: v[(m % 2) * head_rows:(m % 2 + 1) * head_rows],
                   lambda m, t, j: (bias_ref[m, t], None if t == 0 else selb_ref[m, pl.ds(j, 1), :]),
                   lambda m, j: (None, selb_ref[m, pl.ds(j, 1), :]),
                   (qz, m_run, acc), pre_ref, with_first_block=first + extra, may_be_first_tile=qs == 0)

        def finish():
            for g in range(PAIRS_PER_STEP):
                o = jnp.concatenate([acc[m, :HEAD_DIM, :] * (1.0 / acc[m, HEAD_DIM:HEAD_DIM + 1, :])
                                     for m in (2 * g, 2 * g + 1)], axis=0)
                o_ref[0, qs * ATT_TILE:(qs + 1) * ATT_TILE, g * PAIR:(g + 1) * PAIR] = o.T.astype(BF16)

        return sweep, finish

    _query_tiles(tile)


def _attention(kind, qT, k, vT, bias, extra, *, lam_init=None):
    B, nk, D, t = qT.shape
    G, S, T = PAIRS_PER_STEP, nk * t, Q_TILES_PER_STEP
    n_maps = 2 * G
    assert D % (G * PAIR) == 0 and nk >= 2 and nk % T == 0 and t == ATT_TILE and QK_AHEAD_PAIR <= QK_AHEAD <= n_maps
    in_specs = [pl.BlockSpec((1, T, G * PAIR, t), lambda b, p, q: (b, q, p, 0)),
                pl.BlockSpec((1, S, G * PAIR), lambda b, p, q: (b, 0, p)),
                pl.BlockSpec((1, G, nk, vT.shape[3], t), lambda b, p, q: (b, p, 0, 0, 0)),
                pl.BlockSpec((n_maps, 2, t, t), lambda b, p, q: (p, 0, 0, 0), pipeline_mode=pl.Buffered(1))]
    scratch = [pltpu.VMEM((T, n_maps, PAIR, t), BF16),
               pltpu.VMEM((T, n_maps, 1, t), F32),
               pltpu.VMEM((max(QK_AHEAD, QK_AHEAD_PAIR), 2, t, t), F32)]
    if kind == "diff":
        body = functools.partial(_diff_kernel, lam_init=lam_init)
        in_specs += [pl.BlockSpec(e.shape, lambda b, p, q: (0, 0)) for e in extra]
        scratch += [pltpu.VMEM((T, n_maps, PAIR + ONES_ROWS, t), F32)]
    else:
        body = functools.partial(_moba_kernel, nk=nk)
        scratch += [pltpu.VMEM((T, n_maps, HEAD_DIM + ONES_ROWS, t), F32),
                    pltpu.VMEM((nk, G * PAIR), F32),
                    pltpu.VMEM((n_maps, nk, t), F32)]
    return pl.pallas_call(
        body,
        grid=(B, D // (G * PAIR), nk // T),
        in_specs=in_specs,
        out_specs=pl.BlockSpec((1, T * t, G * PAIR), lambda b, p, q: (b, q, p)),
        out_shape=jax.ShapeDtypeStruct((B, S, D), BF16),
        scratch_shapes=scratch,
        compiler_params=pltpu.CompilerParams(dimension_semantics=("parallel", "parallel", "arbitrary"),
                                             vmem_limit_bytes=VMEM_LIMIT_BYTES),
        name=kind + "_attention",
    )(qT, k, vT, bias, *extra)


def kernel(x, rel_bias, norm_g, final_norm_g, ffn_w_in, ffn_w_out, diff_w_qkv, diff_lambda,
           diff_subln_g, diff_w_o, moba_w_qkv, moba_w_o):
    B, S, D = x.shape
    depth = norm_g.shape[0]
    assert S % TOKEN_TILE == 0 and D % PAIR == 0 and MOBA_BLOCK == ATT_TILE
    assert rel_bias.shape == (REL_BUCKETS, D // HEAD_DIM)
    bias = _bias_tiles(rel_bias, ATT_TILE)

    w_in_all, w_out_all = ffn_w_in.astype(BF16), ffn_w_out.astype(BF16)
    h = x.reshape(B * S, D)
    for i in range(depth):
        g = norm_g[i]
        h = _ffn(h, g[0], w_in_all, w_out_all, (i, 0))
        j = i // 2
        if i % 2 == 0:
            qT, k, vT = _qkv_proj(h.reshape(B, S, D), g[1], diff_w_qkv[j], PAIR)
            lam_init = 0.8 - 0.6 * math.exp(-0.3 * i)
            g_sub = jnp.broadcast_to((diff_subln_g[j].astype(F32) * (1.0 - lam_init))[:, None], (PAIR, ATT_TILE))
            o = _attention("diff", qT, k, vT, bias, (diff_lambda[j].astype(F32), g_sub), lam_init=lam_init)
            w_o = diff_w_o[j]
        else:
            qT, k, vT = _qkv_proj(h.reshape(B, S, D), g[1], moba_w_qkv[j], HEAD_DIM)
            o = _attention("moba", qT, k, vT, bias, ())
            w_o = moba_w_o[j]
        last = i == depth - 1
        h = _ffn(h, g[2], w_in_all, w_out_all, (i, 1), proj=(o.reshape(B * S, D), w_o),
                 final_g=final_norm_g if last else None)
    return h.reshape(B, S, D)
```

```python
import functools
import math

import numpy as np
import jax
import jax.numpy as jnp
from jax import lax
from jax.experimental import pallas as pl
from jax.experimental.pallas import tpu as pltpu

HEAD_DIM = 64
PAIR = 2 * HEAD_DIM
MOBA_BLOCK = 256
MOBA_TOPK = 3
KMEAN_TERMS = 3
ONES_ROWS = 16
REL_BUCKETS = 32
REL_MAX_DIST = 128
FFN_RESIDUAL = 0.5
RMS_EPS = 1e-6
SUBLN_EPS = 1e-5

ATT_TILE = 256
PAIRS_PER_STEP = 8
Q_TILES_PER_STEP = 2
QK_AHEAD = 4
QK_AHEAD_PAIR = 2
FFN_CHUNK = 256
TOKEN_TILE = 1024
FFN_TOKEN_TILE = 1024
MASK_VALUE = -1e30
LOG2E = math.log2(math.e)
VMEM_LIMIT_BYTES = 62 * 1024 * 1024

F32 = jnp.float32
BF16 = jnp.bfloat16
_TT = (((0,), (1,)), ((), ()))


def _rms(x, g, eps):
    return x * lax.rsqrt(jnp.mean(x * x, axis=-1, keepdims=True) + eps) * g


def _const_spec(shape):
    return pl.BlockSpec(shape, lambda *_: (0,) * len(shape), pipeline_mode=pl.Buffered(1))


def _rel_bucket_np(dist):
    n = np.maximum(dist, 0)
    max_exact = REL_BUCKETS // 2
    nf = np.maximum(n, 1).astype(np.float32)
    large = max_exact + (np.log(nf / np.float32(max_exact)) / np.float32(math.log(REL_MAX_DIST / max_exact))
                         * np.float32(REL_BUCKETS - max_exact)).astype(np.int32)
    large = np.minimum(large, REL_BUCKETS - 1)
    return np.where(n < max_exact, n, large).astype(np.int32)


def _bucket_rows(t):
    u = np.arange(2 * t)
    diag = np.where(u >= t, _rel_bucket_np(u - t), -1)
    prev = _rel_bucket_np(u)
    assert _rel_bucket_np(np.arange(t + 1, 8 * t)).min() == REL_BUCKETS - 1
    return np.stack([diag, prev]).astype(np.int32)[:, None, :]


def _bias_kernel(rb_ref, idx_ref, out_ref):
    m = pl.program_id(0)
    t = out_ref.shape[2]
    far = rb_ref[REL_BUCKETS - 1, m]
    for tile in range(2):
        idx = idx_ref[tile]
        row = jnp.zeros(idx.shape, F32)
        for b in range(REL_BUCKETS - 1):
            row = jnp.where(idx == b, (rb_ref[b, m] - far) * LOG2E, row)
        row = jnp.where(idx < 0, MASK_VALUE, row)
        rolled = pltpu.roll(jnp.broadcast_to(row, (t, 2 * t)), 0, 1, stride=1, stride_axis=0)
        out_ref[0, tile] = rolled[:, t:]


def _bias_tiles(rel_bias, t):
    n_maps = rel_bias.shape[1]
    idx = jnp.asarray(_bucket_rows(t))
    return pl.pallas_call(
        _bias_kernel,
        grid=(n_maps,),
        in_specs=[pl.BlockSpec(memory_space=pltpu.SMEM),
                  pl.BlockSpec((2, 1, 2 * t), lambda m: (0, 0, 0))],
        out_specs=pl.BlockSpec((1, 2, t, t), lambda m: (m, 0, 0, 0)),
        out_shape=jax.ShapeDtypeStruct((n_maps, 2, t, t), F32),
        name="rel_bias_tiles",
    )(rel_bias.astype(F32), idx)


def _ffn_kernel(*refs, n_chunks, has_proj, has_final):
    refs = list(refs)
    x_ref = refs.pop(0)
    if has_proj:
        o_ref, wo_ref = refs.pop(0), refs.pop(0)
    g_ref, win_ref, wout_ref = refs.pop(0), refs.pop(0), refs.pop(0)
    if has_final:
        gf_ref = refs.pop(0)
    (out_ref,) = refs

    x = x_ref[...]
    if has_proj:
        x = x + jnp.dot(o_ref[...], wo_ref[...], preferred_element_type=F32)
    hn = _rms(x, g_ref[...], RMS_EPS).astype(BF16)
    acc = None
    d_ff = n_chunks * FFN_CHUNK
    for c in range(n_chunks):
        cols = slice(c * FFN_CHUNK, (c + 1) * FFN_CHUNK)
        gate = jnp.dot(hn, win_ref[:, cols], preferred_element_type=F32)
        up = jnp.dot(hn, win_ref[:, d_ff + c * FFN_CHUNK:d_ff + (c + 1) * FFN_CHUNK], preferred_element_type=F32)
        a = (gate * (1.0 / (1.0 + jnp.exp(-gate))) * up).astype(BF16)
        part = jnp.dot(a, wout_ref[cols, :], preferred_element_type=F32)
        acc = part if acc is None else acc + part
    y = x + FFN_RESIDUAL * acc
    if has_final:
        y = _rms(y, gf_ref[...], RMS_EPS)
    out_ref[...] = y


def _stacked_spec(stack, index):
    lead = len(index)
    return pl.BlockSpec((None,) * lead + stack.shape[lead:], lambda *_: tuple(index) + (0, 0),
                        pipeline_mode=pl.Buffered(1))


def _ffn(x2d, g, w_in_all, w_out_all, which, proj=None, final_g=None):
    T, D = x2d.shape
    F = w_out_all.shape[-2]
    n_chunks = F // FFN_CHUNK
    assert n_chunks * FFN_CHUNK == F and w_in_all.shape[-2:] == (D, 2 * F)
    tm = math.gcd(T, FFN_TOKEN_TILE)

    row = lambda i: (i, 0)
    args, specs = [x2d], [pl.BlockSpec((tm, D), row)]
    if proj is not None:
        o2d, w_o = proj
        args += [o2d, w_o.astype(BF16)]
        specs += [pl.BlockSpec((tm, o2d.shape[1]), row), _const_spec(w_o.shape)]
    args += [g.reshape(1, D).astype(F32), w_in_all, w_out_all]
    specs += [_const_spec((1, D)), _stacked_spec(w_in_all, which), _stacked_spec(w_out_all, which)]
    if final_g is not None:
        args.append(final_g.reshape(1, D).astype(F32))
        specs.append(_const_spec((1, D)))

    return pl.pallas_call(
        functools.partial(_ffn_kernel, n_chunks=n_chunks, has_proj=proj is not None,
                          has_final=final_g is not None),
        grid=(T // tm,),
        in_specs=specs,
        out_specs=pl.BlockSpec((tm, D), row),
        out_shape=jax.ShapeDtypeStruct((T, D), F32),
        compiler_params=pltpu.CompilerParams(dimension_semantics=("parallel",),
                                             vmem_limit_bytes=VMEM_LIMIT_BYTES),
        name="ffn",
    )(*args)


def _proj_kernel(x_ref, g_ref, w_ref, qT_ref, k_ref, vT_ref, *, n_pairs, n_sub, v_dim):
    D = x_ref.shape[2]
    hn = _rms(x_ref[0], g_ref[...], RMS_EPS).astype(BF16)
    k_ref[0] = jnp.dot(hn, w_ref[:, D:2 * D], preferred_element_type=F32).astype(BF16)
    qT = lax.dot_general(w_ref[:, :D], hn, _TT, preferred_element_type=F32) * (HEAD_DIM ** -0.5 * LOG2E)
    qT = qT.astype(BF16)
    for c in range(n_sub):
        qT_ref[0, c] = qT[:, c * ATT_TILE:(c + 1) * ATT_TILE]
    vT = lax.dot_general(w_ref[:, 2 * D:], hn, _TT, preferred_element_type=F32).astype(BF16)
    ones_rows = jnp.where(lax.broadcasted_iota(jnp.int32, (ONES_ROWS, ATT_TILE), 0) == 0, 1.0, 0.0).astype(BF16)
    group = v_dim + ONES_ROWS
    for p in range(n_pairs):
        for c in range(n_sub):
            cols = slice(c * ATT_TILE, (c + 1) * ATT_TILE)
            for i in range(PAIR // v_dim):
                vT_ref[0, p, c, i * group:i * group + v_dim, :] = vT[p * PAIR + i * v_dim:p * PAIR + (i + 1) * v_dim, cols]
                vT_ref[0, p, c, i * group + v_dim:(i + 1) * group, :] = ones_rows


def _v_rows(v_dim):
    return (PAIR // v_dim) * (v_dim + ONES_ROWS)


def _qkv_proj(h, g, w_qkv, v_dim):
    B, S, D = h.shape
    tm = TOKEN_TILE
    n_pairs, n_sub, nk, rows = D // PAIR, tm // ATT_TILE, S // ATT_TILE, _v_rows(v_dim)
    return pl.pallas_call(
        functools.partial(_proj_kernel, n_pairs=n_pairs, n_sub=n_sub, v_dim=v_dim),
        grid=(B, S // tm),
        in_specs=[pl.BlockSpec((1, tm, D), lambda b, s: (b, s, 0)),
                  _const_spec((1, D)), _const_spec((D, 3 * D))],
        out_specs=[pl.BlockSpec((1, n_sub, D, ATT_TILE), lambda b, s: (b, s, 0, 0)),
                   pl.BlockSpec((1, tm, D), lambda b, s: (b, s, 0)),
                   pl.BlockSpec((1, n_pairs, n_sub, rows, ATT_TILE), lambda b, s: (b, 0, s, 0, 0))],
        out_shape=[jax.ShapeDtypeStruct((B, nk, D, ATT_TILE), BF16),
                   jax.ShapeDtypeStruct((B, S, D), BF16),
                   jax.ShapeDtypeStruct((B, n_pairs, nk, rows, ATT_TILE), BF16)],
        compiler_params=pltpu.CompilerParams(dimension_semantics=("parallel", "parallel"),
                                             vmem_limit_bytes=VMEM_LIMIT_BYTES),
        name="qkv_proj",
    )(h, g.reshape(1, D).astype(F32), w_qkv.astype(BF16))


def _split_pairs(qT_ref, qs, qz_ref):
    zeros = jnp.zeros((HEAD_DIM, qT_ref.shape[3]), qT_ref.dtype)
    for g in range(PAIRS_PER_STEP):
        q = qT_ref[0, qs, g * PAIR:(g + 1) * PAIR, :]
        qz_ref[2 * g, :HEAD_DIM, :] = q[:HEAD_DIM]
        qz_ref[2 * g, HEAD_DIM:, :] = zeros
        qz_ref[2 * g + 1, :HEAD_DIM, :] = zeros
        qz_ref[2 * g + 1, HEAD_DIM:, :] = q[HEAD_DIM:]


def _update(m, scores, vT_tiles, biases, m_ref, acc_ref, first):
    scores = [s if b is None else b + s for s, (b, _) in zip(scores, biases)]
    masks = [r for _, r in biases]
    if all(r is None for r in masks):
        m_cur = jnp.max(functools.reduce(jnp.maximum, scores), axis=0, keepdims=True)
    else:
        m_cur = functools.reduce(jnp.maximum, [jnp.max(s, axis=0, keepdims=True) + (0.0 if r is None else r)
                                               for s, r in zip(scores, masks)])
    if first:
        m_new = m_cur
    else:
        m_old = m_ref[m]
        m_new = jnp.maximum(m_old, m_cur)
    shifts = [m_new if r is None else m_new - r for r in masks]
    pv = functools.reduce(jnp.add, [jnp.dot(v, jnp.exp2(s - sh).astype(BF16), preferred_element_type=F32)
                                    for v, s, sh in zip(vT_tiles, scores, shifts)])
    if first:
        acc_ref[m] = pv
    else:
        acc_ref[m] = jnp.exp2(m_old - m_new) * acc_ref[m] + pv
    m_ref[m] = m_new


def _sweep(qi, k_ref, vT_ref, v_rows, near_bias, far_bias, state, pre_ref, with_first_block=(), may_be_first_tile=True):
    qz_ref, m_ref, acc_ref = state
    n_maps = 2 * PAIRS_PER_STEP

    def raw_scores(j, m):
        rows = pl.ds(pl.multiple_of(j * ATT_TILE, ATT_TILE), ATT_TILE)
        k_t = k_ref[0, rows, (m // 2) * PAIR:(m // 2 + 1) * PAIR]
        return jnp.dot(k_t, qz_ref[m], preferred_element_type=F32)

    def block(tiles, biases, first, ahead, preloaded, next_tiles, next_ahead):
        scores = {}
        for m in range(ahead):
            scores[m] = ([pre_ref[m, t] for t in range(len(tiles))] if preloaded
                         else [raw_scores(j, m) for j in tiles])
        for m in range(n_maps):
            nxt = m + ahead
            if nxt < n_maps:
                scores[nxt] = [raw_scores(j, nxt) for j in tiles]
            elif nxt - n_maps < next_ahead:
                for t, j in enumerate(next_tiles):
                    pre_ref[nxt - n_maps, t] = raw_scores(j, nxt - n_maps)
            _update(m, scores.pop(m), [v_rows(vT_ref[0, m // 2, j], m) for j in tiles], biases(m),
                    m_ref, acc_ref, first)

    n_far = jnp.maximum(qi - 1, 0)
    n_pairs = n_far // 2

    def first_pair():
        for traced_alongside in with_first_block:
            traced_alongside()
        block([qi - 1, qi], lambda m: [near_bias(m, 1, qi - 1), near_bias(m, 0, qi)], True, QK_AHEAD_PAIR, False,
              [0, 1], QK_AHEAD_PAIR)

    if may_be_first_tile:
        @pl.when(qi == 0)
        def _():
            block([qi], lambda m: [near_bias(m, 0, qi)], True, QK_AHEAD, False, [], 0)

        pl.when(qi >= 1)(first_pair)
    else:
        first_pair()

    @pl.when(n_far % 2 == 1)
    def _():
        block([n_far - 1], lambda m: [far_bias(m, n_far - 1)], False, QK_AHEAD, False, [], 0)

    def far_pair(p, carry):
        j, jn = 2 * p, 2 * jnp.minimum(p + 1, n_pairs - 1)
        block([j, j + 1], lambda m: [far_bias(m, j), far_bias(m, j + 1)], False, QK_AHEAD_PAIR, True,
              [jn, jn + 1], QK_AHEAD_PAIR)
        return carry

    lax.fori_loop(0, n_pairs, far_pair, 0)


def _query_tiles(step_body):
    first = pl.program_id(2) * Q_TILES_PER_STEP
    pending = []
    for qs in range(Q_TILES_PER_STEP):
        sweep, finish = step_body(qs, first + qs)
        sweep(pending)
        pending = [finish]
    pending[0]()


def _diff_kernel(qT_ref, k_ref, vT_ref, bias_ref, lam_ref, g_ref, o_ref,
                 qz_ref, m_ref, pre_ref, acc_ref, *, lam_init):
    def tile(qs, qi):
        qz, m_run, acc = qz_ref.at[qs], m_ref.at[qs], acc_ref.at[qs]

        def sweep(extra):
            if qs == 0:
                _split_pairs(qT_ref, qs, qz)
                first = []
            else:
                first = [functools.partial(_split_pairs, qT_ref, qs, qz)]
            _sweep(qi, k_ref, vT_ref, lambda v, m: v,
                   lambda m, t, j: (bias_ref[m, t], None), lambda m, j: (None, None),
                   (qz, m_run, acc), pre_ref, with_first_block=first + extra, may_be_first_tile=qs == 0)

        def finish():
            lp = lam_ref[...]
            lam = (jnp.exp(jnp.sum(lp[0:1] * lp[1:2], axis=-1, keepdims=True))
                   - jnp.exp(jnp.sum(lp[2:3] * lp[3:4], axis=-1, keepdims=True)) + lam_init)

            def weighted(m, scale):
                return acc[m, :PAIR, :] * (scale / acc[m, PAIR:PAIR + 1, :])

            for g in range(PAIRS_PER_STEP):
                o = weighted(2 * g, 1.0) - weighted(2 * g + 1, lam)
                o = o * lax.rsqrt(jnp.mean(o * o, axis=0, keepdims=True) + SUBLN_EPS) * g_ref[...]
                o_ref[0, qs * ATT_TILE:(qs + 1) * ATT_TILE, g * PAIR:(g + 1) * PAIR] = o.T.astype(BF16)

        return sweep, finish

    _query_tiles(tile)


def _moba_kernel(qT_ref, k_ref, vT_ref, bias_ref, o_ref,
                 qz_ref, m_ref, pre_ref, acc_ref, kmean_ref, kgate_ref, selb_ref, *, nk):
    tq = qT_ref.shape[3]
    head_rows = HEAD_DIM + ONES_ROWS

    def tile(qs, qi):
        qz, m_run, acc = qz_ref.at[qs], m_ref.at[qs], acc_ref.at[qs]

        def select_blocks():
            blk = lax.broadcasted_iota(jnp.int32, (nk, tq), 0).astype(F32)
            eligible = blk < qi.astype(F32)
            for m in range(2 * PAIRS_PER_STEP):
                if m % 2 == 0:
                    q_pair = qT_ref[0, qs, (m // 2) * PAIR:(m // 2 + 1) * PAIR, :]
                    gates = jnp.dot(kgate_ref[m // 2], q_pair, preferred_element_type=F32)
                base = (m % 2) * KMEAN_TERMS * nk
                gate = functools.reduce(jnp.add, [gates[base + t * nk:base + (t + 1) * nk]
                                                  for t in range(KMEAN_TERMS)])
                gate = jnp.where(eligible, gate, -jnp.inf)
                picked = jnp.zeros((nk, tq), F32)
                for _ in range(MOBA_TOPK):
                    best = jnp.max(gate, axis=0, keepdims=True)
                    pick = blk == jnp.min(jnp.where(gate == best, blk, float(nk)), axis=0, keepdims=True)
                    picked = jnp.where(pick, 1.0, picked)
                    gate = jnp.where(pick, -jnp.inf, gate)
                selb_ref[m] = jnp.where(eligible, jnp.where(picked > 0.0, 0.0, MASK_VALUE), MASK_VALUE)

        def sweep(extra):
            if qs == 0:
                @pl.when(qi == 0)
                def _():
                    for j in range(nk):
                        kb = k_ref[0, j * MOBA_BLOCK:(j + 1) * MOBA_BLOCK, :].astype(F32)
                        kmean_ref[j:j + 1, :] = jnp.mean(kb, axis=0, keepdims=True)
                    lane = lax.broadcasted_iota(jnp.int32, (nk, PAIR), 1)
                    for g in range(PAIRS_PER_STEP):
                        mean = kmean_ref[:, g * PAIR:(g + 1) * PAIR]
                        for i in range(2):
                            rest = jnp.where((lane >= i * HEAD_DIM) & (lane < (i + 1) * HEAD_DIM), mean, 0.0)
                            for t in range(KMEAN_TERMS):
                                term = rest.astype(BF16)
                                row = (i * KMEAN_TERMS + t) * nk
                                kgate_ref[g, row:row + nk, :] = term
                                rest = rest - term.astype(F32)

                _split_pairs(qT_ref, qs, qz)
                first = [select_blocks]
            else:
                first = [functools.partial(_split_pairs, qT_ref, qs, qz), select_blocks]
            _sweep(qi, k_ref, vT_ref, lambda v, m: v[(m % 2) * head_rows:(m % 2 + 1) * head_rows],
                   lambda m, t, j: (bias_ref[m, t], None if t == 0 else selb_ref[m, pl.ds(j, 1), :]),
                   lambda m, j: (None, selb_ref[m, pl.ds(j, 1), :]),
                   (qz, m_run, acc), pre_ref, with_first_block=first + extra, may_be_first_tile=qs == 0)

        def finish():
            for g in range(PAIRS_PER_STEP):
                o = jnp.concatenate([acc[m, :HEAD_DIM, :] * (1.0 / acc[m, HEAD_DIM:HEAD_DIM + 1, :])
                                     for m in (2 * g, 2 * g + 1)], axis=0)
                o_ref[0, qs * ATT_TILE:(qs + 1) * ATT_TILE, g * PAIR:(g + 1) * PAIR] = o.T.astype(BF16)

        return sweep, finish

    _query_tiles(tile)


def _attention(kind, qT, k, vT, bias, extra, *, lam_init=None):
    B, nk, D, t = qT.shape
    G, S, T = PAIRS_PER_STEP, nk * t, Q_TILES_PER_STEP
    n_maps = 2 * G
    assert D % (G * PAIR) == 0 and nk >= 2 and nk % T == 0 and t == ATT_TILE and QK_AHEAD_PAIR <= QK_AHEAD <= n_maps
    in_specs = [pl.BlockSpec((1, T, G * PAIR, t), lambda b, p, q: (b, q, p, 0)),
                pl.BlockSpec((1, S, G * PAIR), lambda b, p, q: (b, 0, p)),
                pl.BlockSpec((1, G, nk, vT.shape[3], t), lambda b, p, q: (b, p, 0, 0, 0)),
                pl.BlockSpec((n_maps, 2, t, t), lambda b, p, q: (p, 0, 0, 0), pipeline_mode=pl.Buffered(1))]
    scratch = [pltpu.VMEM((T, n_maps, PAIR, t), BF16),
               pltpu.VMEM((T, n_maps, 1, t), F32),
               pltpu.VMEM((max(QK_AHEAD, QK_AHEAD_PAIR), 2, t, t), F32)]
    if kind == "diff":
        body = functools.partial(_diff_kernel, lam_init=lam_init)
        in_specs += [pl.BlockSpec(e.shape, lambda b, p, q: (0, 0)) for e in extra]
        scratch += [pltpu.VMEM((T, n_maps, PAIR + ONES_ROWS, t), F32)]
    else:
        body = functools.partial(_moba_kernel, nk=nk)
        scratch += [pltpu.VMEM((T, n_maps, HEAD_DIM + ONES_ROWS, t), F32),
                    pltpu.VMEM((nk, G * PAIR), F32),
                    pltpu.VMEM((G, 2 * KMEAN_TERMS * nk, PAIR), BF16),
                    pltpu.VMEM((n_maps, nk, t), F32)]
    return pl.pallas_call(
        body,
        grid=(B, D // (G * PAIR), nk // T),
        in_specs=in_specs,
        out_specs=pl.BlockSpec((1, T * t, G * PAIR), lambda b, p, q: (b, q, p)),
        out_shape=jax.ShapeDtypeStruct((B, S, D), BF16),
        scratch_shapes=scratch,
        compiler_params=pltpu.CompilerParams(dimension_semantics=("parallel", "parallel", "arbitrary"),
                                             vmem_limit_bytes=VMEM_LIMIT_BYTES),
        name=kind + "_attention",
    )(qT, k, vT, bias, *extra)


def kernel(x, rel_bias, norm_g, final_norm_g, ffn_w_in, ffn_w_out, diff_w_qkv, diff_lambda,
           diff_subln_g, diff_w_o, moba_w_qkv, moba_w_o):
    B, S, D = x.shape
    depth = norm_g.shape[0]
    assert S % TOKEN_TILE == 0 and D % PAIR == 0 and MOBA_BLOCK == ATT_TILE
    assert rel_bias.shape == (REL_BUCKETS, D // HEAD_DIM)
    bias = _bias_tiles(rel_bias, ATT_TILE)

    w_in_all, w_out_all = ffn_w_in.astype(BF16), ffn_w_out.astype(BF16)
    h = x.reshape(B * S, D)
    for i in range(depth):
        g = norm_g[i]
        h = _ffn(h, g[0], w_in_all, w_out_all, (i, 0))
        j = i // 2
        if i % 2 == 0:
            qT, k, vT = _qkv_proj(h.reshape(B, S, D), g[1], diff_w_qkv[j], PAIR)
            lam_init = 0.8 - 0.6 * math.exp(-0.3 * i)
            g_sub = jnp.broadcast_to((diff_subln_g[j].astype(F32) * (1.0 - lam_init))[:, None], (PAIR, ATT_TILE))
            o = _attention("diff", qT, k, vT, bias, (diff_lambda[j].astype(F32), g_sub), lam_init=lam_init)
            w_o = diff_w_o[j]
        else:
            qT, k, vT = _qkv_proj(h.reshape(B, S, D), g[1], moba_w_qkv[j], HEAD_DIM)
            o = _attention("moba", qT, k, vT, bias, ())
            w_o = moba_w_o[j]
        last = i == depth - 1
        h = _ffn(h, g[2], w_in_all, w_out_all, (i, 1), proj=(o.reshape(B * S, D), w_o),
                 final_g=final_norm_g if last else None)
    return h.reshape(B, S, D)
```

```python
import functools
import math

import numpy as np
import jax
import jax.numpy as jnp
from jax import lax
from jax.experimental import pallas as pl
from jax.experimental.pallas import tpu as pltpu

HEAD_DIM = 64
PAIR = 2 * HEAD_DIM
MOBA_BLOCK = 256
MOBA_TOPK = 3
KMEAN_TERMS = 3
ONES_ROWS = 16
REL_BUCKETS = 32
REL_MAX_DIST = 128
FFN_RESIDUAL = 0.5
RMS_EPS = 1e-6
SUBLN_EPS = 1e-5

ATT_TILE = 256
PAIRS_PER_STEP = 8
Q_TILES_PER_STEP = 2
QK_AHEAD = 4
QK_AHEAD_PAIR = 2
FFN_CHUNK = 256
TOKEN_TILE = 1024
FFN_TOKEN_TILE = 1024
MASK_VALUE = -1e30
LOG2E = math.log2(math.e)
VMEM_LIMIT_BYTES = 62 * 1024 * 1024

F32 = jnp.float32
BF16 = jnp.bfloat16
_TT = (((0,), (1,)), ((), ()))


def _rms(x, g, eps):
    return x * lax.rsqrt(jnp.mean(x * x, axis=-1, keepdims=True) + eps) * g


def _const_spec(shape):
    return pl.BlockSpec(shape, lambda *_: (0,) * len(shape), pipeline_mode=pl.Buffered(1))


def _rel_bucket_np(dist):
    n = np.maximum(dist, 0)
    max_exact = REL_BUCKETS // 2
    nf = np.maximum(n, 1).astype(np.float32)
    large = max_exact + (np.log(nf / np.float32(max_exact)) / np.float32(math.log(REL_MAX_DIST / max_exact))
                         * np.float32(REL_BUCKETS - max_exact)).astype(np.int32)
    large = np.minimum(large, REL_BUCKETS - 1)
    return np.where(n < max_exact, n, large).astype(np.int32)


def _bucket_rows(t):
    u = np.arange(2 * t)
    diag = np.where(u >= t, _rel_bucket_np(u - t), -1)
    prev = _rel_bucket_np(u)
    assert _rel_bucket_np(np.arange(t + 1, 8 * t)).min() == REL_BUCKETS - 1
    return np.stack([diag, prev]).astype(np.int32)[:, None, :]


def _bias_kernel(rb_ref, idx_ref, out_ref):
    m = pl.program_id(0)
    t = out_ref.shape[2]
    far = rb_ref[REL_BUCKETS - 1, m]
    for tile in range(2):
        idx = idx_ref[tile]
        row = jnp.zeros(idx.shape, F32)
        for b in range(REL_BUCKETS - 1):
            row = jnp.where(idx == b, (rb_ref[b, m] - far) * LOG2E, row)
        row = jnp.where(idx < 0, MASK_VALUE, row)
        rolled = pltpu.roll(jnp.broadcast_to(row, (t, 2 * t)), 0, 1, stride=1, stride_axis=0)
        out_ref[0, tile] = rolled[:, t:]


def _bias_tiles(rel_bias, t):
    n_maps = rel_bias.shape[1]
    idx = jnp.asarray(_bucket_rows(t))
    return pl.pallas_call(
        _bias_kernel,
        grid=(n_maps,),
        in_specs=[pl.BlockSpec(memory_space=pltpu.SMEM),
                  pl.BlockSpec((2, 1, 2 * t), lambda m: (0, 0, 0))],
        out_specs=pl.BlockSpec((1, 2, t, t), lambda m: (m, 0, 0, 0)),
        out_shape=jax.ShapeDtypeStruct((n_maps, 2, t, t), F32),
        name="rel_bias_tiles",
    )(rel_bias.astype(F32), idx)


def _ffn_kernel(*refs, n_chunks, has_proj, has_final, n_cast):
    refs = list(refs)
    x_ref = refs.pop(0)
    if has_proj:
        o_ref, wo_ref = refs.pop(0), refs.pop(0)
    g_ref, win_ref, wout_ref = refs.pop(0), refs.pop(0), refs.pop(0)
    if has_final:
        gf_ref = refs.pop(0)
    cast_in = [refs.pop(0) for _ in range(n_cast)]
    out_ref = refs.pop(0)
    for src, dst in zip(cast_in, refs):
        dst[...] = src[...].astype(BF16)

    x = x_ref[...]
    if has_proj:
        x = x + jnp.dot(o_ref[...], wo_ref[...], preferred_element_type=F32)
    hn = _rms(x, g_ref[...], RMS_EPS).astype(BF16)
    acc = None
    d_ff = n_chunks * FFN_CHUNK
    for c in range(n_chunks):
        cols = slice(c * FFN_CHUNK, (c + 1) * FFN_CHUNK)
        gate = jnp.dot(hn, win_ref[:, cols], preferred_element_type=F32)
        up = jnp.dot(hn, win_ref[:, d_ff + c * FFN_CHUNK:d_ff + (c + 1) * FFN_CHUNK], preferred_element_type=F32)
        a = (gate * (1.0 / (1.0 + jnp.exp(-gate))) * up).astype(BF16)
        part = jnp.dot(a, wout_ref[cols, :], preferred_element_type=F32)
        acc = part if acc is None else acc + part
    y = x + FFN_RESIDUAL * acc
    if has_final:
        y = _rms(y, gf_ref[...], RMS_EPS)
    out_ref[...] = y


def _stacked_spec(stack, index):
    lead = len(index)
    return pl.BlockSpec((None,) * lead + stack.shape[lead:], lambda *_: tuple(index) + (0, 0),
                        pipeline_mode=pl.Buffered(1))


def _row_slabs(stack, index, n_steps):
    lead = len(index)
    R, C = stack.shape[lead:]
    rows = R // n_steps
    assert rows * n_steps == R and rows % 16 == 0, (stack.shape, n_steps)
    return (pl.BlockSpec((None,) * lead + (rows, C), lambda s: tuple(index) + (s, 0)),
            pl.BlockSpec((rows, C), lambda s: (s, 0)), jax.ShapeDtypeStruct((R, C), BF16))


def _ffn(x2d, g, w_in_all, w_out_all, which, proj=None, final_g=None, cast=()):
    T, D = x2d.shape
    F = w_out_all.shape[-2]
    n_chunks = F // FFN_CHUNK
    assert n_chunks * FFN_CHUNK == F and w_in_all.shape[-2:] == (D, 2 * F)
    tm = math.gcd(T, FFN_TOKEN_TILE)

    row = lambda i: (i, 0)
    args, specs = [x2d], [pl.BlockSpec((tm, D), row)]
    if proj is not None:
        o2d, w_o = proj
        args += [o2d, w_o]
        specs += [pl.BlockSpec((tm, o2d.shape[1]), row), _const_spec(w_o.shape)]
    args += [g.reshape(1, D).astype(F32), w_in_all, w_out_all]
    specs += [_const_spec((1, D)), _stacked_spec(w_in_all, which), _stacked_spec(w_out_all, which)]
    if final_g is not None:
        args.append(final_g.reshape(1, D).astype(F32))
        specs.append(_const_spec((1, D)))

    slabs = [_row_slabs(stack, index, T // tm) for stack, index in cast]
    args += [stack for stack, _ in cast]
    specs += [in_spec for in_spec, _, _ in slabs]

    return pl.pallas_call(
        functools.partial(_ffn_kernel, n_chunks=n_chunks, has_proj=proj is not None,
                          has_final=final_g is not None, n_cast=len(cast)),
        grid=(T // tm,),
        in_specs=specs,
        out_specs=[pl.BlockSpec((tm, D), row)] + [out_spec for _, out_spec, _ in slabs],
        out_shape=[jax.ShapeDtypeStruct((T, D), F32)] + [shape for _, _, shape in slabs],
        compiler_params=pltpu.CompilerParams(dimension_semantics=("parallel",),
                                             vmem_limit_bytes=VMEM_LIMIT_BYTES),
        name="ffn",
    )(*args)


def _proj_kernel(x_ref, g_ref, w_ref, qT_ref, k_ref, vT_ref, *, n_pairs, n_sub, v_dim):
    D = x_ref.shape[2]
    hn = _rms(x_ref[0], g_ref[...], RMS_EPS).astype(BF16)
    k_ref[0] = jnp.dot(hn, w_ref[:, D:2 * D], preferred_element_type=F32).astype(BF16)
    qT = lax.dot_general(w_ref[:, :D], hn, _TT, preferred_element_type=F32) * (HEAD_DIM ** -0.5 * LOG2E)
    qT = qT.astype(BF16)
    for c in range(n_sub):
        qT_ref[0, c] = qT[:, c * ATT_TILE:(c + 1) * ATT_TILE]
    vT = lax.dot_general(w_ref[:, 2 * D:], hn, _TT, preferred_element_type=F32).astype(BF16)
    ones_rows = jnp.where(lax.broadcasted_iota(jnp.int32, (ONES_ROWS, ATT_TILE), 0) == 0, 1.0, 0.0).astype(BF16)
    group = v_dim + ONES_ROWS
    for p in range(n_pairs):
        for c in range(n_sub):
            cols = slice(c * ATT_TILE, (c + 1) * ATT_TILE)
            for i in range(PAIR // v_dim):
                vT_ref[0, p, c, i * group:i * group + v_dim, :] = vT[p * PAIR + i * v_dim:p * PAIR + (i + 1) * v_dim, cols]
                vT_ref[0, p, c, i * group + v_dim:(i + 1) * group, :] = ones_rows


def _v_rows(v_dim):
    return (PAIR // v_dim) * (v_dim + ONES_ROWS)


def _qkv_proj(h, g, w_qkv, v_dim):
    B, S, D = h.shape
    tm = TOKEN_TILE
    n_pairs, n_sub, nk, rows = D // PAIR, tm // ATT_TILE, S // ATT_TILE, _v_rows(v_dim)
    return pl.pallas_call(
        functools.partial(_proj_kernel, n_pairs=n_pairs, n_sub=n_sub, v_dim=v_dim),
        grid=(B, S // tm),
        in_specs=[pl.BlockSpec((1, tm, D), lambda b, s: (b, s, 0)),
                  _const_spec((1, D)), _const_spec((D, 3 * D))],
        out_specs=[pl.BlockSpec((1, n_sub, D, ATT_TILE), lambda b, s: (b, s, 0, 0)),
                   pl.BlockSpec((1, tm, D), lambda b, s: (b, s, 0)),
                   pl.BlockSpec((1, n_pairs, n_sub, rows, ATT_TILE), lambda b, s: (b, 0, s, 0, 0))],
        out_shape=[jax.ShapeDtypeStruct((B, nk, D, ATT_TILE), BF16),
                   jax.ShapeDtypeStruct((B, S, D), BF16),
                   jax.ShapeDtypeStruct((B, n_pairs, nk, rows, ATT_TILE), BF16)],
        compiler_params=pltpu.CompilerParams(dimension_semantics=("parallel", "parallel"),
                                             vmem_limit_bytes=VMEM_LIMIT_BYTES),
        name="qkv_proj",
    )(h, g.reshape(1, D).astype(F32), w_qkv)


def _split_pairs(qT_ref, qs, qz_ref):
    zeros = jnp.zeros((HEAD_DIM, qT_ref.shape[3]), qT_ref.dtype)
    for g in range(PAIRS_PER_STEP):
        q = qT_ref[0, qs, g * PAIR:(g + 1) * PAIR, :]
        qz_ref[2 * g, :HEAD_DIM, :] = q[:HEAD_DIM]
        qz_ref[2 * g, HEAD_DIM:, :] = zeros
        qz_ref[2 * g + 1, :HEAD_DIM, :] = zeros
        qz_ref[2 * g + 1, HEAD_DIM:, :] = q[HEAD_DIM:]


def _update(m, scores, vT_tiles, biases, m_ref, acc_ref, first):
    scores = [s if b is None else b + s for s, (b, _) in zip(scores, biases)]
    masks = [r for _, r in biases]
    if all(r is None for r in masks):
        m_cur = jnp.max(functools.reduce(jnp.maximum, scores), axis=0, keepdims=True)
    else:
        m_cur = functools.reduce(jnp.maximum, [jnp.max(s, axis=0, keepdims=True) + (0.0 if r is None else r)
                                               for s, r in zip(scores, masks)])
    if first:
        m_new = m_cur
    else:
        m_old = m_ref[m]
        m_new = jnp.maximum(m_old, m_cur)
    shifts = [m_new if r is None else m_new - r for r in masks]
    pv = functools.reduce(jnp.add, [jnp.dot(v, jnp.exp2(s - sh).astype(BF16), preferred_element_type=F32)
                                    for v, s, sh in zip(vT_tiles, scores, shifts)])
    if first:
        acc_ref[m] = pv
    else:
        acc_ref[m] = jnp.exp2(m_old - m_new) * acc_ref[m] + pv
    m_ref[m] = m_new


def _sweep(qi, k_ref, vT_ref, v_rows, near_bias, far_bias, state, pre_ref, with_first_block=(), may_be_first_tile=True):
    qz_ref, m_ref, acc_ref = state
    n_maps = 2 * PAIRS_PER_STEP

    def raw_scores(j, m):
        rows = pl.ds(pl.multiple_of(j * ATT_TILE, ATT_TILE), ATT_TILE)
        k_t = k_ref[0, rows, (m // 2) * PAIR:(m // 2 + 1) * PAIR]
        return jnp.dot(k_t, qz_ref[m], preferred_element_type=F32)

    def block(tiles, biases, first, ahead, preloaded, next_tiles, next_ahead):
        scores = {}
        for m in range(ahead):
            scores[m] = ([pre_ref[m, t] for t in range(len(tiles))] if preloaded
                         else [raw_scores(j, m) for j in tiles])
        for m in range(n_maps):
            nxt = m + ahead
            if nxt < n_maps:
                scores[nxt] = [raw_scores(j, nxt) for j in tiles]
            elif nxt - n_maps < next_ahead:
                for t, j in enumerate(next_tiles):
                    pre_ref[nxt - n_maps, t] = raw_scores(j, nxt - n_maps)
            _update(m, scores.pop(m), [v_rows(vT_ref[0, m // 2, j], m) for j in tiles], biases(m),
                    m_ref, acc_ref, first)

    n_far = jnp.maximum(qi - 1, 0)
    n_pairs = n_far // 2

    def first_pair():
        for traced_alongside in with_first_block:
            traced_alongside()
        block([qi - 1, qi], lambda m: [near_bias(m, 1, qi - 1), near_bias(m, 0, qi)], True, QK_AHEAD_PAIR, False,
              [0, 1], QK_AHEAD_PAIR)

    if may_be_first_tile:
        @pl.when(qi == 0)
        def _():
            block([qi], lambda m: [near_bias(m, 0, qi)], True, QK_AHEAD, False, [], 0)

        pl.when(qi >= 1)(first_pair)
    else:
        first_pair()

    @pl.when(n_far % 2 == 1)
    def _():
        block([n_far - 1], lambda m: [far_bias(m, n_far - 1)], False, QK_AHEAD, False, [], 0)

    def far_pair(p, carry):
        j, jn = 2 * p, 2 * jnp.minimum(p + 1, n_pairs - 1)
        block([j, j + 1], lambda m: [far_bias(m, j), far_bias(m, j + 1)], False, QK_AHEAD_PAIR, True,
              [jn, jn + 1], QK_AHEAD_PAIR)
        return carry

    lax.fori_loop(0, n_pairs, far_pair, 0)


def _query_tiles(step_body):
    first = pl.program_id(2) * Q_TILES_PER_STEP
    pending = []
    for qs in range(Q_TILES_PER_STEP):
        sweep, finish = step_body(qs, first + qs)
        sweep(pending)
        pending = [finish]
    pending[0]()


def _diff_kernel(qT_ref, k_ref, vT_ref, bias_ref, lam_ref, g_ref, o_ref,
                 qz_ref, m_ref, pre_ref, acc_ref, *, lam_init):
    def tile(qs, qi):
        qz, m_run, acc = qz_ref.at[qs], m_ref.at[qs], acc_ref.at[qs]

        def sweep(extra):
            if qs == 0:
                _split_pairs(qT_ref, qs, qz)
                first = []
            else:
                first = [functools.partial(_split_pairs, qT_ref, qs, qz)]
            _sweep(qi, k_ref, vT_ref, lambda v, m: v,
                   lambda m, t, j: (bias_ref[m, t], None), lambda m, j: (None, None),
                   (qz, m_run, acc), pre_ref, with_first_block=first + extra, may_be_first_tile=qs == 0)

        def finish():
            lp = lam_ref[...]
            lam = (jnp.exp(jnp.sum(lp[0:1] * lp[1:2], axis=-1, keepdims=True))
                   - jnp.exp(jnp.sum(lp[2:3] * lp[3:4], axis=-1, keepdims=True)) + lam_init)

            def weighted(m, scale):
                return acc[m, :PAIR, :] * (scale / acc[m, PAIR:PAIR + 1, :])

            for g in range(PAIRS_PER_STEP):
                o = weighted(2 * g, 1.0) - weighted(2 * g + 1, lam)
                o = o * lax.rsqrt(jnp.mean(o * o, axis=0, keepdims=True) + SUBLN_EPS) * g_ref[...]
                o_ref[0, qs * ATT_TILE:(qs + 1) * ATT_TILE, g * PAIR:(g + 1) * PAIR] = o.T.astype(BF16)

        return sweep, finish

    _query_tiles(tile)


def _moba_kernel(qT_ref, k_ref, vT_ref, bias_ref, o_ref,
                 qz_ref, m_ref, pre_ref, acc_ref, kmean_ref, kgate_ref, selb_ref, *, nk):
    tq = qT_ref.shape[3]
    head_rows = HEAD_DIM + ONES_ROWS

    def tile(qs, qi):
        qz, m_run, acc = qz_ref.at[qs], m_ref.at[qs], acc_ref.at[qs]

        def select_blocks():
            blk = lax.broadcasted_iota(jnp.int32, (nk, tq), 0).astype(F32)
            eligible = blk < qi.astype(F32)
            for m in range(2 * PAIRS_PER_STEP):
                if m % 2 == 0:
                    q_pair = qT_ref[0, qs, (m // 2) * PAIR:(m // 2 + 1) * PAIR, :]
                    gates = jnp.dot(kgate_ref[m // 2], q_pair, preferred_element_type=F32)
                base = (m % 2) * KMEAN_TERMS * nk
                gate = functools.reduce(jnp.add, [gates[base + t * nk:base + (t + 1) * nk]
                                                  for t in range(KMEAN_TERMS)])
                gate = jnp.where(eligible, gate, -jnp.inf)
                picked = jnp.zeros((nk, tq), F32)
                for _ in range(MOBA_TOPK):
                    best = jnp.max(gate, axis=0, keepdims=True)
                    pick = blk == jnp.min(jnp.where(gate == best, blk, float(nk)), axis=0, keepdims=True)
                    picked = jnp.where(pick, 1.0, picked)
                    gate = jnp.where(pick, -jnp.inf, gate)
                selb_ref[m] = jnp.where(eligible, jnp.where(picked > 0.0, 0.0, MASK_VALUE), MASK_VALUE)

        def sweep(extra):
            if qs == 0:
                @pl.when(qi == 0)
                def _():
                    for j in range(nk):
                        kb = k_ref[0, j * MOBA_BLOCK:(j + 1) * MOBA_BLOCK, :].astype(F32)
                        kmean_ref[j:j + 1, :] = jnp.mean(kb, axis=0, keepdims=True)
                    lane = lax.broadcasted_iota(jnp.int32, (nk, PAIR), 1)
                    for g in range(PAIRS_PER_STEP):
                        mean = kmean_ref[:, g * PAIR:(g + 1) * PAIR]
                        for i in range(2):
                            rest = jnp.where((lane >= i * HEAD_DIM) & (lane < (i + 1) * HEAD_DIM), mean, 0.0)
                            for t in range(KMEAN_TERMS):
                                term = rest.astype(BF16)
                                row = (i * KMEAN_TERMS + t) * nk
                                kgate_ref[g, row:row + nk, :] = term
                                rest = rest - term.astype(F32)

                _split_pairs(qT_ref, qs, qz)
                first = [select_blocks]
            else:
                first = [functools.partial(_split_pairs, qT_ref, qs, qz), select_blocks]
            _sweep(qi, k_ref, vT_ref, lambda v, m: v[(m % 2) * head_rows:(m % 2 + 1) * head_rows],
                   lambda m, t, j: (bias_ref[m, t], None if t == 0 else selb_ref[m, pl.ds(j, 1), :]),
                   lambda m, j: (None, selb_ref[m, pl.ds(j, 1), :]),
                   (qz, m_run, acc), pre_ref, with_first_block=first + extra, may_be_first_tile=qs == 0)

        def finish():
            for g in range(PAIRS_PER_STEP):
                o = jnp.concatenate([acc[m, :HEAD_DIM, :] * (1.0 / acc[m, HEAD_DIM:HEAD_DIM + 1, :])
                                     for m in (2 * g, 2 * g + 1)], axis=0)
                o_ref[0, qs * ATT_TILE:(qs + 1) * ATT_TILE, g * PAIR:(g + 1) * PAIR] = o.T.astype(BF16)

        return sweep, finish

    _query_tiles(tile)


def _attention(kind, qT, k, vT, bias, extra, *, lam_init=None):
    B, nk, D, t = qT.shape
    G, S, T = PAIRS_PER_STEP, nk * t, Q_TILES_PER_STEP
    n_maps = 2 * G
    assert D % (G * PAIR) == 0 and nk >= 2 and nk % T == 0 and t == ATT_TILE and QK_AHEAD_PAIR <= QK_AHEAD <= n_maps
    in_specs = [pl.BlockSpec((1, T, G * PAIR, t), lambda b, p, q: (b, q, p, 0)),
                pl.BlockSpec((1, S, G * PAIR), lambda b, p, q: (b, 0, p)),
                pl.BlockSpec((1, G, nk, vT.shape[3], t), lambda b, p, q: (b, p, 0, 0, 0)),
                pl.BlockSpec((n_maps, 2, t, t), lambda b, p, q: (p, 0, 0, 0), pipeline_mode=pl.Buffered(1))]
    scratch = [pltpu.VMEM((T, n_maps, PAIR, t), BF16),
               pltpu.VMEM((T, n_maps, 1, t), F32),
               pltpu.VMEM((max(QK_AHEAD, QK_AHEAD_PAIR), 2, t, t), F32)]
    if kind == "diff":
        body = functools.partial(_diff_kernel, lam_init=lam_init)
        in_specs += [pl.BlockSpec(e.shape, lambda b, p, q: (0, 0)) for e in extra]
        scratch += [pltpu.VMEM((T, n_maps, PAIR + ONES_ROWS, t), F32)]
    else:
        body = functools.partial(_moba_kernel, nk=nk)
        scratch += [pltpu.VMEM((T, n_maps, HEAD_DIM + ONES_ROWS, t), F32),
                    pltpu.VMEM((nk, G * PAIR), F32),
                    pltpu.VMEM((G, 2 * KMEAN_TERMS * nk, PAIR), BF16),
                    pltpu.VMEM((n_maps, nk, t), F32)]
    return pl.pallas_call(
        body,
        grid=(B, D // (G * PAIR), nk // T),
        in_specs=in_specs,
        out_specs=pl.BlockSpec((1, T * t, G * PAIR), lambda b, p, q: (b, q, p)),
        out_shape=jax.ShapeDtypeStruct((B, S, D), BF16),
        scratch_shapes=scratch,
        compiler_params=pltpu.CompilerParams(dimension_semantics=("parallel", "parallel", "arbitrary"),
                                             vmem_limit_bytes=VMEM_LIMIT_BYTES),
        name=kind + "_attention",
    )(qT, k, vT, bias, *extra)


def kernel(x, rel_bias, norm_g, final_norm_g, ffn_w_in, ffn_w_out, diff_w_qkv, diff_lambda,
           diff_subln_g, diff_w_o, moba_w_qkv, moba_w_o):
    B, S, D = x.shape
    depth = norm_g.shape[0]
    assert S % TOKEN_TILE == 0 and D % PAIR == 0 and MOBA_BLOCK == ATT_TILE
    assert rel_bias.shape == (REL_BUCKETS, D // HEAD_DIM)
    bias = _bias_tiles(rel_bias, ATT_TILE)

    F = ffn_w_out.shape[2]
    w_out_rows = ffn_w_out.reshape(depth, 2, D // 2, 2 * F)
    order = [(i, j) for i in range(depth) for j in range(2)]
    ffn_weights = {order[0]: (ffn_w_in[0, 0].astype(BF16), ffn_w_out[0, 0].astype(BF16))}

    mixer_stacks = {"diff_qkv": diff_w_qkv, "diff_o": diff_w_o, "moba_qkv": moba_w_qkv, "moba_o": moba_w_o}
    mixer_weights = {}

    def ffn(h, g, member, **kw):
        nxt = order[order.index(member) + 1] if member != order[-1] else None
        cast = [(ffn_w_in, nxt), (w_out_rows, nxt)] if nxt else []
        riders = [(name, j) for name, stack in mixer_stacks.items() for j in range(stack.shape[0])] \
            if member == order[0] else []
        cast += [(mixer_stacks[name], (j,)) for name, j in riders]
        h, *converted = _ffn(h, g, *ffn_weights.pop(member), (), cast=cast, **kw)
        if nxt:
            ffn_weights[nxt] = (converted[0], converted[1].reshape(F, D))
        mixer_weights.update(zip(riders, converted[2 if nxt else 0:]))
        return h

    h = x.reshape(B * S, D)
    for i in range(depth):
        g = norm_g[i]
        h = ffn(h, g[0], (i, 0))
        j = i // 2
        if i % 2 == 0:
            qT, k, vT = _qkv_proj(h.reshape(B, S, D), g[1], mixer_weights["diff_qkv", j], PAIR)
            lam_init = 0.8 - 0.6 * math.exp(-0.3 * i)
            g_sub = jnp.broadcast_to((diff_subln_g[j].astype(F32) * (1.0 - lam_init))[:, None], (PAIR, ATT_TILE))
            o = _attention("diff", qT, k, vT, bias, (diff_lambda[j].astype(F32), g_sub), lam_init=lam_init)
            w_o = mixer_weights["diff_o", j]
        else:
            qT, k, vT = _qkv_proj(h.reshape(B, S, D), g[1], mixer_weights["moba_qkv", j], HEAD_DIM)
            o = _attention("moba", qT, k, vT, bias, ())
            w_o = mixer_weights["moba_o", j]
        last = i == depth - 1
        h = ffn(h, g[2], (i, 1), proj=(o.reshape(B * S, D), w_o), final_g=final_norm_g if last else None)
    return h.reshape(B, S, D)
```

```python
import functools
import math

import numpy as np
import jax
import jax.numpy as jnp
from jax import lax
from jax.experimental import pallas as pl
from jax.experimental.pallas import tpu as pltpu

HEAD_DIM = 64
PAIR = 2 * HEAD_DIM
MOBA_BLOCK = 256
MOBA_TOPK = 3
KMEAN_TERMS = 3
ONES_ROWS = 16
REL_BUCKETS = 32
REL_MAX_DIST = 128
FFN_RESIDUAL = 0.5
RMS_EPS = 1e-6
SUBLN_EPS = 1e-5

ATT_TILE = 256
PAIRS_PER_STEP = 8
Q_TILES_PER_STEP = 2
QK_AHEAD = 4
QK_AHEAD_PAIR = 2
FFN_CHUNK = 256
TOKEN_TILE = 1024
FFN_TOKEN_TILE = 1024
MASK_VALUE = -1e30
LOG2E = math.log2(math.e)
VMEM_LIMIT_BYTES = 62 * 1024 * 1024

F32 = jnp.float32
BF16 = jnp.bfloat16
_TT = (((0,), (1,)), ((), ()))


def _rms(x, g, eps):
    return x * lax.rsqrt(jnp.mean(x * x, axis=-1, keepdims=True) + eps) * g


def _const_spec(shape):
    return pl.BlockSpec(shape, lambda *_: (0,) * len(shape), pipeline_mode=pl.Buffered(1))


def _rel_bucket_np(dist):
    n = np.maximum(dist, 0)
    max_exact = REL_BUCKETS // 2
    nf = np.maximum(n, 1).astype(np.float32)
    large = max_exact + (np.log(nf / np.float32(max_exact)) / np.float32(math.log(REL_MAX_DIST / max_exact))
                         * np.float32(REL_BUCKETS - max_exact)).astype(np.int32)
    large = np.minimum(large, REL_BUCKETS - 1)
    return np.where(n < max_exact, n, large).astype(np.int32)


def _bucket_rows(t):
    u = np.arange(2 * t)
    diag = np.where(u >= t, _rel_bucket_np(u - t), -1)
    prev = _rel_bucket_np(u)
    assert _rel_bucket_np(np.arange(t + 1, 8 * t)).min() == REL_BUCKETS - 1
    return np.stack([diag, prev]).astype(np.int32)[:, None, :]


def _bias_kernel(rb_ref, idx_ref, out_ref):
    m = pl.program_id(0)
    t = out_ref.shape[2]
    far = rb_ref[REL_BUCKETS - 1, m]
    for tile in range(2):
        idx = idx_ref[tile]
        row = jnp.zeros(idx.shape, F32)
        for b in range(REL_BUCKETS - 1):
            row = jnp.where(idx == b, (rb_ref[b, m] - far) * LOG2E, row)
        row = jnp.where(idx < 0, MASK_VALUE, row)
        rolled = pltpu.roll(jnp.broadcast_to(row, (t, 2 * t)), 0, 1, stride=1, stride_axis=0)
        out_ref[0, tile] = rolled[:, t:]


def _bias_tiles(rel_bias, t):
    n_maps = rel_bias.shape[1]
    idx = jnp.asarray(_bucket_rows(t))
    return pl.pallas_call(
        _bias_kernel,
        grid=(n_maps,),
        in_specs=[pl.BlockSpec(memory_space=pltpu.SMEM),
                  pl.BlockSpec((2, 1, 2 * t), lambda m: (0, 0, 0))],
        out_specs=pl.BlockSpec((1, 2, t, t), lambda m: (m, 0, 0, 0)),
        out_shape=jax.ShapeDtypeStruct((n_maps, 2, t, t), F32),
        name="rel_bias_tiles",
    )(rel_bias.astype(F32), idx)


def _ffn_kernel(*refs, n_chunks, has_proj, has_final, n_cast):
    refs = list(refs)
    x_ref = refs.pop(0)
    if has_proj:
        o_ref, wo_ref = refs.pop(0), refs.pop(0)
    g_ref, win_ref, wout_ref = refs.pop(0), refs.pop(0), refs.pop(0)
    if has_final:
        gf_ref = refs.pop(0)
    cast_in = [refs.pop(0) for _ in range(n_cast)]
    out_ref = refs.pop(0)
    for src, dst in zip(cast_in, refs):
        dst[...] = src[...].astype(BF16)

    x = x_ref[...]
    if has_proj:
        x = x + jnp.dot(o_ref[...], wo_ref[...], preferred_element_type=F32)
    hn = _rms(x, g_ref[...], RMS_EPS).astype(BF16)
    acc = None
    d_ff = n_chunks * FFN_CHUNK
    for c in range(n_chunks):
        cols = slice(c * FFN_CHUNK, (c + 1) * FFN_CHUNK)
        gate = jnp.dot(hn, win_ref[:, cols], preferred_element_type=F32)
        up = jnp.dot(hn, win_ref[:, d_ff + c * FFN_CHUNK:d_ff + (c + 1) * FFN_CHUNK], preferred_element_type=F32)
        a = (gate * (1.0 / (1.0 + jnp.exp(-gate))) * up).astype(BF16)
        part = jnp.dot(a, wout_ref[cols, :], preferred_element_type=F32)
        acc = part if acc is None else acc + part
    y = x + FFN_RESIDUAL * acc
    if has_final:
        y = _rms(y, gf_ref[...], RMS_EPS)
    out_ref[...] = y


def _stacked_spec(stack, index):
    lead = len(index)
    return pl.BlockSpec((None,) * lead + stack.shape[lead:], lambda *_: tuple(index) + (0, 0),
                        pipeline_mode=pl.Buffered(1))


def _row_slabs(stack, index, n_steps):
    lead = len(index)
    R, C = stack.shape[lead:]
    rows = min(r for r in range(16, R + 1, 16) if R % r == 0 and R // r <= n_steps)
    slab = lambda s: jnp.minimum(s, R // rows - 1)
    return (pl.BlockSpec((None,) * lead + (rows, C), lambda s: tuple(index) + (slab(s), 0)),
            pl.BlockSpec((rows, C), lambda s: (slab(s), 0)), jax.ShapeDtypeStruct((R, C), BF16))


def _ffn(x2d, g, w_in_all, w_out_all, which, proj=None, final_g=None, cast=()):
    T, D = x2d.shape
    F = w_out_all.shape[-2]
    n_chunks = F // FFN_CHUNK
    assert n_chunks * FFN_CHUNK == F and w_in_all.shape[-2:] == (D, 2 * F)
    tm = math.gcd(T, FFN_TOKEN_TILE)

    row = lambda i: (i, 0)
    args, specs = [x2d], [pl.BlockSpec((tm, D), row)]
    if proj is not None:
        o2d, w_o = proj
        args += [o2d, w_o]
        specs += [pl.BlockSpec((tm, o2d.shape[1]), row), _const_spec(w_o.shape)]
    args += [g.reshape(1, D).astype(F32), w_in_all, w_out_all]
    specs += [_const_spec((1, D)), _stacked_spec(w_in_all, which), _stacked_spec(w_out_all, which)]
    if final_g is not None:
        args.append(final_g.reshape(1, D).astype(F32))
        specs.append(_const_spec((1, D)))

    slabs = [_row_slabs(stack, index, T // tm) for stack, index in cast]
    args += [stack for stack, _ in cast]
    specs += [in_spec for in_spec, _, _ in slabs]

    return pl.pallas_call(
        functools.partial(_ffn_kernel, n_chunks=n_chunks, has_proj=proj is not None,
                          has_final=final_g is not None, n_cast=len(cast)),
        grid=(T // tm,),
        in_specs=specs,
        out_specs=[pl.BlockSpec((tm, D), row)] + [out_spec for _, out_spec, _ in slabs],
        out_shape=[jax.ShapeDtypeStruct((T, D), F32)] + [shape for _, _, shape in slabs],
        compiler_params=pltpu.CompilerParams(dimension_semantics=("parallel",),
                                             vmem_limit_bytes=VMEM_LIMIT_BYTES),
        name="ffn",
    )(*args)


def _proj_kernel(x_ref, g_ref, w_ref, qT_ref, k_ref, vT_ref, *, n_pairs, n_sub, v_dim):
    D = x_ref.shape[2]
    hn = _rms(x_ref[0], g_ref[...], RMS_EPS).astype(BF16)
    k_ref[0] = jnp.dot(hn, w_ref[:, D:2 * D], preferred_element_type=F32).astype(BF16)
    qT = lax.dot_general(w_ref[:, :D], hn, _TT, preferred_element_type=F32) * (HEAD_DIM ** -0.5 * LOG2E)
    qT = qT.astype(BF16)
    for c in range(n_sub):
        qT_ref[0, c] = qT[:, c * ATT_TILE:(c + 1) * ATT_TILE]
    vT = lax.dot_general(w_ref[:, 2 * D:], hn, _TT, preferred_element_type=F32).astype(BF16)
    ones_rows = jnp.where(lax.broadcasted_iota(jnp.int32, (ONES_ROWS, ATT_TILE), 0) == 0, 1.0, 0.0).astype(BF16)
    group = v_dim + ONES_ROWS
    for p in range(n_pairs):
        for c in range(n_sub):
            cols = slice(c * ATT_TILE, (c + 1) * ATT_TILE)
            for i in range(PAIR // v_dim):
                vT_ref[0, p, c, i * group:i * group + v_dim, :] = vT[p * PAIR + i * v_dim:p * PAIR + (i + 1) * v_dim, cols]
                vT_ref[0, p, c, i * group + v_dim:(i + 1) * group, :] = ones_rows


def _v_rows(v_dim):
    return (PAIR // v_dim) * (v_dim + ONES_ROWS)


def _qkv_proj(h, g, w_qkv, v_dim):
    B, S, D = h.shape
    tm = TOKEN_TILE
    n_pairs, n_sub, nk, rows = D // PAIR, tm // ATT_TILE, S // ATT_TILE, _v_rows(v_dim)
    return pl.pallas_call(
        functools.partial(_proj_kernel, n_pairs=n_pairs, n_sub=n_sub, v_dim=v_dim),
        grid=(B, S // tm),
        in_specs=[pl.BlockSpec((1, tm, D), lambda b, s: (b, s, 0)),
                  _const_spec((1, D)), _const_spec((D, 3 * D))],
        out_specs=[pl.BlockSpec((1, n_sub, D, ATT_TILE), lambda b, s: (b, s, 0, 0)),
                   pl.BlockSpec((1, tm, D), lambda b, s: (b, s, 0)),
                   pl.BlockSpec((1, n_pairs, n_sub, rows, ATT_TILE), lambda b, s: (b, 0, s, 0, 0))],
        out_shape=[jax.ShapeDtypeStruct((B, nk, D, ATT_TILE), BF16),
                   jax.ShapeDtypeStruct((B, S, D), BF16),
                   jax.ShapeDtypeStruct((B, n_pairs, nk, rows, ATT_TILE), BF16)],
        compiler_params=pltpu.CompilerParams(dimension_semantics=("parallel", "parallel"),
                                             vmem_limit_bytes=VMEM_LIMIT_BYTES),
        name="qkv_proj",
    )(h, g.reshape(1, D).astype(F32), w_qkv)


def _split_pairs(qT_ref, qs, qz_ref):
    zeros = jnp.zeros((HEAD_DIM, qT_ref.shape[3]), qT_ref.dtype)
    for g in range(PAIRS_PER_STEP):
        q = qT_ref[0, qs, g * PAIR:(g + 1) * PAIR, :]
        qz_ref[2 * g, :HEAD_DIM, :] = q[:HEAD_DIM]
        qz_ref[2 * g, HEAD_DIM:, :] = zeros
        qz_ref[2 * g + 1, :HEAD_DIM, :] = zeros
        qz_ref[2 * g + 1, HEAD_DIM:, :] = q[HEAD_DIM:]


def _update(m, scores, vT_tiles, biases, m_ref, acc_ref, first):
    scores = [s if b is None else b + s for s, (b, _) in zip(scores, biases)]
    masks = [r for _, r in biases]
    if all(r is None for r in masks):
        m_cur = jnp.max(functools.reduce(jnp.maximum, scores), axis=0, keepdims=True)
    else:
        m_cur = functools.reduce(jnp.maximum, [jnp.max(s, axis=0, keepdims=True) + (0.0 if r is None else r)
                                               for s, r in zip(scores, masks)])
    if first:
        m_new = m_cur
    else:
        m_old = m_ref[m]
        m_new = jnp.maximum(m_old, m_cur)
    shifts = [m_new if r is None else m_new - r for r in masks]
    pv = functools.reduce(jnp.add, [jnp.dot(v, jnp.exp2(s - sh).astype(BF16), preferred_element_type=F32)
                                    for v, s, sh in zip(vT_tiles, scores, shifts)])
    if first:
        acc_ref[m] = pv
    else:
        acc_ref[m] = jnp.exp2(m_old - m_new) * acc_ref[m] + pv
    m_ref[m] = m_new


def _sweep(qi, k_ref, vT_ref, v_rows, near_bias, far_bias, state, pre_ref, with_first_block=(), may_be_first_tile=True):
    qz_ref, m_ref, acc_ref = state
    n_maps = 2 * PAIRS_PER_STEP

    def raw_scores(j, m):
        rows = pl.ds(pl.multiple_of(j * ATT_TILE, ATT_TILE), ATT_TILE)
        k_t = k_ref[0, rows, (m // 2) * PAIR:(m // 2 + 1) * PAIR]
        return jnp.dot(k_t, qz_ref[m], preferred_element_type=F32)

    def block(tiles, biases, first, ahead, preloaded, next_tiles, next_ahead):
        scores = {}
        for m in range(ahead):
            scores[m] = ([pre_ref[m, t] for t in range(len(tiles))] if preloaded
                         else [raw_scores(j, m) for j in tiles])
        for m in range(n_maps):
            nxt = m + ahead
            if nxt < n_maps:
                scores[nxt] = [raw_scores(j, nxt) for j in tiles]
            elif nxt - n_maps < next_ahead:
                for t, j in enumerate(next_tiles):
                    pre_ref[nxt - n_maps, t] = raw_scores(j, nxt - n_maps)
            _update(m, scores.pop(m), [v_rows(vT_ref[0, m // 2, j], m) for j in tiles], biases(m),
                    m_ref, acc_ref, first)

    n_far = jnp.maximum(qi - 1, 0)
    n_pairs = n_far // 2

    def first_pair():
        for traced_alongside in with_first_block:
            traced_alongside()
        block([qi - 1, qi], lambda m: [near_bias(m, 1, qi - 1), near_bias(m, 0, qi)], True, QK_AHEAD_PAIR, False,
              [0, 1], QK_AHEAD_PAIR)

    if may_be_first_tile:
        @pl.when(qi == 0)
        def _():
            block([qi], lambda m: [near_bias(m, 0, qi)], True, QK_AHEAD, False, [], 0)

        pl.when(qi >= 1)(first_pair)
    else:
        first_pair()

    @pl.when(n_far % 2 == 1)
    def _():
        block([n_far - 1], lambda m: [far_bias(m, n_far - 1)], False, QK_AHEAD, False, [], 0)

    def far_pair(p, carry):
        j, jn = 2 * p, 2 * jnp.minimum(p + 1, n_pairs - 1)
        block([j, j + 1], lambda m: [far_bias(m, j), far_bias(m, j + 1)], False, QK_AHEAD_PAIR, True,
              [jn, jn + 1], QK_AHEAD_PAIR)
        return carry

    lax.fori_loop(0, n_pairs, far_pair, 0)


def _query_tiles(step_body):
    first = pl.program_id(2) * Q_TILES_PER_STEP
    pending = []
    for qs in range(Q_TILES_PER_STEP):
        sweep, finish = step_body(qs, first + qs)
        sweep(pending)
        pending = [finish]
    pending[0]()


def _diff_kernel(qT_ref, k_ref, vT_ref, bias_ref, lam_ref, g_ref, o_ref,
                 qz_ref, m_ref, pre_ref, acc_ref, *, lam_init):
    def tile(qs, qi):
        qz, m_run, acc = qz_ref.at[qs], m_ref.at[qs], acc_ref.at[qs]

        def sweep(extra):
            if qs == 0:
                _split_pairs(qT_ref, qs, qz)
                first = []
            else:
                first = [functools.partial(_split_pairs, qT_ref, qs, qz)]
            _sweep(qi, k_ref, vT_ref, lambda v, m: v,
                   lambda m, t, j: (bias_ref[m, t], None), lambda m, j: (None, None),
                   (qz, m_run, acc), pre_ref, with_first_block=first + extra, may_be_first_tile=qs == 0)

        def finish():
            lp = lam_ref[...]
            lam = (jnp.exp(jnp.sum(lp[0:1] * lp[1:2], axis=-1, keepdims=True))
                   - jnp.exp(jnp.sum(lp[2:3] * lp[3:4], axis=-1, keepdims=True)) + lam_init)

            def weighted(m, scale):
                return acc[m, :PAIR, :] * (scale / acc[m, PAIR:PAIR + 1, :])

            for g in range(PAIRS_PER_STEP):
                o = weighted(2 * g, 1.0) - weighted(2 * g + 1, lam)
                o = o * lax.rsqrt(jnp.mean(o * o, axis=0, keepdims=True) + SUBLN_EPS) * g_ref[...]
                o_ref[0, qs * ATT_TILE:(qs + 1) * ATT_TILE, g * PAIR:(g + 1) * PAIR] = o.T.astype(BF16)

        return sweep, finish

    _query_tiles(tile)


def _moba_kernel(qT_ref, k_ref, vT_ref, bias_ref, o_ref,
                 qz_ref, m_ref, pre_ref, acc_ref, kmean_ref, kgate_ref, selb_ref, *, nk):
    tq = qT_ref.shape[3]
    head_rows = HEAD_DIM + ONES_ROWS

    def tile(qs, qi):
        qz, m_run, acc = qz_ref.at[qs], m_ref.at[qs], acc_ref.at[qs]

        def select_blocks():
            blk = lax.broadcasted_iota(jnp.int32, (nk, tq), 0).astype(F32)
            eligible = blk < qi.astype(F32)
            for m in range(2 * PAIRS_PER_STEP):
                if m % 2 == 0:
                    q_pair = qT_ref[0, qs, (m // 2) * PAIR:(m // 2 + 1) * PAIR, :]
                    gates = jnp.dot(kgate_ref[m // 2], q_pair, preferred_element_type=F32)
                base = (m % 2) * KMEAN_TERMS * nk
                gate = functools.reduce(jnp.add, [gates[base + t * nk:base + (t + 1) * nk]
                                                  for t in range(KMEAN_TERMS)])
                gate = jnp.where(eligible, gate, -jnp.inf)
                picked = jnp.zeros((nk, tq), F32)
                for _ in range(MOBA_TOPK):
                    best = jnp.max(gate, axis=0, keepdims=True)
                    pick = blk == jnp.min(jnp.where(gate == best, blk, float(nk)), axis=0, keepdims=True)
                    picked = jnp.where(pick, 1.0, picked)
                    gate = jnp.where(pick, -jnp.inf, gate)
                selb_ref[m] = jnp.where(eligible, jnp.where(picked > 0.0, 0.0, MASK_VALUE), MASK_VALUE)

        def sweep(extra):
            if qs == 0:
                @pl.when(qi == 0)
                def _():
                    for j in range(nk):
                        kb = k_ref[0, j * MOBA_BLOCK:(j + 1) * MOBA_BLOCK, :].astype(F32)
                        kmean_ref[j:j + 1, :] = jnp.mean(kb, axis=0, keepdims=True)
                    lane = lax.broadcasted_iota(jnp.int32, (nk, PAIR), 1)
                    for g in range(PAIRS_PER_STEP):
                        mean = kmean_ref[:, g * PAIR:(g + 1) * PAIR]
                        for i in range(2):
                            rest = jnp.where((lane >= i * HEAD_DIM) & (lane < (i + 1) * HEAD_DIM), mean, 0.0)
                            for t in range(KMEAN_TERMS):
                                term = rest.astype(BF16)
                                row = (i * KMEAN_TERMS + t) * nk
                                kgate_ref[g, row:row + nk, :] = term
                                rest = rest - term.astype(F32)

                _split_pairs(qT_ref, qs, qz)
                first = [select_blocks]
            else:
                first = [functools.partial(_split_pairs, qT_ref, qs, qz), select_blocks]
            _sweep(qi, k_ref, vT_ref, lambda v, m: v[(m % 2) * head_rows:(m % 2 + 1) * head_rows],
                   lambda m, t, j: (bias_ref[m, t], None if t == 0 else selb_ref[m, pl.ds(j, 1), :]),
                   lambda m, j: (None, selb_ref[m, pl.ds(j, 1), :]),
                   (qz, m_run, acc), pre_ref, with_first_block=first + extra, may_be_first_tile=qs == 0)

        def finish():
            for g in range(PAIRS_PER_STEP):
                o = jnp.concatenate([acc[m, :HEAD_DIM, :] * (1.0 / acc[m, HEAD_DIM:HEAD_DIM + 1, :])
                                     for m in (2 * g, 2 * g + 1)], axis=0)
                o_ref[0, qs * ATT_TILE:(qs + 1) * ATT_TILE, g * PAIR:(g + 1) * PAIR] = o.T.astype(BF16)

        return sweep, finish

    _query_tiles(tile)


def _attention(kind, qT, k, vT, bias, extra, *, lam_init=None):
    B, nk, D, t = qT.shape
    G, S, T = PAIRS_PER_STEP, nk * t, Q_TILES_PER_STEP
    n_maps = 2 * G
    assert D % (G * PAIR) == 0 and nk >= 2 and nk % T == 0 and t == ATT_TILE and QK_AHEAD_PAIR <= QK_AHEAD <= n_maps
    in_specs = [pl.BlockSpec((1, T, G * PAIR, t), lambda b, p, q: (b, q, p, 0)),
                pl.BlockSpec((1, S, G * PAIR), lambda b, p, q: (b, 0, p)),
                pl.BlockSpec((1, G, nk, vT.shape[3], t), lambda b, p, q: (b, p, 0, 0, 0)),
                pl.BlockSpec((n_maps, 2, t, t), lambda b, p, q: (p, 0, 0, 0), pipeline_mode=pl.Buffered(1))]
    scratch = [pltpu.VMEM((T, n_maps, PAIR, t), BF16),
               pltpu.VMEM((T, n_maps, 1, t), F32),
               pltpu.VMEM((max(QK_AHEAD, QK_AHEAD_PAIR), 2, t, t), F32)]
    if kind == "diff":
        body = functools.partial(_diff_kernel, lam_init=lam_init)
        in_specs += [pl.BlockSpec(e.shape, lambda b, p, q: (0, 0)) for e in extra]
        scratch += [pltpu.VMEM((T, n_maps, PAIR + ONES_ROWS, t), F32)]
    else:
        body = functools.partial(_moba_kernel, nk=nk)
        scratch += [pltpu.VMEM((T, n_maps, HEAD_DIM + ONES_ROWS, t), F32),
                    pltpu.VMEM((nk, G * PAIR), F32),
                    pltpu.VMEM((G, 2 * KMEAN_TERMS * nk, PAIR), BF16),
                    pltpu.VMEM((n_maps, nk, t), F32)]
    return pl.pallas_call(
        body,
        grid=(B, D // (G * PAIR), nk // T),
        in_specs=in_specs,
        out_specs=pl.BlockSpec((1, T * t, G * PAIR), lambda b, p, q: (b, q, p)),
        out_shape=jax.ShapeDtypeStruct((B, S, D), BF16),
        scratch_shapes=scratch,
        compiler_params=pltpu.CompilerParams(dimension_semantics=("parallel", "parallel", "arbitrary"),
                                             vmem_limit_bytes=VMEM_LIMIT_BYTES),
        name=kind + "_attention",
    )(qT, k, vT, bias, *extra)


def kernel(x, rel_bias, norm_g, final_norm_g, ffn_w_in, ffn_w_out, diff_w_qkv, diff_lambda,
           diff_subln_g, diff_w_o, moba_w_qkv, moba_w_o):
    B, S, D = x.shape
    depth = norm_g.shape[0]
    assert S % TOKEN_TILE == 0 and D % PAIR == 0 and MOBA_BLOCK == ATT_TILE
    assert rel_bias.shape == (REL_BUCKETS, D // HEAD_DIM)
    bias = _bias_tiles(rel_bias, ATT_TILE)

    order = [(i, j) for i in range(depth) for j in range(2)]
    ffn_weights = {order[0]: (ffn_w_in[0, 0].astype(BF16), ffn_w_out[0, 0].astype(BF16))}

    mixer_stacks = {"diff_qkv": diff_w_qkv, "diff_o": diff_w_o, "moba_qkv": moba_w_qkv, "moba_o": moba_w_o}
    mixer_weights = {}

    def ffn(h, g, member, **kw):
        nxt = order[order.index(member) + 1] if member != order[-1] else None
        cast = [(ffn_w_in, nxt), (ffn_w_out, nxt)] if nxt else []
        riders = [(name, j) for name, stack in mixer_stacks.items() for j in range(stack.shape[0])] \
            if member == order[0] else []
        cast += [(mixer_stacks[name], (j,)) for name, j in riders]
        h, *converted = _ffn(h, g, *ffn_weights.pop(member), (), cast=cast, **kw)
        if nxt:
            ffn_weights[nxt] = (converted[0], converted[1])
        mixer_weights.update(zip(riders, converted[2 if nxt else 0:]))
        return h

    h = x.reshape(B * S, D)
    for i in range(depth):
        g = norm_g[i]
        h = ffn(h, g[0], (i, 0))
        j = i // 2
        if i % 2 == 0:
            qT, k, vT = _qkv_proj(h.reshape(B, S, D), g[1], mixer_weights["diff_qkv", j], PAIR)
            lam_init = 0.8 - 0.6 * math.exp(-0.3 * i)
            g_sub = jnp.broadcast_to((diff_subln_g[j].astype(F32) * (1.0 - lam_init))[:, None], (PAIR, ATT_TILE))
            o = _attention("diff", qT, k, vT, bias, (diff_lambda[j].astype(F32), g_sub), lam_init=lam_init)
            w_o = mixer_weights["diff_o", j]
        else:
            qT, k, vT = _qkv_proj(h.reshape(B, S, D), g[1], mixer_weights["moba_qkv", j], HEAD_DIM)
            o = _attention("moba", qT, k, vT, bias, ())
            w_o = mixer_weights["moba_o", j]
        last = i == depth - 1
        h = ffn(h, g[2], (i, 1), proj=(o.reshape(B * S, D), w_o), final_g=final_norm_g if last else None)
    return h.reshape(B, S, D)
```

```python
import functools
import math

import numpy as np
import jax
import jax.numpy as jnp
from jax import lax
from jax.experimental import pallas as pl
from jax.experimental.pallas import tpu as pltpu

HEAD_DIM = 64
PAIR = 2 * HEAD_DIM
MOBA_BLOCK = 256
MOBA_TOPK = 3
KMEAN_TERMS = 3
ONES_ROWS = 16
REL_BUCKETS = 32
REL_MAX_DIST = 128
FFN_RESIDUAL = 0.5
RMS_EPS = 1e-6
SUBLN_EPS = 1e-5

ATT_TILE = 256
PAIRS_PER_STEP = 8
Q_TILES_PER_STEP = 2
QK_AHEAD = 4
QK_AHEAD_PAIR = 2
FFN_CHUNK = 256
TOKEN_TILE = 1024
FFN_TOKEN_TILE = 1024
MASK_VALUE = -1e30
LOG2E = math.log2(math.e)
VMEM_LIMIT_BYTES = 62 * 1024 * 1024

F32 = jnp.float32
BF16 = jnp.bfloat16
_TT = (((0,), (1,)), ((), ()))


def _rms(x, g, eps):
    return x * lax.rsqrt(jnp.mean(x * x, axis=-1, keepdims=True) + eps) * g


def _const_spec(shape):
    return pl.BlockSpec(shape, lambda *_: (0,) * len(shape), pipeline_mode=pl.Buffered(1))


def _rel_bucket_np(dist):
    n = np.maximum(dist, 0)
    max_exact = REL_BUCKETS // 2
    nf = np.maximum(n, 1).astype(np.float32)
    large = max_exact + (np.log(nf / np.float32(max_exact)) / np.float32(math.log(REL_MAX_DIST / max_exact))
                         * np.float32(REL_BUCKETS - max_exact)).astype(np.int32)
    large = np.minimum(large, REL_BUCKETS - 1)
    return np.where(n < max_exact, n, large).astype(np.int32)


def _bucket_rows(t):
    u = np.arange(2 * t)
    diag = np.where(u >= t, _rel_bucket_np(u - t), -1)
    prev = _rel_bucket_np(u)
    assert _rel_bucket_np(np.arange(t + 1, 8 * t)).min() == REL_BUCKETS - 1
    return np.stack([diag, prev]).astype(np.int32)[:, None, :]


def _bias_kernel(rb_ref, idx_ref, out_ref):
    m = pl.program_id(0)
    t = out_ref.shape[2]
    far = rb_ref[REL_BUCKETS - 1, m]
    for tile in range(2):
        idx = idx_ref[tile]
        row = jnp.zeros(idx.shape, F32)
        for b in range(REL_BUCKETS - 1):
            row = jnp.where(idx == b, (rb_ref[b, m] - far) * LOG2E, row)
        row = jnp.where(idx < 0, MASK_VALUE, row)
        rolled = pltpu.roll(jnp.broadcast_to(row, (t, 2 * t)), 0, 1, stride=1, stride_axis=0)
        out_ref[0, tile] = rolled[:, t:]


def _bias_tiles(rel_bias, t):
    n_maps = rel_bias.shape[1]
    idx = jnp.asarray(_bucket_rows(t))
    return pl.pallas_call(
        _bias_kernel,
        grid=(n_maps,),
        in_specs=[pl.BlockSpec(memory_space=pltpu.SMEM),
                  pl.BlockSpec((2, 1, 2 * t), lambda m: (0, 0, 0))],
        out_specs=pl.BlockSpec((1, 2, t, t), lambda m: (m, 0, 0, 0)),
        out_shape=jax.ShapeDtypeStruct((n_maps, 2, t, t), F32),
        name="rel_bias_tiles",
    )(rel_bias.astype(F32), idx)


def _ffn_kernel(*refs, n_chunks, has_proj, has_final, n_cast):
    refs = list(refs)
    x_ref = refs.pop(0)
    if has_proj:
        o_ref, wo_ref = refs.pop(0), refs.pop(0)
    g_ref, win_ref, wout_ref = refs.pop(0), refs.pop(0), refs.pop(0)
    if has_final:
        gf_ref = refs.pop(0)
    cast_in = [refs.pop(0) for _ in range(n_cast)]
    out_ref = refs.pop(0)
    for src, dst in zip(cast_in, refs):
        dst[...] = src[...].astype(BF16)

    x = x_ref[...]
    if has_proj:
        x = x + jnp.dot(o_ref[...], wo_ref[...], preferred_element_type=F32)
    hn = _rms(x, g_ref[...], RMS_EPS).astype(BF16)
    acc = None
    d_ff = n_chunks * FFN_CHUNK
    for c in range(n_chunks):
        cols = slice(c * FFN_CHUNK, (c + 1) * FFN_CHUNK)
        gate = jnp.dot(hn, win_ref[:, cols], preferred_element_type=F32)
        up = jnp.dot(hn, win_ref[:, d_ff + c * FFN_CHUNK:d_ff + (c + 1) * FFN_CHUNK], preferred_element_type=F32)
        a = (gate * (1.0 / (1.0 + jnp.exp(-gate))) * up).astype(BF16)
        part = jnp.dot(a, wout_ref[cols, :], preferred_element_type=F32)
        acc = part if acc is None else acc + part
    y = x + FFN_RESIDUAL * acc
    if has_final:
        y = _rms(y, gf_ref[...], RMS_EPS)
    out_ref[...] = y


def _stacked_spec(stack, index):
    lead = len(index)
    return pl.BlockSpec((None,) * lead + stack.shape[lead:], lambda *_: tuple(index) + (0, 0),
                        pipeline_mode=pl.Buffered(1))


def _row_slabs(stack, index, n_steps):
    lead = len(index)
    R, C = stack.shape[lead:]
    rows = min(r for r in range(16, R + 1, 16) if R % r == 0 and R // r <= n_steps)
    slab = lambda s: jnp.minimum(s, R // rows - 1)
    return (pl.BlockSpec((None,) * lead + (rows, C), lambda s: tuple(index) + (slab(s), 0)),
            pl.BlockSpec((rows, C), lambda s: (slab(s), 0)), jax.ShapeDtypeStruct((R, C), BF16))


def _ffn(x2d, g, w_in_all, w_out_all, which, proj=None, final_g=None, cast=()):
    T, D = x2d.shape
    F = w_out_all.shape[-2]
    n_chunks = F // FFN_CHUNK
    assert n_chunks * FFN_CHUNK == F and w_in_all.shape[-2:] == (D, 2 * F)
    tm = math.gcd(T, FFN_TOKEN_TILE)

    row = lambda i: (i, 0)
    args, specs = [x2d], [pl.BlockSpec((tm, D), row)]
    if proj is not None:
        o2d, w_o = proj
        args += [o2d, w_o]
        specs += [pl.BlockSpec((tm, o2d.shape[1]), row), _const_spec(w_o.shape)]
    args += [g.reshape(1, D).astype(F32), w_in_all, w_out_all]
    specs += [_const_spec((1, D)), _stacked_spec(w_in_all, which), _stacked_spec(w_out_all, which)]
    if final_g is not None:
        args.append(final_g.reshape(1, D).astype(F32))
        specs.append(_const_spec((1, D)))

    slabs = [_row_slabs(stack, index, T // tm) for stack, index in cast]
    args += [stack for stack, _ in cast]
    specs += [in_spec for in_spec, _, _ in slabs]

    return pl.pallas_call(
        functools.partial(_ffn_kernel, n_chunks=n_chunks, has_proj=proj is not None,
                          has_final=final_g is not None, n_cast=len(cast)),
        grid=(T // tm,),
        in_specs=specs,
        out_specs=[pl.BlockSpec((tm, D), row)] + [out_spec for _, out_spec, _ in slabs],
        out_shape=[jax.ShapeDtypeStruct((T, D), F32)] + [shape for _, _, shape in slabs],
        compiler_params=pltpu.CompilerParams(dimension_semantics=("arbitrary",),
                                             vmem_limit_bytes=VMEM_LIMIT_BYTES),
        name="ffn",
    )(*args)


def _proj_kernel(x_ref, g_ref, w_ref, qT_ref, k_ref, vT_ref, *, n_pairs, n_sub, v_dim):
    D = x_ref.shape[2]
    hn = _rms(x_ref[0], g_ref[...], RMS_EPS).astype(BF16)
    k_ref[0] = jnp.dot(hn, w_ref[:, D:2 * D], preferred_element_type=F32).astype(BF16)
    qT = lax.dot_general(w_ref[:, :D], hn, _TT, preferred_element_type=F32) * (HEAD_DIM ** -0.5 * LOG2E)
    qT = qT.astype(BF16)
    for c in range(n_sub):
        qT_ref[0, c] = qT[:, c * ATT_TILE:(c + 1) * ATT_TILE]
    vT = lax.dot_general(w_ref[:, 2 * D:], hn, _TT, preferred_element_type=F32).astype(BF16)
    ones_rows = jnp.where(lax.broadcasted_iota(jnp.int32, (ONES_ROWS, ATT_TILE), 0) == 0, 1.0, 0.0).astype(BF16)
    group = v_dim + ONES_ROWS
    for p in range(n_pairs):
        for c in range(n_sub):
            cols = slice(c * ATT_TILE, (c + 1) * ATT_TILE)
            for i in range(PAIR // v_dim):
                vT_ref[0, p, c, i * group:i * group + v_dim, :] = vT[p * PAIR + i * v_dim:p * PAIR + (i + 1) * v_dim, cols]
                vT_ref[0, p, c, i * group + v_dim:(i + 1) * group, :] = ones_rows


def _v_rows(v_dim):
    return (PAIR // v_dim) * (v_dim + ONES_ROWS)


def _qkv_proj(h, g, w_qkv, v_dim):
    B, S, D = h.shape
    tm = TOKEN_TILE
    n_pairs, n_sub, nk, rows = D // PAIR, tm // ATT_TILE, S // ATT_TILE, _v_rows(v_dim)
    return pl.pallas_call(
        functools.partial(_proj_kernel, n_pairs=n_pairs, n_sub=n_sub, v_dim=v_dim),
        grid=(B, S // tm),
        in_specs=[pl.BlockSpec((1, tm, D), lambda b, s: (b, s, 0)),
                  _const_spec((1, D)), _const_spec((D, 3 * D))],
        out_specs=[pl.BlockSpec((1, n_sub, D, ATT_TILE), lambda b, s: (b, s, 0, 0)),
                   pl.BlockSpec((1, tm, D), lambda b, s: (b, s, 0)),
                   pl.BlockSpec((1, n_pairs, n_sub, rows, ATT_TILE), lambda b, s: (b, 0, s, 0, 0))],
        out_shape=[jax.ShapeDtypeStruct((B, nk, D, ATT_TILE), BF16),
                   jax.ShapeDtypeStruct((B, S, D), BF16),
                   jax.ShapeDtypeStruct((B, n_pairs, nk, rows, ATT_TILE), BF16)],
        compiler_params=pltpu.CompilerParams(dimension_semantics=("parallel", "parallel"),
                                             vmem_limit_bytes=VMEM_LIMIT_BYTES),
        name="qkv_proj",
    )(h, g.reshape(1, D).astype(F32), w_qkv)


def _split_pairs(qT_ref, qs, qz_ref):
    zeros = jnp.zeros((HEAD_DIM, qT_ref.shape[3]), qT_ref.dtype)
    for g in range(PAIRS_PER_STEP):
        q = qT_ref[0, qs, g * PAIR:(g + 1) * PAIR, :]
        qz_ref[2 * g, :HEAD_DIM, :] = q[:HEAD_DIM]
        qz_ref[2 * g, HEAD_DIM:, :] = zeros
        qz_ref[2 * g + 1, :HEAD_DIM, :] = zeros
        qz_ref[2 * g + 1, HEAD_DIM:, :] = q[HEAD_DIM:]


def _update(m, scores, vT_tiles, biases, m_ref, acc_ref, first):
    scores = [s if b is None else b + s for s, (b, _) in zip(scores, biases)]
    masks = [r for _, r in biases]
    if all(r is None for r in masks):
        m_cur = jnp.max(functools.reduce(jnp.maximum, scores), axis=0, keepdims=True)
    else:
        m_cur = functools.reduce(jnp.maximum, [jnp.max(s, axis=0, keepdims=True) + (0.0 if r is None else r)
                                               for s, r in zip(scores, masks)])
    if first:
        m_new = m_cur
    else:
        m_old = m_ref[m]
        m_new = jnp.maximum(m_old, m_cur)
    shifts = [m_new if r is None else m_new - r for r in masks]
    pv = functools.reduce(jnp.add, [jnp.dot(v, jnp.exp2(s - sh).astype(BF16), preferred_element_type=F32)
                                    for v, s, sh in zip(vT_tiles, scores, shifts)])
    if first:
        acc_ref[m] = pv
    else:
        acc_ref[m] = jnp.exp2(m_old - m_new) * acc_ref[m] + pv
    m_ref[m] = m_new


def _sweep(qi, k_ref, vT_ref, v_rows, near_bias, far_bias, state, pre_ref, with_first_block=(), may_be_first_tile=True):
    qz_ref, m_ref, acc_ref = state
    n_maps = 2 * PAIRS_PER_STEP

    def raw_scores(j, m):
        rows = pl.ds(pl.multiple_of(j * ATT_TILE, ATT_TILE), ATT_TILE)
        k_t = k_ref[0, rows, (m // 2) * PAIR:(m // 2 + 1) * PAIR]
        return jnp.dot(k_t, qz_ref[m], preferred_element_type=F32)

    def block(tiles, biases, first, ahead, preloaded, next_tiles, next_ahead):
        scores = {}
        for m in range(ahead):
            scores[m] = ([pre_ref[m, t] for t in range(len(tiles))] if preloaded
                         else [raw_scores(j, m) for j in tiles])
        for m in range(n_maps):
            nxt = m + ahead
            if nxt < n_maps:
                scores[nxt] = [raw_scores(j, nxt) for j in tiles]
            elif nxt - n_maps < next_ahead:
                for t, j in enumerate(next_tiles):
                    pre_ref[nxt - n_maps, t] = raw_scores(j, nxt - n_maps)
            _update(m, scores.pop(m), [v_rows(vT_ref[0, m // 2, j], m) for j in tiles], biases(m),
                    m_ref, acc_ref, first)

    n_far = jnp.maximum(qi - 1, 0)
    n_pairs = n_far // 2

    def first_pair():
        for traced_alongside in with_first_block:
            traced_alongside()
        block([qi - 1, qi], lambda m: [near_bias(m, 1, qi - 1), near_bias(m, 0, qi)], True, QK_AHEAD_PAIR, False,
              [0, 1], QK_AHEAD_PAIR)

    if may_be_first_tile:
        @pl.when(qi == 0)
        def _():
            block([qi], lambda m: [near_bias(m, 0, qi)], True, QK_AHEAD, False, [], 0)

        pl.when(qi >= 1)(first_pair)
    else:
        first_pair()

    @pl.when(n_far % 2 == 1)
    def _():
        block([n_far - 1], lambda m: [far_bias(m, n_far - 1)], False, QK_AHEAD, False, [], 0)

    def far_pair(p, carry):
        j, jn = 2 * p, 2 * jnp.minimum(p + 1, n_pairs - 1)
        block([j, j + 1], lambda m: [far_bias(m, j), far_bias(m, j + 1)], False, QK_AHEAD_PAIR, True,
              [jn, jn + 1], QK_AHEAD_PAIR)
        return carry

    lax.fori_loop(0, n_pairs, far_pair, 0)


def _query_tiles(step_body):
    first = pl.program_id(2) * Q_TILES_PER_STEP
    pending = []
    for qs in range(Q_TILES_PER_STEP):
        sweep, finish = step_body(qs, first + qs)
        sweep(pending)
        pending = [finish]
    pending[0]()


def _diff_kernel(qT_ref, k_ref, vT_ref, bias_ref, lam_ref, g_ref, o_ref,
                 qz_ref, m_ref, pre_ref, acc_ref, *, lam_init):
    def tile(qs, qi):
        qz, m_run, acc = qz_ref.at[qs], m_ref.at[qs], acc_ref.at[qs]

        def sweep(extra):
            if qs == 0:
                _split_pairs(qT_ref, qs, qz)
                first = []
            else:
                first = [functools.partial(_split_pairs, qT_ref, qs, qz)]
            _sweep(qi, k_ref, vT_ref, lambda v, m: v,
                   lambda m, t, j: (bias_ref[m, t], None), lambda m, j: (None, None),
                   (qz, m_run, acc), pre_ref, with_first_block=first + extra, may_be_first_tile=qs == 0)

        def finish():
            lp = lam_ref[...]
            lam = (jnp.exp(jnp.sum(lp[0:1] * lp[1:2], axis=-1, keepdims=True))
                   - jnp.exp(jnp.sum(lp[2:3] * lp[3:4], axis=-1, keepdims=True)) + lam_init)

            def weighted(m, scale):
                return acc[m, :PAIR, :] * (scale / acc[m, PAIR:PAIR + 1, :])

            for g in range(PAIRS_PER_STEP):
                o = weighted(2 * g, 1.0) - weighted(2 * g + 1, lam)
                o = o * lax.rsqrt(jnp.mean(o * o, axis=0, keepdims=True) + SUBLN_EPS) * g_ref[...]
                o_ref[0, qs * ATT_TILE:(qs + 1) * ATT_TILE, g * PAIR:(g + 1) * PAIR] = o.T.astype(BF16)

        return sweep, finish

    _query_tiles(tile)


def _moba_kernel(qT_ref, k_ref, vT_ref, bias_ref, o_ref,
                 qz_ref, m_ref, pre_ref, acc_ref, kmean_ref, kgate_ref, selb_ref, *, nk):
    tq = qT_ref.shape[3]
    head_rows = HEAD_DIM + ONES_ROWS

    def tile(qs, qi):
        qz, m_run, acc = qz_ref.at[qs], m_ref.at[qs], acc_ref.at[qs]

        def select_blocks():
            blk = lax.broadcasted_iota(jnp.int32, (nk, tq), 0).astype(F32)
            eligible = blk < qi.astype(F32)
            for m in range(2 * PAIRS_PER_STEP):
                if m % 2 == 0:
                    q_pair = qT_ref[0, qs, (m // 2) * PAIR:(m // 2 + 1) * PAIR, :]
                    gates = jnp.dot(kgate_ref[m // 2], q_pair, preferred_element_type=F32)
                base = (m % 2) * KMEAN_TERMS * nk
                gate = functools.reduce(jnp.add, [gates[base + t * nk:base + (t + 1) * nk]
                                                  for t in range(KMEAN_TERMS)])
                gate = jnp.where(eligible, gate, -jnp.inf)
                picked = jnp.zeros((nk, tq), F32)
                for _ in range(MOBA_TOPK):
                    best = jnp.max(gate, axis=0, keepdims=True)
                    pick = blk == jnp.min(jnp.where(gate == best, blk, float(nk)), axis=0, keepdims=True)
                    picked = jnp.where(pick, 1.0, picked)
                    gate = jnp.where(pick, -jnp.inf, gate)
                selb_ref[m] = jnp.where(eligible, jnp.where(picked > 0.0, 0.0, MASK_VALUE), MASK_VALUE)

        def sweep(extra):
            if qs == 0:
                @pl.when(qi == 0)
                def _():
                    for j in range(nk):
                        kb = k_ref[0, j * MOBA_BLOCK:(j + 1) * MOBA_BLOCK, :].astype(F32)
                        kmean_ref[j:j + 1, :] = jnp.mean(kb, axis=0, keepdims=True)
                    lane = lax.broadcasted_iota(jnp.int32, (nk, PAIR), 1)
                    for g in range(PAIRS_PER_STEP):
                        mean = kmean_ref[:, g * PAIR:(g + 1) * PAIR]
                        for i in range(2):
                            rest = jnp.where((lane >= i * HEAD_DIM) & (lane < (i + 1) * HEAD_DIM), mean, 0.0)
                            for t in range(KMEAN_TERMS):
                                term = rest.astype(BF16)
                                row = (i * KMEAN_TERMS + t) * nk
                                kgate_ref[g, row:row + nk, :] = term
                                rest = rest - term.astype(F32)

                _split_pairs(qT_ref, qs, qz)
                first = [select_blocks]
            else:
                first = [functools.partial(_split_pairs, qT_ref, qs, qz), select_blocks]
            _sweep(qi, k_ref, vT_ref, lambda v, m: v[(m % 2) * head_rows:(m % 2 + 1) * head_rows],
                   lambda m, t, j: (bias_ref[m, t], None if t == 0 else selb_ref[m, pl.ds(j, 1), :]),
                   lambda m, j: (None, selb_ref[m, pl.ds(j, 1), :]),
                   (qz, m_run, acc), pre_ref, with_first_block=first + extra, may_be_first_tile=qs == 0)

        def finish():
            for g in range(PAIRS_PER_STEP):
                o = jnp.concatenate([acc[m, :HEAD_DIM, :] * (1.0 / acc[m, HEAD_DIM:HEAD_DIM + 1, :])
                                     for m in (2 * g, 2 * g + 1)], axis=0)
                o_ref[0, qs * ATT_TILE:(qs + 1) * ATT_TILE, g * PAIR:(g + 1) * PAIR] = o.T.astype(BF16)

        return sweep, finish

    _query_tiles(tile)


def _attention(kind, qT, k, vT, bias, extra, *, lam_init=None):
    B, nk, D, t = qT.shape
    G, S, T = PAIRS_PER_STEP, nk * t, Q_TILES_PER_STEP
    n_maps = 2 * G
    assert D % (G * PAIR) == 0 and nk >= 2 and nk % T == 0 and t == ATT_TILE and QK_AHEAD_PAIR <= QK_AHEAD <= n_maps
    in_specs = [pl.BlockSpec((1, T, G * PAIR, t), lambda b, p, q: (b, q, p, 0)),
                pl.BlockSpec((1, S, G * PAIR), lambda b, p, q: (b, 0, p)),
                pl.BlockSpec((1, G, nk, vT.shape[3], t), lambda b, p, q: (b, p, 0, 0, 0)),
                pl.BlockSpec((n_maps, 2, t, t), lambda b, p, q: (p, 0, 0, 0), pipeline_mode=pl.Buffered(1))]
    scratch = [pltpu.VMEM((T, n_maps, PAIR, t), BF16),
               pltpu.VMEM((T, n_maps, 1, t), F32),
               pltpu.VMEM((max(QK_AHEAD, QK_AHEAD_PAIR), 2, t, t), F32)]
    if kind == "diff":
        body = functools.partial(_diff_kernel, lam_init=lam_init)
        in_specs += [pl.BlockSpec(e.shape, lambda b, p, q: (0, 0)) for e in extra]
        scratch += [pltpu.VMEM((T, n_maps, PAIR + ONES_ROWS, t), F32)]
    else:
        body = functools.partial(_moba_kernel, nk=nk)
        scratch += [pltpu.VMEM((T, n_maps, HEAD_DIM + ONES_ROWS, t), F32),
                    pltpu.VMEM((nk, G * PAIR), F32),
                    pltpu.VMEM((G, 2 * KMEAN_TERMS * nk, PAIR), BF16),
                    pltpu.VMEM((n_maps, nk, t), F32)]
    return pl.pallas_call(
        body,
        grid=(B, D // (G * PAIR), nk // T),
        in_specs=in_specs,
        out_specs=pl.BlockSpec((1, T * t, G * PAIR), lambda b, p, q: (b, q, p)),
        out_shape=jax.ShapeDtypeStruct((B, S, D), BF16),
        scratch_shapes=scratch,
        compiler_params=pltpu.CompilerParams(dimension_semantics=("parallel", "parallel", "arbitrary"),
                                             vmem_limit_bytes=VMEM_LIMIT_BYTES),
        name=kind + "_attention",
    )(qT, k, vT, bias, *extra)


def kernel(x, rel_bias, norm_g, final_norm_g, ffn_w_in, ffn_w_out, diff_w_qkv, diff_lambda,
           diff_subln_g, diff_w_o, moba_w_qkv, moba_w_o):
    B, S, D = x.shape
    depth = norm_g.shape[0]
    assert S % TOKEN_TILE == 0 and D % PAIR == 0 and MOBA_BLOCK == ATT_TILE
    assert rel_bias.shape == (REL_BUCKETS, D // HEAD_DIM)
    bias = _bias_tiles(rel_bias, ATT_TILE)

    order = [(i, j) for i in range(depth) for j in range(2)]
    ffn_weights = {order[0]: (ffn_w_in[0, 0].astype(BF16), ffn_w_out[0, 0].astype(BF16))}

    mixer_stacks = {"diff_qkv": diff_w_qkv, "diff_o": diff_w_o, "moba_qkv": moba_w_qkv, "moba_o": moba_w_o}
    mixer_weights = {}

    def ffn(h, g, member, **kw):
        nxt = order[order.index(member) + 1] if member != order[-1] else None
        cast = [(ffn_w_in, nxt), (ffn_w_out, nxt)] if nxt else []
        riders = [(name, j) for name, stack in mixer_stacks.items() for j in range(stack.shape[0])] \
            if member == order[0] else []
        cast += [(mixer_stacks[name], (j,)) for name, j in riders]
        h, *converted = _ffn(h, g, *ffn_weights.pop(member), (), cast=cast, **kw)
        if nxt:
            ffn_weights[nxt] = (converted[0], converted[1])
        mixer_weights.update(zip(riders, converted[2 if nxt else 0:]))
        return h

    h = x.reshape(B * S, D)
    for i in range(depth):
        g = norm_g[i]
        h = ffn(h, g[0], (i, 0))
        j = i // 2
        if i % 2 == 0:
            qT, k, vT = _qkv_proj(h.reshape(B, S, D), g[1], mixer_weights["diff_qkv", j], PAIR)
            lam_init = 0.8 - 0.6 * math.exp(-0.3 * i)
            g_sub = jnp.broadcast_to((diff_subln_g[j].astype(F32) * (1.0 - lam_init))[:, None], (PAIR, ATT_TILE))
            o = _attention("diff", qT, k, vT, bias, (diff_lambda[j].astype(F32), g_sub), lam_init=lam_init)
            w_o = mixer_weights["diff_o", j]
        else:
            qT, k, vT = _qkv_proj(h.reshape(B, S, D), g[1], mixer_weights["moba_qkv", j], HEAD_DIM)
            o = _attention("moba", qT, k, vT, bias, ())
            w_o = mixer_weights["moba_o", j]
        last = i == depth - 1
        h = ffn(h, g[2], (i, 1), proj=(o.reshape(B * S, D), w_o), final_g=final_norm_g if last else None)
    return h.reshape(B, S, D)
```

```python
import functools
import math

import numpy as np
import jax
import jax.numpy as jnp
from jax import lax
from jax.experimental import pallas as pl
from jax.experimental.pallas import tpu as pltpu

HEAD_DIM = 64
PAIR = 2 * HEAD_DIM
MOBA_BLOCK = 256
MOBA_TOPK = 3
KMEAN_TERMS = 3
ONES_ROWS = 16
REL_BUCKETS = 32
REL_MAX_DIST = 128
FFN_RESIDUAL = 0.5
RMS_EPS = 1e-6
SUBLN_EPS = 1e-5

ATT_TILE = 256
PAIRS_PER_STEP = 8
Q_TILES_PER_STEP = 2
QK_AHEAD = 4
QK_AHEAD_PAIR = 2
FFN_CHUNK = 256
TOKEN_TILE = 1024
FFN_TOKEN_TILE = 1024
MASK_VALUE = -1e30
LOG2E = math.log2(math.e)
VMEM_LIMIT_BYTES = 62 * 1024 * 1024

F32 = jnp.float32
BF16 = jnp.bfloat16
_TT = (((0,), (1,)), ((), ()))


def _rms(x, g, eps):
    return x * lax.rsqrt(jnp.mean(x * x, axis=-1, keepdims=True) + eps) * g


def _const_spec(shape):
    return pl.BlockSpec(shape, lambda *_: (0,) * len(shape), pipeline_mode=pl.Buffered(1))


def _rel_bucket_np(dist):
    n = np.maximum(dist, 0)
    max_exact = REL_BUCKETS // 2
    nf = np.maximum(n, 1).astype(np.float32)
    large = max_exact + (np.log(nf / np.float32(max_exact)) / np.float32(math.log(REL_MAX_DIST / max_exact))
                         * np.float32(REL_BUCKETS - max_exact)).astype(np.int32)
    large = np.minimum(large, REL_BUCKETS - 1)
    return np.where(n < max_exact, n, large).astype(np.int32)


def _bucket_rows(t):
    u = np.arange(2 * t)
    diag = np.where(u >= t, _rel_bucket_np(u - t), -1)
    prev = _rel_bucket_np(u)
    assert _rel_bucket_np(np.arange(t + 1, 8 * t)).min() == REL_BUCKETS - 1
    return np.stack([diag, prev]).astype(np.int32)[:, None, :]


def _bias_kernel(rb_ref, idx_ref, out_ref):
    m = pl.program_id(0)
    t = out_ref.shape[2]
    far = rb_ref[REL_BUCKETS - 1, m]
    for tile in range(2):
        idx = idx_ref[tile]
        row = jnp.zeros(idx.shape, F32)
        for b in range(REL_BUCKETS - 1):
            row = jnp.where(idx == b, (rb_ref[b, m] - far) * LOG2E, row)
        row = jnp.where(idx < 0, MASK_VALUE, row)
        rolled = pltpu.roll(jnp.broadcast_to(row, (t, 2 * t)), 0, 1, stride=1, stride_axis=0)
        out_ref[0, tile] = rolled[:, t:]


def _bias_tiles(rel_bias, t):
    n_maps = rel_bias.shape[1]
    idx = jnp.asarray(_bucket_rows(t))
    return pl.pallas_call(
        _bias_kernel,
        grid=(n_maps,),
        in_specs=[pl.BlockSpec(memory_space=pltpu.SMEM),
                  pl.BlockSpec((2, 1, 2 * t), lambda m: (0, 0, 0))],
        out_specs=pl.BlockSpec((1, 2, t, t), lambda m: (m, 0, 0, 0)),
        out_shape=jax.ShapeDtypeStruct((n_maps, 2, t, t), F32),
        name="rel_bias_tiles",
    )(rel_bias.astype(F32), idx)


def _ffn_kernel(*refs, n_chunks, has_proj, has_final, n_cast):
    refs = list(refs)
    x_ref = refs.pop(0)
    if has_proj:
        o_ref, wo_ref = refs.pop(0), refs.pop(0)
    g_ref, win_ref, wout_ref = refs.pop(0), refs.pop(0), refs.pop(0)
    if has_final:
        gf_ref = refs.pop(0)
    cast_in = [refs.pop(0) for _ in range(n_cast)]
    out_ref = refs.pop(0)
    for src, dst in zip(cast_in, refs):
        dst[...] = src[...].astype(BF16)

    x = x_ref[...]
    if has_proj:
        x = x + jnp.dot(o_ref[...], wo_ref[...], preferred_element_type=F32)
    hn = _rms(x, g_ref[...], RMS_EPS).astype(BF16)
    d_ff = n_chunks * FFN_CHUNK
    acts = []
    for c in range(n_chunks):
        cols = slice(c * FFN_CHUNK, (c + 1) * FFN_CHUNK)
        gate = jnp.dot(hn, win_ref[:, cols], preferred_element_type=F32)
        up = jnp.dot(hn, win_ref[:, d_ff + c * FFN_CHUNK:d_ff + (c + 1) * FFN_CHUNK], preferred_element_type=F32)
        acts.append((gate * (1.0 / (1.0 + jnp.exp(-gate))) * up).astype(BF16))
    acc = jnp.dot(jnp.concatenate(acts, axis=1), wout_ref[...], preferred_element_type=F32)
    y = x + FFN_RESIDUAL * acc
    if has_final:
        y = _rms(y, gf_ref[...], RMS_EPS)
    out_ref[...] = y


def _stacked_spec(stack, index):
    lead = len(index)
    return pl.BlockSpec((None,) * lead + stack.shape[lead:], lambda *_: tuple(index) + (0, 0),
                        pipeline_mode=pl.Buffered(1))


def _row_slabs(stack, index, n_steps):
    lead = len(index)
    R, C = stack.shape[lead:]
    rows = min(r for r in range(16, R + 1, 16) if R % r == 0 and R // r <= n_steps)
    slab = lambda s: jnp.minimum(s, R // rows - 1)
    return (pl.BlockSpec((None,) * lead + (rows, C), lambda s: tuple(index) + (slab(s), 0)),
            pl.BlockSpec((rows, C), lambda s: (slab(s), 0)), jax.ShapeDtypeStruct((R, C), BF16))


def _ffn(x2d, g, w_in_all, w_out_all, which, proj=None, final_g=None, cast=()):
    T, D = x2d.shape
    F = w_out_all.shape[-2]
    n_chunks = F // FFN_CHUNK
    assert n_chunks * FFN_CHUNK == F and w_in_all.shape[-2:] == (D, 2 * F)
    tm = math.gcd(T, FFN_TOKEN_TILE)

    row = lambda i: (i, 0)
    args, specs = [x2d], [pl.BlockSpec((tm, D), row)]
    if proj is not None:
        o2d, w_o = proj
        args += [o2d, w_o]
        specs += [pl.BlockSpec((tm, o2d.shape[1]), row), _const_spec(w_o.shape)]
    args += [g.reshape(1, D).astype(F32), w_in_all, w_out_all]
    specs += [_const_spec((1, D)), _stacked_spec(w_in_all, which), _stacked_spec(w_out_all, which)]
    if final_g is not None:
        args.append(final_g.reshape(1, D).astype(F32))
        specs.append(_const_spec((1, D)))

    slabs = [_row_slabs(stack, index, T // tm) for stack, index in cast]
    args += [stack for stack, _ in cast]
    specs += [in_spec for in_spec, _, _ in slabs]

    return pl.pallas_call(
        functools.partial(_ffn_kernel, n_chunks=n_chunks, has_proj=proj is not None,
                          has_final=final_g is not None, n_cast=len(cast)),
        grid=(T // tm,),
        in_specs=specs,
        out_specs=[pl.BlockSpec((tm, D), row)] + [out_spec for _, out_spec, _ in slabs],
        out_shape=[jax.ShapeDtypeStruct((T, D), F32)] + [shape for _, _, shape in slabs],
        compiler_params=pltpu.CompilerParams(dimension_semantics=("arbitrary",),
                                             vmem_limit_bytes=VMEM_LIMIT_BYTES),
        name="ffn",
    )(*args)


def _proj_kernel(x_ref, g_ref, w_ref, qT_ref, k_ref, vT_ref, *, n_pairs, n_sub, v_dim):
    D = x_ref.shape[2]
    hn = _rms(x_ref[0], g_ref[...], RMS_EPS).astype(BF16)
    k_ref[0] = jnp.dot(hn, w_ref[:, D:2 * D], preferred_element_type=F32).astype(BF16)
    qT = lax.dot_general(w_ref[:, :D], hn, _TT, preferred_element_type=F32) * (HEAD_DIM ** -0.5 * LOG2E)
    qT = qT.astype(BF16)
    for c in range(n_sub):
        qT_ref[0, c] = qT[:, c * ATT_TILE:(c + 1) * ATT_TILE]
    vT = lax.dot_general(w_ref[:, 2 * D:], hn, _TT, preferred_element_type=F32).astype(BF16)
    ones_rows = jnp.where(lax.broadcasted_iota(jnp.int32, (ONES_ROWS, ATT_TILE), 0) == 0, 1.0, 0.0).astype(BF16)
    group = v_dim + ONES_ROWS
    for p in range(n_pairs):
        for c in range(n_sub):
            cols = slice(c * ATT_TILE, (c + 1) * ATT_TILE)
            for i in range(PAIR // v_dim):
                vT_ref[0, p, c, i * group:i * group + v_dim, :] = vT[p * PAIR + i * v_dim:p * PAIR + (i + 1) * v_dim, cols]
                vT_ref[0, p, c, i * group + v_dim:(i + 1) * group, :] = ones_rows


def _v_rows(v_dim):
    return (PAIR // v_dim) * (v_dim + ONES_ROWS)


def _qkv_proj(h, g, w_qkv, v_dim):
    B, S, D = h.shape
    tm = TOKEN_TILE
    n_pairs, n_sub, nk, rows = D // PAIR, tm // ATT_TILE, S // ATT_TILE, _v_rows(v_dim)
    return pl.pallas_call(
        functools.partial(_proj_kernel, n_pairs=n_pairs, n_sub=n_sub, v_dim=v_dim),
        grid=(B, S // tm),
        in_specs=[pl.BlockSpec((1, tm, D), lambda b, s: (b, s, 0)),
                  _const_spec((1, D)), _const_spec((D, 3 * D))],
        out_specs=[pl.BlockSpec((1, n_sub, D, ATT_TILE), lambda b, s: (b, s, 0, 0)),
                   pl.BlockSpec((1, tm, D), lambda b, s: (b, s, 0)),
                   pl.BlockSpec((1, n_pairs, n_sub, rows, ATT_TILE), lambda b, s: (b, 0, s, 0, 0))],
        out_shape=[jax.ShapeDtypeStruct((B, nk, D, ATT_TILE), BF16),
                   jax.ShapeDtypeStruct((B, S, D), BF16),
                   jax.ShapeDtypeStruct((B, n_pairs, nk, rows, ATT_TILE), BF16)],
        compiler_params=pltpu.CompilerParams(dimension_semantics=("parallel", "parallel"),
                                             vmem_limit_bytes=VMEM_LIMIT_BYTES),
        name="qkv_proj",
    )(h, g.reshape(1, D).astype(F32), w_qkv)


def _split_pairs(qT_ref, qs, qz_ref):
    zeros = jnp.zeros((HEAD_DIM, qT_ref.shape[3]), qT_ref.dtype)
    for g in range(PAIRS_PER_STEP):
        q = qT_ref[0, qs, g * PAIR:(g + 1) * PAIR, :]
        qz_ref[2 * g, :HEAD_DIM, :] = q[:HEAD_DIM]
        qz_ref[2 * g, HEAD_DIM:, :] = zeros
        qz_ref[2 * g + 1, :HEAD_DIM, :] = zeros
        qz_ref[2 * g + 1, HEAD_DIM:, :] = q[HEAD_DIM:]


def _update(m, scores, vT_tiles, biases, m_ref, acc_ref, first):
    scores = [s if b is None else b + s for s, (b, _) in zip(scores, biases)]
    masks = [r for _, r in biases]
    if all(r is None for r in masks):
        m_cur = jnp.max(functools.reduce(jnp.maximum, scores), axis=0, keepdims=True)
    else:
        m_cur = functools.reduce(jnp.maximum, [jnp.max(s, axis=0, keepdims=True) + (0.0 if r is None else r)
                                               for s, r in zip(scores, masks)])
    if first:
        m_new = m_cur
    else:
        m_old = m_ref[m]
        m_new = jnp.maximum(m_old, m_cur)
    shifts = [m_new if r is None else m_new - r for r in masks]
    pv = functools.reduce(jnp.add, [jnp.dot(v, jnp.exp2(s - sh).astype(BF16), preferred_element_type=F32)
                                    for v, s, sh in zip(vT_tiles, scores, shifts)])
    if first:
        acc_ref[m] = pv
    else:
        acc_ref[m] = jnp.exp2(m_old - m_new) * acc_ref[m] + pv
    m_ref[m] = m_new


def _sweep(qi, k_ref, vT_ref, v_rows, near_bias, far_bias, state, pre_ref, with_first_block=(), may_be_first_tile=True):
    qz_ref, m_ref, acc_ref = state
    n_maps = 2 * PAIRS_PER_STEP

    def raw_scores(j, m):
        rows = pl.ds(pl.multiple_of(j * ATT_TILE, ATT_TILE), ATT_TILE)
        k_t = k_ref[0, rows, (m // 2) * PAIR:(m // 2 + 1) * PAIR]
        return jnp.dot(k_t, qz_ref[m], preferred_element_type=F32)

    def block(tiles, biases, first, ahead, preloaded, next_tiles, next_ahead):
        scores = {}
        for m in range(ahead):
            scores[m] = ([pre_ref[m, t] for t in range(len(tiles))] if preloaded
                         else [raw_scores(j, m) for j in tiles])
        for m in range(n_maps):
            nxt = m + ahead
            if nxt < n_maps:
                scores[nxt] = [raw_scores(j, nxt) for j in tiles]
            elif nxt - n_maps < next_ahead:
                for t, j in enumerate(next_tiles):
                    pre_ref[nxt - n_maps, t] = raw_scores(j, nxt - n_maps)
            _update(m, scores.pop(m), [v_rows(vT_ref[0, m // 2, j], m) for j in tiles], biases(m),
                    m_ref, acc_ref, first)

    n_far = jnp.maximum(qi - 1, 0)
    n_pairs = n_far // 2

    def first_pair():
        for traced_alongside in with_first_block:
            traced_alongside()
        block([qi - 1, qi], lambda m: [near_bias(m, 1, qi - 1), near_bias(m, 0, qi)], True, QK_AHEAD_PAIR, False,
              [0, 1], QK_AHEAD_PAIR)

    if may_be_first_tile:
        @pl.when(qi == 0)
        def _():
            block([qi], lambda m: [near_bias(m, 0, qi)], True, QK_AHEAD, False, [], 0)

        pl.when(qi >= 1)(first_pair)
    else:
        first_pair()

    @pl.when(n_far % 2 == 1)
    def _():
        block([n_far - 1], lambda m: [far_bias(m, n_far - 1)], False, QK_AHEAD, False, [], 0)

    def far_pair(p, carry):
        j, jn = 2 * p, 2 * jnp.minimum(p + 1, n_pairs - 1)
        block([j, j + 1], lambda m: [far_bias(m, j), far_bias(m, j + 1)], False, QK_AHEAD_PAIR, True,
              [jn, jn + 1], QK_AHEAD_PAIR)
        return carry

    lax.fori_loop(0, n_pairs, far_pair, 0)


def _query_tiles(step_body):
    first = pl.program_id(2) * Q_TILES_PER_STEP
    pending = []
    for qs in range(Q_TILES_PER_STEP):
        sweep, finish = step_body(qs, first + qs)
        sweep(pending)
        pending = [finish]
    pending[0]()


def _diff_kernel(qT_ref, k_ref, vT_ref, bias_ref, lam_ref, g_ref, o_ref,
                 qz_ref, m_ref, pre_ref, acc_ref, *, lam_init):
    def tile(qs, qi):
        qz, m_run, acc = qz_ref.at[qs], m_ref.at[qs], acc_ref.at[qs]

        def sweep(extra):
            if qs == 0:
                _split_pairs(qT_ref, qs, qz)
                first = []
            else:
                first = [functools.partial(_split_pairs, qT_ref, qs, qz)]
            _sweep(qi, k_ref, vT_ref, lambda v, m: v,
                   lambda m, t, j: (bias_ref[m, t], None), lambda m, j: (None, None),
                   (qz, m_run, acc), pre_ref, with_first_block=first + extra, may_be_first_tile=qs == 0)

        def finish():
            lp = lam_ref[...]
            lam = (jnp.exp(jnp.sum(lp[0:1] * lp[1:2], axis=-1, keepdims=True))
                   - jnp.exp(jnp.sum(lp[2:3] * lp[3:4], axis=-1, keepdims=True)) + lam_init)

            def weighted(m, scale):
                return acc[m, :PAIR, :] * (scale / acc[m, PAIR:PAIR + 1, :])

            for g in range(PAIRS_PER_STEP):
                o = weighted(2 * g, 1.0) - weighted(2 * g + 1, lam)
                o = o * lax.rsqrt(jnp.mean(o * o, axis=0, keepdims=True) + SUBLN_EPS) * g_ref[...]
                o_ref[0, qs * ATT_TILE:(qs + 1) * ATT_TILE, g * PAIR:(g + 1) * PAIR] = o.T.astype(BF16)

        return sweep, finish

    _query_tiles(tile)


def _moba_kernel(qT_ref, k_ref, vT_ref, bias_ref, o_ref,
                 qz_ref, m_ref, pre_ref, acc_ref, kmean_ref, kgate_ref, selb_ref, *, nk):
    tq = qT_ref.shape[3]
    head_rows = HEAD_DIM + ONES_ROWS

    def tile(qs, qi):
        qz, m_run, acc = qz_ref.at[qs], m_ref.at[qs], acc_ref.at[qs]

        def select_blocks():
            blk = lax.broadcasted_iota(jnp.int32, (nk, tq), 0).astype(F32)
            eligible = blk < qi.astype(F32)
            for m in range(2 * PAIRS_PER_STEP):
                if m % 2 == 0:
                    q_pair = qT_ref[0, qs, (m // 2) * PAIR:(m // 2 + 1) * PAIR, :]
                    gates = jnp.dot(kgate_ref[m // 2], q_pair, preferred_element_type=F32)
                base = (m % 2) * KMEAN_TERMS * nk
                gate = functools.reduce(jnp.add, [gates[base + t * nk:base + (t + 1) * nk]
                                                  for t in range(KMEAN_TERMS)])
                gate = jnp.where(eligible, gate, -jnp.inf)
                picked = jnp.zeros((nk, tq), F32)
                for _ in range(MOBA_TOPK):
                    best = jnp.max(gate, axis=0, keepdims=True)
                    pick = blk == jnp.min(jnp.where(gate == best, blk, float(nk)), axis=0, keepdims=True)
                    picked = jnp.where(pick, 1.0, picked)
                    gate = jnp.where(pick, -jnp.inf, gate)
                selb_ref[m] = jnp.where(eligible, jnp.where(picked > 0.0, 0.0, MASK_VALUE), MASK_VALUE)

        def sweep(extra):
            if qs == 0:
                @pl.when(qi == 0)
                def _():
                    for j in range(nk):
                        kb = k_ref[0, j * MOBA_BLOCK:(j + 1) * MOBA_BLOCK, :].astype(F32)
                        kmean_ref[j:j + 1, :] = jnp.mean(kb, axis=0, keepdims=True)
                    lane = lax.broadcasted_iota(jnp.int32, (nk, PAIR), 1)
                    for g in range(PAIRS_PER_STEP):
                        mean = kmean_ref[:, g * PAIR:(g + 1) * PAIR]
                        for i in range(2):
                            rest = jnp.where((lane >= i * HEAD_DIM) & (lane < (i + 1) * HEAD_DIM), mean, 0.0)
                            for t in range(KMEAN_TERMS):
                                term = rest.astype(BF16)
                                row = (i * KMEAN_TERMS + t) * nk
                                kgate_ref[g, row:row + nk, :] = term
                                rest = rest - term.astype(F32)

                _split_pairs(qT_ref, qs, qz)
                first = [select_blocks]
            else:
                first = [functools.partial(_split_pairs, qT_ref, qs, qz), select_blocks]
            _sweep(qi, k_ref, vT_ref, lambda v, m: v[(m % 2) * head_rows:(m % 2 + 1) * head_rows],
                   lambda m, t, j: (bias_ref[m, t], None if t == 0 else selb_ref[m, pl.ds(j, 1), :]),
                   lambda m, j: (None, selb_ref[m, pl.ds(j, 1), :]),
                   (qz, m_run, acc), pre_ref, with_first_block=first + extra, may_be_first_tile=qs == 0)

        def finish():
            for g in range(PAIRS_PER_STEP):
                o = jnp.concatenate([acc[m, :HEAD_DIM, :] * (1.0 / acc[m, HEAD_DIM:HEAD_DIM + 1, :])
                                     for m in (2 * g, 2 * g + 1)], axis=0)
                o_ref[0, qs * ATT_TILE:(qs + 1) * ATT_TILE, g * PAIR:(g + 1) * PAIR] = o.T.astype(BF16)

        return sweep, finish

    _query_tiles(tile)


def _attention(kind, qT, k, vT, bias, extra, *, lam_init=None):
    B, nk, D, t = qT.shape
    G, S, T = PAIRS_PER_STEP, nk * t, Q_TILES_PER_STEP
    n_maps = 2 * G
    assert D % (G * PAIR) == 0 and nk >= 2 and nk % T == 0 and t == ATT_TILE and QK_AHEAD_PAIR <= QK_AHEAD <= n_maps
    in_specs = [pl.BlockSpec((1, T, G * PAIR, t), lambda b, p, q: (b, q, p, 0)),
                pl.BlockSpec((1, S, G * PAIR), lambda b, p, q: (b, 0, p)),
                pl.BlockSpec((1, G, nk, vT.shape[3], t), lambda b, p, q: (b, p, 0, 0, 0)),
                pl.BlockSpec((n_maps, 2, t, t), lambda b, p, q: (p, 0, 0, 0), pipeline_mode=pl.Buffered(1))]
    scratch = [pltpu.VMEM((T, n_maps, PAIR, t), BF16),
               pltpu.VMEM((T, n_maps, 1, t), F32),
               pltpu.VMEM((max(QK_AHEAD, QK_AHEAD_PAIR), 2, t, t), F32)]
    if kind == "diff":
        body = functools.partial(_diff_kernel, lam_init=lam_init)
        in_specs += [pl.BlockSpec(e.shape, lambda b, p, q: (0, 0)) for e in extra]
        scratch += [pltpu.VMEM((T, n_maps, PAIR + ONES_ROWS, t), F32)]
    else:
        body = functools.partial(_moba_kernel, nk=nk)
        scratch += [pltpu.VMEM((T, n_maps, HEAD_DIM + ONES_ROWS, t), F32),
                    pltpu.VMEM((nk, G * PAIR), F32),
                    pltpu.VMEM((G, 2 * KMEAN_TERMS * nk, PAIR), BF16),
                    pltpu.VMEM((n_maps, nk, t), F32)]
    return pl.pallas_call(
        body,
        grid=(B, D // (G * PAIR), nk // T),
        in_specs=in_specs,
        out_specs=pl.BlockSpec((1, T * t, G * PAIR), lambda b, p, q: (b, q, p)),
        out_shape=jax.ShapeDtypeStruct((B, S, D), BF16),
        scratch_shapes=scratch,
        compiler_params=pltpu.CompilerParams(dimension_semantics=("parallel", "parallel", "arbitrary"),
                                             vmem_limit_bytes=VMEM_LIMIT_BYTES),
        name=kind + "_attention",
    )(qT, k, vT, bias, *extra)


def kernel(x, rel_bias, norm_g, final_norm_g, ffn_w_in, ffn_w_out, diff_w_qkv, diff_lambda,
           diff_subln_g, diff_w_o, moba_w_qkv, moba_w_o):
    B, S, D = x.shape
    depth = norm_g.shape[0]
    assert S % TOKEN_TILE == 0 and D % PAIR == 0 and MOBA_BLOCK == ATT_TILE
    assert rel_bias.shape == (REL_BUCKETS, D // HEAD_DIM)
    bias = _bias_tiles(rel_bias, ATT_TILE)

    order = [(i, j) for i in range(depth) for j in range(2)]
    ffn_weights = {order[0]: (ffn_w_in[0, 0].astype(BF16), ffn_w_out[0, 0].astype(BF16))}

    mixer_stacks = {"diff_qkv": diff_w_qkv, "diff_o": diff_w_o, "moba_qkv": moba_w_qkv, "moba_o": moba_w_o}
    mixer_weights = {}

    def ffn(h, g, member, **kw):
        nxt = order[order.index(member) + 1] if member != order[-1] else None
        cast = [(ffn_w_in, nxt), (ffn_w_out, nxt)] if nxt else []
        riders = [(name, j) for name, stack in mixer_stacks.items() for j in range(stack.shape[0])] \
            if member == order[0] else []
        cast += [(mixer_stacks[name], (j,)) for name, j in riders]
        h, *converted = _ffn(h, g, *ffn_weights.pop(member), (), cast=cast, **kw)
        if nxt:
            ffn_weights[nxt] = (converted[0], converted[1])
        mixer_weights.update(zip(riders, converted[2 if nxt else 0:]))
        return h

    h = x.reshape(B * S, D)
    for i in range(depth):
        g = norm_g[i]
        h = ffn(h, g[0], (i, 0))
        j = i // 2
        if i % 2 == 0:
            qT, k, vT = _qkv_proj(h.reshape(B, S, D), g[1], mixer_weights["diff_qkv", j], PAIR)
            lam_init = 0.8 - 0.6 * math.exp(-0.3 * i)
            g_sub = jnp.broadcast_to((diff_subln_g[j].astype(F32) * (1.0 - lam_init))[:, None], (PAIR, ATT_TILE))
            o = _attention("diff", qT, k, vT, bias, (diff_lambda[j].astype(F32), g_sub), lam_init=lam_init)
            w_o = mixer_weights["diff_o", j]
        else:
            qT, k, vT = _qkv_proj(h.reshape(B, S, D), g[1], mixer_weights["moba_qkv", j], HEAD_DIM)
            o = _attention("moba", qT, k, vT, bias, ())
            w_o = mixer_weights["moba_o", j]
        last = i == depth - 1
        h = ffn(h, g[2], (i, 1), proj=(o.reshape(B * S, D), w_o), final_g=final_norm_g if last else None)
    return h.reshape(B, S, D)
```

```python
import functools
import math

import numpy as np
import jax
import jax.numpy as jnp
from jax import lax
from jax.experimental import pallas as pl
from jax.experimental.pallas import tpu as pltpu

HEAD_DIM = 64
PAIR = 2 * HEAD_DIM
MOBA_BLOCK = 256
MOBA_TOPK = 3
KMEAN_TERMS = 3
ONES_ROWS = 16
REL_BUCKETS = 32
REL_MAX_DIST = 128
FFN_RESIDUAL = 0.5
RMS_EPS = 1e-6
SUBLN_EPS = 1e-5

ATT_TILE = 256
PAIRS_PER_STEP = 8
Q_TILES_PER_STEP = 2
QK_AHEAD = 4
QK_AHEAD_PAIR = 2
FFN_CHUNK = 256
TOKEN_TILE = 1024
FFN_TOKEN_TILE = 1024
MASK_VALUE = -1e30
LOG2E = math.log2(math.e)
VMEM_LIMIT_BYTES = 62 * 1024 * 1024

F32 = jnp.float32
BF16 = jnp.bfloat16
_TT = (((0,), (1,)), ((), ()))


def _rms(x, g, eps):
    return x * lax.rsqrt(jnp.mean(x * x, axis=-1, keepdims=True) + eps) * g


def _const_spec(shape):
    return pl.BlockSpec(shape, lambda *_: (0,) * len(shape), pipeline_mode=pl.Buffered(1))


def _rel_bucket_np(dist):
    n = np.maximum(dist, 0)
    max_exact = REL_BUCKETS // 2
    nf = np.maximum(n, 1).astype(np.float32)
    large = max_exact + (np.log(nf / np.float32(max_exact)) / np.float32(math.log(REL_MAX_DIST / max_exact))
                         * np.float32(REL_BUCKETS - max_exact)).astype(np.int32)
    large = np.minimum(large, REL_BUCKETS - 1)
    return np.where(n < max_exact, n, large).astype(np.int32)


def _bucket_rows(t):
    u = np.arange(2 * t)
    diag = np.where(u >= t, _rel_bucket_np(u - t), -1)
    prev = _rel_bucket_np(u)
    assert _rel_bucket_np(np.arange(t + 1, 8 * t)).min() == REL_BUCKETS - 1
    return np.stack([diag, prev]).astype(np.int32)[:, None, :]


def _bias_kernel(rb_ref, idx_ref, out_ref):
    m = pl.program_id(0)
    t = out_ref.shape[2]
    far = rb_ref[REL_BUCKETS - 1, m]
    for tile in range(2):
        idx = idx_ref[tile]
        row = jnp.zeros(idx.shape, F32)
        for b in range(REL_BUCKETS - 1):
            row = jnp.where(idx == b, (rb_ref[b, m] - far) * LOG2E, row)
        row = jnp.where(idx < 0, MASK_VALUE, row)
        rolled = pltpu.roll(jnp.broadcast_to(row, (t, 2 * t)), 0, 1, stride=1, stride_axis=0)
        out_ref[0, tile] = rolled[:, t:]


def _bias_tiles(rel_bias, t):
    n_maps = rel_bias.shape[1]
    idx = jnp.asarray(_bucket_rows(t))
    return pl.pallas_call(
        _bias_kernel,
        grid=(n_maps,),
        in_specs=[pl.BlockSpec(memory_space=pltpu.SMEM),
                  pl.BlockSpec((2, 1, 2 * t), lambda m: (0, 0, 0))],
        out_specs=pl.BlockSpec((1, 2, t, t), lambda m: (m, 0, 0, 0)),
        out_shape=jax.ShapeDtypeStruct((n_maps, 2, t, t), F32),
        name="rel_bias_tiles",
    )(rel_bias.astype(F32), idx)


def _ffn_kernel(*refs, n_chunks, has_proj, has_final, n_cast):
    refs = list(refs)
    x_ref = refs.pop(0)
    if has_proj:
        o_ref, wo_ref = refs.pop(0), refs.pop(0)
    g_ref, win_ref, wout_ref = refs.pop(0), refs.pop(0), refs.pop(0)
    if has_final:
        gf_ref = refs.pop(0)
    cast_in = [refs.pop(0) for _ in range(n_cast)]
    out_ref = refs.pop(0)
    for src, dst in zip(cast_in, refs):
        dst[...] = src[...].astype(BF16)

    x = x_ref[...]
    if has_proj:
        x = x + jnp.dot(o_ref[...], wo_ref[...], preferred_element_type=F32)
    hn = _rms(x, g_ref[...], RMS_EPS).astype(BF16)
    d_ff = n_chunks * FFN_CHUNK
    acts = []
    for c in range(n_chunks):
        cols = slice(c * FFN_CHUNK, (c + 1) * FFN_CHUNK)
        gate = jnp.dot(hn, win_ref[:, cols], preferred_element_type=F32)
        up = jnp.dot(hn, win_ref[:, d_ff + c * FFN_CHUNK:d_ff + (c + 1) * FFN_CHUNK], preferred_element_type=F32)
        acts.append((gate * (0.5 * jnp.tanh(0.5 * gate) + 0.5) * up).astype(BF16))
    acc = jnp.dot(jnp.concatenate(acts, axis=1), wout_ref[...], preferred_element_type=F32)
    y = x + FFN_RESIDUAL * acc
    if has_final:
        y = _rms(y, gf_ref[...], RMS_EPS)
    out_ref[...] = y


def _stacked_spec(stack, index):
    lead = len(index)
    return pl.BlockSpec((None,) * lead + stack.shape[lead:], lambda *_: tuple(index) + (0, 0),
                        pipeline_mode=pl.Buffered(1))


def _row_slabs(stack, index, n_steps):
    lead = len(index)
    R, C = stack.shape[lead:]
    rows = min(r for r in range(16, R + 1, 16) if R % r == 0 and R // r <= n_steps)
    slab = lambda s: jnp.minimum(s, R // rows - 1)
    return (pl.BlockSpec((None,) * lead + (rows, C), lambda s: tuple(index) + (slab(s), 0)),
            pl.BlockSpec((rows, C), lambda s: (slab(s), 0)), jax.ShapeDtypeStruct((R, C), BF16))


def _ffn(x2d, g, w_in_all, w_out_all, which, proj=None, final_g=None, cast=()):
    T, D = x2d.shape
    F = w_out_all.shape[-2]
    n_chunks = F // FFN_CHUNK
    assert n_chunks * FFN_CHUNK == F and w_in_all.shape[-2:] == (D, 2 * F)
    tm = math.gcd(T, FFN_TOKEN_TILE)

    row = lambda i: (i, 0)
    args, specs = [x2d], [pl.BlockSpec((tm, D), row)]
    if proj is not None:
        o2d, w_o = proj
        args += [o2d, w_o]
        specs += [pl.BlockSpec((tm, o2d.shape[1]), row), _const_spec(w_o.shape)]
    args += [g.reshape(1, D).astype(F32), w_in_all, w_out_all]
    specs += [_const_spec((1, D)), _stacked_spec(w_in_all, which), _stacked_spec(w_out_all, which)]
    if final_g is not None:
        args.append(final_g.reshape(1, D).astype(F32))
        specs.append(_const_spec((1, D)))

    slabs = [_row_slabs(stack, index, T // tm) for stack, index in cast]
    args += [stack for stack, _ in cast]
    specs += [in_spec for in_spec, _, _ in slabs]

    return pl.pallas_call(
        functools.partial(_ffn_kernel, n_chunks=n_chunks, has_proj=proj is not None,
                          has_final=final_g is not None, n_cast=len(cast)),
        grid=(T // tm,),
        in_specs=specs,
        out_specs=[pl.BlockSpec((tm, D), row)] + [out_spec for _, out_spec, _ in slabs],
        out_shape=[jax.ShapeDtypeStruct((T, D), F32)] + [shape for _, _, shape in slabs],
        compiler_params=pltpu.CompilerParams(dimension_semantics=("arbitrary",),
                                             vmem_limit_bytes=VMEM_LIMIT_BYTES),
        name="ffn",
    )(*args)


def _proj_kernel(x_ref, g_ref, w_ref, qT_ref, k_ref, vT_ref, *, n_pairs, n_sub, v_dim):
    D = x_ref.shape[2]
    hn = _rms(x_ref[0], g_ref[...], RMS_EPS).astype(BF16)
    k_ref[0] = jnp.dot(hn, w_ref[:, D:2 * D], preferred_element_type=F32).astype(BF16)
    qT = lax.dot_general(w_ref[:, :D], hn, _TT, preferred_element_type=F32) * (HEAD_DIM ** -0.5 * LOG2E)
    qT = qT.astype(BF16)
    for c in range(n_sub):
        qT_ref[0, c] = qT[:, c * ATT_TILE:(c + 1) * ATT_TILE]
    vT = lax.dot_general(w_ref[:, 2 * D:], hn, _TT, preferred_element_type=F32).astype(BF16)
    ones_rows = jnp.where(lax.broadcasted_iota(jnp.int32, (ONES_ROWS, ATT_TILE), 0) == 0, 1.0, 0.0).astype(BF16)
    group = v_dim + ONES_ROWS
    for p in range(n_pairs):
        for c in range(n_sub):
            cols = slice(c * ATT_TILE, (c + 1) * ATT_TILE)
            for i in range(PAIR // v_dim):
                vT_ref[0, p, c, i * group:i * group + v_dim, :] = vT[p * PAIR + i * v_dim:p * PAIR + (i + 1) * v_dim, cols]
                vT_ref[0, p, c, i * group + v_dim:(i + 1) * group, :] = ones_rows


def _v_rows(v_dim):
    return (PAIR // v_dim) * (v_dim + ONES_ROWS)


def _qkv_proj(h, g, w_qkv, v_dim):
    B, S, D = h.shape
    tm = TOKEN_TILE
    n_pairs, n_sub, nk, rows = D // PAIR, tm // ATT_TILE, S // ATT_TILE, _v_rows(v_dim)
    return pl.pallas_call(
        functools.partial(_proj_kernel, n_pairs=n_pairs, n_sub=n_sub, v_dim=v_dim),
        grid=(B, S // tm),
        in_specs=[pl.BlockSpec((1, tm, D), lambda b, s: (b, s, 0)),
                  _const_spec((1, D)), _const_spec((D, 3 * D))],
        out_specs=[pl.BlockSpec((1, n_sub, D, ATT_TILE), lambda b, s: (b, s, 0, 0)),
                   pl.BlockSpec((1, tm, D), lambda b, s: (b, s, 0)),
                   pl.BlockSpec((1, n_pairs, n_sub, rows, ATT_TILE), lambda b, s: (b, 0, s, 0, 0))],
        out_shape=[jax.ShapeDtypeStruct((B, nk, D, ATT_TILE), BF16),
                   jax.ShapeDtypeStruct((B, S, D), BF16),
                   jax.ShapeDtypeStruct((B, n_pairs, nk, rows, ATT_TILE), BF16)],
        compiler_params=pltpu.CompilerParams(dimension_semantics=("parallel", "parallel"),
                                             vmem_limit_bytes=VMEM_LIMIT_BYTES),
        name="qkv_proj",
    )(h, g.reshape(1, D).astype(F32), w_qkv)


def _split_pairs(qT_ref, qs, qz_ref):
    zeros = jnp.zeros((HEAD_DIM, qT_ref.shape[3]), qT_ref.dtype)
    for g in range(PAIRS_PER_STEP):
        q = qT_ref[0, qs, g * PAIR:(g + 1) * PAIR, :]
        qz_ref[2 * g, :HEAD_DIM, :] = q[:HEAD_DIM]
        qz_ref[2 * g, HEAD_DIM:, :] = zeros
        qz_ref[2 * g + 1, :HEAD_DIM, :] = zeros
        qz_ref[2 * g + 1, HEAD_DIM:, :] = q[HEAD_DIM:]


def _update(m, scores, vT_tiles, biases, m_ref, acc_ref, first):
    scores = [s if b is None else b + s for s, (b, _) in zip(scores, biases)]
    masks = [r for _, r in biases]
    if all(r is None for r in masks):
        m_cur = jnp.max(functools.reduce(jnp.maximum, scores), axis=0, keepdims=True)
    else:
        m_cur = functools.reduce(jnp.maximum, [jnp.max(s, axis=0, keepdims=True) + (0.0 if r is None else r)
                                               for s, r in zip(scores, masks)])
    if first:
        m_new = m_cur
    else:
        m_old = m_ref[m]
        m_new = jnp.maximum(m_old, m_cur)
    shifts = [m_new if r is None else m_new - r for r in masks]
    pv = functools.reduce(jnp.add, [jnp.dot(v, jnp.exp2(s - sh).astype(BF16), preferred_element_type=F32)
                                    for v, s, sh in zip(vT_tiles, scores, shifts)])
    if first:
        acc_ref[m] = pv
    else:
        acc_ref[m] = jnp.exp2(m_old - m_new) * acc_ref[m] + pv
    m_ref[m] = m_new


def _sweep(qi, k_ref, vT_ref, v_rows, near_bias, far_bias, state, pre_ref, with_first_block=(), may_be_first_tile=True):
    qz_ref, m_ref, acc_ref = state
    n_maps = 2 * PAIRS_PER_STEP

    def raw_scores(j, m):
        rows = pl.ds(pl.multiple_of(j * ATT_TILE, ATT_TILE), ATT_TILE)
        k_t = k_ref[0, rows, (m // 2) * PAIR:(m // 2 + 1) * PAIR]
        return jnp.dot(k_t, qz_ref[m], preferred_element_type=F32)

    def block(tiles, biases, first, ahead, preloaded, next_tiles, next_ahead):
        scores = {}
        for m in range(ahead):
            scores[m] = ([pre_ref[m, t] for t in range(len(tiles))] if preloaded
                         else [raw_scores(j, m) for j in tiles])
        for m in range(n_maps):
            nxt = m + ahead
            if nxt < n_maps:
                scores[nxt] = [raw_scores(j, nxt) for j in tiles]
            elif nxt - n_maps < next_ahead:
                for t, j in enumerate(next_tiles):
                    pre_ref[nxt - n_maps, t] = raw_scores(j, nxt - n_maps)
            _update(m, scores.pop(m), [v_rows(vT_ref[0, m // 2, j], m) for j in tiles], biases(m),
                    m_ref, acc_ref, first)

    n_far = jnp.maximum(qi - 1, 0)
    n_pairs = n_far // 2

    def first_pair():
        for traced_alongside in with_first_block:
            traced_alongside()
        block([qi - 1, qi], lambda m: [near_bias(m, 1, qi - 1), near_bias(m, 0, qi)], True, QK_AHEAD_PAIR, False,
              [0, 1], QK_AHEAD_PAIR)

    if may_be_first_tile:
        @pl.when(qi == 0)
        def _():
            block([qi], lambda m: [near_bias(m, 0, qi)], True, QK_AHEAD, False, [], 0)

        pl.when(qi >= 1)(first_pair)
    else:
        first_pair()

    @pl.when(n_far % 2 == 1)
    def _():
        block([n_far - 1], lambda m: [far_bias(m, n_far - 1)], False, QK_AHEAD, False, [], 0)

    def far_pair(p, carry):
        j, jn = 2 * p, 2 * jnp.minimum(p + 1, n_pairs - 1)
        block([j, j + 1], lambda m: [far_bias(m, j), far_bias(m, j + 1)], False, QK_AHEAD_PAIR, True,
              [jn, jn + 1], QK_AHEAD_PAIR)
        return carry

    lax.fori_loop(0, n_pairs, far_pair, 0)


def _query_tiles(step_body):
    first = pl.program_id(2) * Q_TILES_PER_STEP
    pending = []
    for qs in range(Q_TILES_PER_STEP):
        sweep, finish = step_body(qs, first + qs)
        sweep(pending)
        pending = [finish]
    pending[0]()


def _diff_kernel(qT_ref, k_ref, vT_ref, bias_ref, lam_ref, g_ref, o_ref,
                 qz_ref, m_ref, pre_ref, acc_ref, *, lam_init):
    def tile(qs, qi):
        qz, m_run, acc = qz_ref.at[qs], m_ref.at[qs], acc_ref.at[qs]

        def sweep(extra):
            if qs == 0:
                _split_pairs(qT_ref, qs, qz)
                first = []
            else:
                first = [functools.partial(_split_pairs, qT_ref, qs, qz)]
            _sweep(qi, k_ref, vT_ref, lambda v, m: v,
                   lambda m, t, j: (bias_ref[m, t], None), lambda m, j: (None, None),
                   (qz, m_run, acc), pre_ref, with_first_block=first + extra, may_be_first_tile=qs == 0)

        def finish():
            lp = lam_ref[...]
            lam = (jnp.exp(jnp.sum(lp[0:1] * lp[1:2], axis=-1, keepdims=True))
                   - jnp.exp(jnp.sum(lp[2:3] * lp[3:4], axis=-1, keepdims=True)) + lam_init)

            def weighted(m, scale):
                return acc[m, :PAIR, :] * (scale / acc[m, PAIR:PAIR + 1, :])

            for g in range(PAIRS_PER_STEP):
                o = weighted(2 * g, 1.0) - weighted(2 * g + 1, lam)
                o = o * lax.rsqrt(jnp.mean(o * o, axis=0, keepdims=True) + SUBLN_EPS) * g_ref[...]
                o_ref[0, qs * ATT_TILE:(qs + 1) * ATT_TILE, g * PAIR:(g + 1) * PAIR] = o.T.astype(BF16)

        return sweep, finish

    _query_tiles(tile)


def _moba_kernel(qT_ref, k_ref, vT_ref, bias_ref, o_ref,
                 qz_ref, m_ref, pre_ref, acc_ref, kmean_ref, kgate_ref, selb_ref, *, nk):
    tq = qT_ref.shape[3]
    head_rows = HEAD_DIM + ONES_ROWS

    def tile(qs, qi):
        qz, m_run, acc = qz_ref.at[qs], m_ref.at[qs], acc_ref.at[qs]

        def select_blocks():
            blk = lax.broadcasted_iota(jnp.int32, (nk, tq), 0).astype(F32)
            eligible = blk < qi.astype(F32)
            for m in range(2 * PAIRS_PER_STEP):
                if m % 2 == 0:
                    q_pair = qT_ref[0, qs, (m // 2) * PAIR:(m // 2 + 1) * PAIR, :]
                    gates = jnp.dot(kgate_ref[m // 2], q_pair, preferred_element_type=F32)
                base = (m % 2) * KMEAN_TERMS * nk
                gate = functools.reduce(jnp.add, [gates[base + t * nk:base + (t + 1) * nk]
                                                  for t in range(KMEAN_TERMS)])
                gate = jnp.where(eligible, gate, -jnp.inf)
                picked = jnp.zeros((nk, tq), F32)
                for _ in range(MOBA_TOPK):
                    best = jnp.max(gate, axis=0, keepdims=True)
                    pick = blk == jnp.min(jnp.where(gate == best, blk, float(nk)), axis=0, keepdims=True)
                    picked = jnp.where(pick, 1.0, picked)
                    gate = jnp.where(pick, -jnp.inf, gate)
                selb_ref[m] = jnp.where(eligible, jnp.where(picked > 0.0, 0.0, MASK_VALUE), MASK_VALUE)

        def sweep(extra):
            if qs == 0:
                @pl.when(qi == 0)
                def _():
                    for j in range(nk):
                        kb = k_ref[0, j * MOBA_BLOCK:(j + 1) * MOBA_BLOCK, :].astype(F32)
                        kmean_ref[j:j + 1, :] = jnp.mean(kb, axis=0, keepdims=True)
                    lane = lax.broadcasted_iota(jnp.int32, (nk, PAIR), 1)
                    for g in range(PAIRS_PER_STEP):
                        mean = kmean_ref[:, g * PAIR:(g + 1) * PAIR]
                        for i in range(2):
                            rest = jnp.where((lane >= i * HEAD_DIM) & (lane < (i + 1) * HEAD_DIM), mean, 0.0)
                            for t in range(KMEAN_TERMS):
                                term = rest.astype(BF16)
                                row = (i * KMEAN_TERMS + t) * nk
                                kgate_ref[g, row:row + nk, :] = term
                                rest = rest - term.astype(F32)

                _split_pairs(qT_ref, qs, qz)
                first = [select_blocks]
            else:
                first = [functools.partial(_split_pairs, qT_ref, qs, qz), select_blocks]
            _sweep(qi, k_ref, vT_ref, lambda v, m: v[(m % 2) * head_rows:(m % 2 + 1) * head_rows],
                   lambda m, t, j: (bias_ref[m, t], None if t == 0 else selb_ref[m, pl.ds(j, 1), :]),
                   lambda m, j: (None, selb_ref[m, pl.ds(j, 1), :]),
                   (qz, m_run, acc), pre_ref, with_first_block=first + extra, may_be_first_tile=qs == 0)

        def finish():
            for g in range(PAIRS_PER_STEP):
                o = jnp.concatenate([acc[m, :HEAD_DIM, :] * (1.0 / acc[m, HEAD_DIM:HEAD_DIM + 1, :])
                                     for m in (2 * g, 2 * g + 1)], axis=0)
                o_ref[0, qs * ATT_TILE:(qs + 1) * ATT_TILE, g * PAIR:(g + 1) * PAIR] = o.T.astype(BF16)

        return sweep, finish

    _query_tiles(tile)


def _attention(kind, qT, k, vT, bias, extra, *, lam_init=None):
    B, nk, D, t = qT.shape
    G, S, T = PAIRS_PER_STEP, nk * t, Q_TILES_PER_STEP
    n_maps = 2 * G
    assert D % (G * PAIR) == 0 and nk >= 2 and nk % T == 0 and t == ATT_TILE and QK_AHEAD_PAIR <= QK_AHEAD <= n_maps
    in_specs = [pl.BlockSpec((1, T, G * PAIR, t), lambda b, p, q: (b, q, p, 0)),
                pl.BlockSpec((1, S, G * PAIR), lambda b, p, q: (b, 0, p)),
                pl.BlockSpec((1, G, nk, vT.shape[3], t), lambda b, p, q: (b, p, 0, 0, 0)),
                pl.BlockSpec((n_maps, 2, t, t), lambda b, p, q: (p, 0, 0, 0), pipeline_mode=pl.Buffered(1))]
    scratch = [pltpu.VMEM((T, n_maps, PAIR, t), BF16),
               pltpu.VMEM((T, n_maps, 1, t), F32),
               pltpu.VMEM((max(QK_AHEAD, QK_AHEAD_PAIR), 2, t, t), F32)]
    if kind == "diff":
        body = functools.partial(_diff_kernel, lam_init=lam_init)
        in_specs += [pl.BlockSpec(e.shape, lambda b, p, q: (0, 0)) for e in extra]
        scratch += [pltpu.VMEM((T, n_maps, PAIR + ONES_ROWS, t), F32)]
    else:
        body = functools.partial(_moba_kernel, nk=nk)
        scratch += [pltpu.VMEM((T, n_maps, HEAD_DIM + ONES_ROWS, t), F32),
                    pltpu.VMEM((nk, G * PAIR), F32),
                    pltpu.VMEM((G, 2 * KMEAN_TERMS * nk, PAIR), BF16),
                    pltpu.VMEM((n_maps, nk, t), F32)]
    return pl.pallas_call(
        body,
        grid=(B, D // (G * PAIR), nk // T),
        in_specs=in_specs,
        out_specs=pl.BlockSpec((1, T * t, G * PAIR), lambda b, p, q: (b, q, p)),
        out_shape=jax.ShapeDtypeStruct((B, S, D), BF16),
        scratch_shapes=scratch,
        compiler_params=pltpu.CompilerParams(dimension_semantics=("parallel", "parallel", "arbitrary"),
                                             vmem_limit_bytes=VMEM_LIMIT_BYTES),
        name=kind + "_attention",
    )(qT, k, vT, bias, *extra)


def kernel(x, rel_bias, norm_g, final_norm_g, ffn_w_in, ffn_w_out, diff_w_qkv, diff_lambda,
           diff_subln_g, diff_w_o, moba_w_qkv, moba_w_o):
    B, S, D = x.shape
    depth = norm_g.shape[0]
    assert S % TOKEN_TILE == 0 and D % PAIR == 0 and MOBA_BLOCK == ATT_TILE
    assert rel_bias.shape == (REL_BUCKETS, D // HEAD_DIM)
    bias = _bias_tiles(rel_bias, ATT_TILE)

    order = [(i, j) for i in range(depth) for j in range(2)]
    ffn_weights = {order[0]: (ffn_w_in[0, 0].astype(BF16), ffn_w_out[0, 0].astype(BF16))}

    mixer_stacks = {"diff_qkv": diff_w_qkv, "diff_o": diff_w_o, "moba_qkv": moba_w_qkv, "moba_o": moba_w_o}
    mixer_weights = {}

    def ffn(h, g, member, **kw):
        nxt = order[order.index(member) + 1] if member != order[-1] else None
        cast = [(ffn_w_in, nxt), (ffn_w_out, nxt)] if nxt else []
        riders = [(name, j) for name, stack in mixer_stacks.items() for j in range(stack.shape[0])] \
            if member == order[0] else []
        cast += [(mixer_stacks[name], (j,)) for name, j in riders]
        h, *converted = _ffn(h, g, *ffn_weights.pop(member), (), cast=cast, **kw)
        if nxt:
            ffn_weights[nxt] = (converted[0], converted[1])
        mixer_weights.update(zip(riders, converted[2 if nxt else 0:]))
        return h

    h = x.reshape(B * S, D)
    for i in range(depth):
        g = norm_g[i]
        h = ffn(h, g[0], (i, 0))
        j = i // 2
        if i % 2 == 0:
            qT, k, vT = _qkv_proj(h.reshape(B, S, D), g[1], mixer_weights["diff_qkv", j], PAIR)
            lam_init = 0.8 - 0.6 * math.exp(-0.3 * i)
            g_sub = jnp.broadcast_to((diff_subln_g[j].astype(F32) * (1.0 - lam_init))[:, None], (PAIR, ATT_TILE))
            o = _attention("diff", qT, k, vT, bias, (diff_lambda[j].astype(F32), g_sub), lam_init=lam_init)
            w_o = mixer_weights["diff_o", j]
        else:
            qT, k, vT = _qkv_proj(h.reshape(B, S, D), g[1], mixer_weights["moba_qkv", j], HEAD_DIM)
            o = _attention("moba", qT, k, vT, bias, ())
            w_o = mixer_weights["moba_o", j]
        last = i == depth - 1
        h = ffn(h, g[2], (i, 1), proj=(o.reshape(B * S, D), w_o), final_g=final_norm_g if last else None)
    return h.reshape(B, S, D)
```
